```python
import jax, jax.numpy as jnp
from jax import lax
import numpy as np

D_MODEL = 1024
BATCH = 4
SEQ = 4096
DEPTH = 1

SSM_HEADS = 16
SSM_HEAD_DIM = 64
SSM_WIDTH = SSM_HEADS * SSM_HEAD_DIM
SSM_GROUPS = 2
SSM_STATE = 128
CONV_WIDTH = 4
CHUNK = 128
CONV_DIM = SSM_WIDTH + 2 * SSM_GROUPS * SSM_STATE
ATT_HEADS = 16
ATT_HEAD_DIM = 64
ATT_WIDTH = ATT_HEADS * ATT_HEAD_DIM
Q_BLOCK = 128
MIX_WIDTH = SSM_WIDTH + ATT_WIDTH
D_FF = 4 * D_MODEL
SPLITS = [SSM_WIDTH,
          SSM_WIDTH + CONV_DIM,
          SSM_WIDTH + CONV_DIM + SSM_HEADS,
          SSM_WIDTH + CONV_DIM + SSM_HEADS + ATT_WIDTH,
          SSM_WIDTH + CONV_DIM + SSM_HEADS + 2 * ATT_WIDTH,
          SSM_WIDTH + CONV_DIM + SSM_HEADS + 3 * ATT_WIDTH]
IN_COLS = SPLITS[-1] + ATT_HEADS
DEEPNORM_ALPHA = (2.0 * DEPTH) ** 0.25
DEEPNORM_BETA = (8.0 * DEPTH) ** -0.25
LN_EPS = 1e-5
RMS_EPS = 1e-5

kernel_name = "hymba_ssd_fox_deepnorm_adaln"


def layer_norm(x, g, b):
    xf = x.astype(jnp.float32)
    mu = jnp.mean(xf, axis=-1, keepdims=True)
    var = jnp.mean(jnp.square(xf - mu), axis=-1, keepdims=True)
    return ((xf - mu) * lax.rsqrt(var + LN_EPS) * g + b).astype(x.dtype)


def rms_norm(x, w):
    xf = x.astype(jnp.float32)
    return xf * lax.rsqrt(jnp.mean(xf * xf, axis=-1, keepdims=True) + RMS_EPS) * w


def causal_depthwise_conv(u, w, b):
    out = lax.conv_general_dilated(
        u, w[:, None, :].astype(u.dtype), window_strides=(1,),
        padding=[(CONV_WIDTH - 1, 0)], dimension_numbers=('NWC', 'WIO', 'NWC'),
        feature_group_count=u.shape[-1])
    return out + b


def segsum(a):
    cs = jnp.cumsum(a, axis=-1)
    diff = cs[..., :, None] - cs[..., None, :]
    t = a.shape[-1]
    mask = jnp.tril(jnp.ones((t, t), dtype=bool))
    return jnp.where(mask, diff, -jnp.inf)


def ssd_chunked(xh, dt, A, Bm, Cm):
    b, s, h, p = xh.shape
    g, n = Bm.shape[-2], Bm.shape[-1]
    e = h // g
    nc = s // CHUNK
    xc = (xh.astype(jnp.float32) * dt[..., None]).reshape(b, nc, CHUNK, g, e, p)
    Bc = Bm.astype(jnp.float32).reshape(b, nc, CHUNK, g, n)
    Cc = Cm.astype(jnp.float32).reshape(b, nc, CHUNK, g, n)
    a = (dt * A).reshape(b, nc, CHUNK, g, e).transpose(0, 3, 4, 1, 2)
    a_cs = jnp.cumsum(a, axis=-1)
    decay_in = jnp.exp(segsum(a))
    cb = jnp.einsum('bclgn,bcsgn->bgcls', Cc, Bc)
    scores = cb[:, :, None] * decay_in
    y_diag = jnp.einsum('bgecls,bcsgep->bclgep', scores, xc)
    decay_to_end = jnp.exp(a_cs[..., -1:] - a_cs).transpose(0, 3, 4, 1, 2)
    states = jnp.einsum('bclgn,bclgep->bcgepn', Bc, xc * decay_to_end[..., None])
    chunk_tot = jnp.pad(a_cs[..., -1], [(0, 0), (0, 0), (0, 0), (1, 0)])
    decay_chunk = jnp.exp(segsum(chunk_tot))
    states = jnp.concatenate([jnp.zeros_like(states[:, :1]), states], axis=1)
    new_states = jnp.einsum('bgezc,bcgepn->bzgepn', decay_chunk, states)
    prev_states = new_states[:, :-1]
    decay_out = jnp.exp(a_cs).transpose(0, 3, 4, 1, 2)
    y_off = jnp.einsum('bclgn,bcgepn->bclgep', Cc, prev_states) * decay_out[..., None]
    return (y_diag + y_off).reshape(b, s, h, p)


def forgetting_attention(q, k, v, log_f):
    s, d = q.shape[1], q.shape[-1]
    scale = d ** -0.5
    cum = jnp.cumsum(log_f, axis=1).transpose(0, 2, 1)
    outs = []
    for i in range(s // Q_BLOCK):
        q0, q1 = i * Q_BLOCK, (i + 1) * Q_BLOCK
        logits = jnp.einsum('bqhd,bkhd->bhqk', q[:, q0:q1], k[:, :q1],
                            preferred_element_type=jnp.float32) * scale
        logits = logits + (cum[:, :, q0:q1, None] - cum[:, :, None, :q1])
        mask = jnp.arange(q0, q1)[:, None] >= jnp.arange(q1)[None, :]
        logits = jnp.where(mask, logits, -jnp.inf)
        probs = jax.nn.softmax(logits, axis=-1)
        outs.append(jnp.einsum('bhqk,bkhd->bqhd', probs.astype(v.dtype), v[:, :q1]))
    return jnp.concatenate(outs, axis=1)


def hybrid_mixer(h, w_in, conv_w, conv_b, dt_bias, a_log, d_skip, ssm_norm_w, f_bias,
                 attn_norm_w, w_out):
    b, s, _ = h.shape
    proj = jnp.einsum('bsd,dk->bsk', h, w_in)
    z, xbc, dt_raw, q, k, v, f_raw = jnp.split(proj, SPLITS, axis=-1)
    xbc = jax.nn.silu(causal_depthwise_conv(xbc, conv_w, conv_b))
    xs, Bm, Cm = jnp.split(xbc, [SSM_WIDTH, SSM_WIDTH + SSM_GROUPS * SSM_STATE], axis=-1)
    xs = xs.reshape(b, s, SSM_HEADS, SSM_HEAD_DIM)
    Bm = Bm.reshape(b, s, SSM_GROUPS, SSM_STATE)
    Cm = Cm.reshape(b, s, SSM_GROUPS, SSM_STATE)
    dt = jax.nn.softplus(dt_raw.astype(jnp.float32) + dt_bias)
    A = -jnp.exp(a_log.astype(jnp.float32))
    y = ssd_chunked(xs, dt, A, Bm, Cm) + d_skip[:, None] * xs
    y_ssm = rms_norm(y.reshape(b, s, SSM_WIDTH) * jax.nn.silu(z), ssm_norm_w)
    q = q.reshape(b, s, ATT_HEADS, ATT_HEAD_DIM)
    k = k.reshape(b, s, ATT_HEADS, ATT_HEAD_DIM)
    v = v.reshape(b, s, ATT_HEADS, ATT_HEAD_DIM)
    log_f = jax.nn.log_sigmoid(f_raw.astype(jnp.float32) + f_bias)
    y_att = forgetting_attention(q, k, v, log_f).reshape(b, s, ATT_WIDTH)
    y_att = rms_norm(y_att, attn_norm_w)
    y_mix = jnp.concatenate([y_ssm, y_att.astype(y_ssm.dtype)], axis=-1)
    return jnp.einsum('bsk,kd->bsd', y_mix.astype(h.dtype), w_out)


def setup_inputs(seed: int = 0) -> dict:
    key = jax.random.key(seed)
    ks = jax.random.split(key, 24)
    f32 = jnp.float32
    nrm = lambda k, shape, s: jax.random.normal(k, shape, f32) * s
    dt0 = jnp.exp(jax.random.uniform(ks[6], (DEPTH, SSM_HEADS), f32,
                                     np.log(1e-3).astype(np.float32), np.log(1e-1).astype(np.float32)))
    return {
        "x": nrm(ks[0], (BATCH, SEQ, D_MODEL), 1.0),
        "c": nrm(ks[1], (BATCH, D_MODEL), 1.0),
        "w_ada": nrm(ks[2], (DEPTH, D_MODEL, 6 * D_MODEL), 0.5 * D_MODEL ** -0.5),
        "b_ada": nrm(ks[3], (DEPTH, 6 * D_MODEL), 0.01),
        "w_in": nrm(ks[4], (DEPTH, D_MODEL, IN_COLS), D_MODEL ** -0.5),
        "conv_w": nrm(ks[5], (DEPTH, CONV_WIDTH, CONV_DIM), CONV_WIDTH ** -0.5),
        "conv_b": nrm(ks[7], (DEPTH, CONV_DIM), 0.01),
        "dt_bias": dt0 + jnp.log(-jnp.expm1(-dt0)),
        "a_log": jnp.log(jax.random.uniform(ks[8], (DEPTH, SSM_HEADS), f32, 1.0, 16.0)),
        "d_skip": 1.0 + nrm(ks[9], (DEPTH, SSM_HEADS), 0.1),
        "ssm_norm_w": 1.0 + nrm(ks[10], (DEPTH, SSM_WIDTH), 0.05),
        "f_bias": jax.random.uniform(ks[11], (DEPTH, ATT_HEADS), f32, 1.0, 4.0),
        "attn_norm_w": 1.0 + nrm(ks[12], (DEPTH, ATT_WIDTH), 0.05),
        "w_out": nrm(ks[13], (DEPTH, MIX_WIDTH, D_MODEL), DEEPNORM_BETA * MIX_WIDTH ** -0.5),
        "ln1_g": 1.0 + nrm(ks[14], (DEPTH, D_MODEL), 0.05),
        "ln1_b": nrm(ks[15], (DEPTH, D_MODEL), 0.01),
        "w_ff_in": nrm(ks[16], (DEPTH, D_MODEL, D_FF), D_MODEL ** -0.5),
        "w_ff_out": nrm(ks[17], (DEPTH, D_FF, D_MODEL), DEEPNORM_BETA * D_FF ** -0.5),
        "ln2_g": 1.0 + nrm(ks[18], (DEPTH, D_MODEL), 0.05),
        "ln2_b": nrm(ks[19], (DEPTH, D_MODEL), 0.01),
    }


def reference(x, c, w_ada, b_ada, w_in, conv_w, conv_b, dt_bias, a_log, d_skip, ssm_norm_w,
              f_bias, attn_norm_w, w_out, ln1_g, ln1_b, w_ff_in, w_ff_out, ln2_g, ln2_b):
    c_act = jax.nn.silu(c)
    for l in range(DEPTH):
        mod = jnp.einsum('bd,de->be', c_act, w_ada[l]) + b_ada[l]
        sh1, sc1, g1, sh2, sc2, g2 = [m[:, None, :] for m in jnp.split(mod, 6, axis=-1)]
        h = x * (1.0 + sc1) + sh1
        y = hybrid_mixer(h, w_in[l], conv_w[l], conv_b[l], dt_bias[l], a_log[l], d_skip[l],
                         ssm_norm_w[l], f_bias[l], attn_norm_w[l], w_out[l])
        x = layer_norm(DEEPNORM_ALPHA * x + (1.0 + g1) * y, ln1_g[l], ln1_b[l])
        h = x * (1.0 + sc2) + sh2
        ff = jnp.einsum('bsf,fd->bsd',
                        jnp.square(jax.nn.relu(jnp.einsum('bsd,df->bsf', h, w_ff_in[l]))),
                        w_ff_out[l])
        x = layer_norm(DEEPNORM_ALPHA * x + (1.0 + g2) * ff, ln2_g[l], ln2_b[l])
    return x
```

```python
import functools

import jax
import jax.numpy as jnp
from jax import lax
from jax.experimental import pallas as pl
from jax.experimental.pallas import tpu as pltpu

F32 = jnp.float32
BF16 = jnp.bfloat16
HIGHEST = lax.Precision.HIGHEST

LANES = 128
SUBLANES = 8
VMEM_LIMIT_BYTES = 56 * 1024 * 1024

SSM_HEADS = 16
SSM_HEAD_DIM = 64
SSM_GROUPS = 2
SSM_STATE = 128
CONV_WIDTH = 4
CHUNK = 128
ATT_HEADS = 16
ATT_HEAD_DIM = 64
LN_EPS = 1e-5
RMS_EPS = 1e-5

HEADS_PER_GROUP = SSM_HEADS // SSM_GROUPS
HEADS_PER_BLOCK = LANES // ATT_HEAD_DIM
SMALL_DT = 0
SMALL_F = SSM_HEADS


def _silu(v):
    return v * (1.0 / (1.0 + jnp.exp(-v)))


def _softplus(v):
    return jnp.maximum(v, 0.0) + jnp.log(1.0 + jnp.exp(-jnp.abs(v)))


def _layer_norm(v, g, b):
    mu = jnp.mean(v, axis=-1, keepdims=True)
    d = v - mu
    var = jnp.mean(d * d, axis=-1, keepdims=True)
    return d * lax.rsqrt(var + LN_EPS) * g + b


def _ada_kernel(c_ref, w_ref, b_ref, o_ref):
    ca = _silu(c_ref[...]).astype(BF16)
    o_ref[...] = jnp.dot(ca, w_ref[...].astype(BF16), preferred_element_type=F32) + b_ref[...]


def _ada(c_pad, w, b, tn=512):
    rows, d = c_pad.shape
    n = w.shape[1]
    return pl.pallas_call(
        _ada_kernel,
        grid=(n // tn,),
        in_specs=[pl.BlockSpec((rows, d), lambda j: (0, 0)),
                  pl.BlockSpec((d, tn), lambda j: (0, j)),
                  pl.BlockSpec((1, tn), lambda j: (0, j))],
        out_specs=pl.BlockSpec((rows, tn), lambda j: (0, j)),
        out_shape=jax.ShapeDtypeStruct((rows, n), F32),
        name="ada",
    )(c_pad, w, b)


def _inproj_kernel(x_ref, mod_ref, w_ref, ws_ref, o_ref, os_ref, h_ref):
    @pl.when(pl.program_id(1) == 0)
    def _():
        h = (x_ref[...] * (1.0 + mod_ref[1]) + mod_ref[0]).astype(BF16)
        h_ref[...] = h
        os_ref[...] = jnp.dot(h, ws_ref[...], preferred_element_type=F32)

    o_ref[...] = jnp.dot(h_ref[...], w_ref[...], preferred_element_type=F32).astype(BF16)


def _inproj(x2, mod4, w_main, w_small, seq, tm=1024, tn=512):
    n_tok, d = x2.shape
    n_main = w_main.shape[1]
    tiles_per_seq = seq // tm
    return pl.pallas_call(
        _inproj_kernel,
        grid=(n_tok // tm, n_main // tn),
        in_specs=[pl.BlockSpec((tm, d), lambda i, j: (i, 0)),
                  pl.BlockSpec((None, 6, 1, d), lambda i, j: (i // tiles_per_seq, 0, 0, 0)),
                  pl.BlockSpec((d, tn), lambda i, j: (0, j)),
                  pl.BlockSpec((d, LANES), lambda i, j: (0, 0))],
        out_specs=[pl.BlockSpec((tm, tn), lambda i, j: (i, j)),
                   pl.BlockSpec((tm, LANES), lambda i, j: (i, 0))],
        out_shape=[jax.ShapeDtypeStruct((n_tok, n_main), BF16),
                   jax.ShapeDtypeStruct((n_tok, LANES), F32)],
        scratch_shapes=[pltpu.VMEM((tm, d), BF16)],
        compiler_params=pltpu.CompilerParams(
            dimension_semantics=("arbitrary", "arbitrary"), vmem_limit_bytes=VMEM_LIMIT_BYTES),
        name="inproj",
    )(x2, mod4, w_main, w_small)


def _ssd_kernel(z_ref, x_ref, bc_ref, sm_ref, cw_ref, cb_ref, brow_ref, alog_ref, dsk_ref, nw_ref,
                e_ref, y_ref, cc_ref, cr_ref, xe_ref, st_ref, car_ref):
    width = SSM_HEADS * SSM_HEAD_DIM
    gw = HEADS_PER_GROUP * SSM_HEAD_DIM
    halo = SUBLANES

    @pl.when(pl.program_id(1) == 0)
    def _():
        xe_ref[0:halo, :] = jnp.zeros((halo, xe_ref.shape[1]), F32)
        st_ref[...] = jnp.zeros(st_ref.shape, F32)
        car_ref[...] = jnp.zeros(car_ref.shape, F32)

    xe_ref[halo:halo + CHUNK, 0:width] = x_ref[...].astype(F32)
    xe_ref[halo:halo + CHUNK, width:] = bc_ref[...].astype(F32)
    acc = jnp.broadcast_to(cb_ref[...], (CHUNK, xe_ref.shape[1]))
    for k in range(CONV_WIDTH):
        off = halo - (CONV_WIDTH - 1) + k
        acc = acc + cw_ref[k:k + 1, :] * xe_ref[off:off + CHUNK, :]
    xe_ref[0:halo, :] = xe_ref[CHUNK:CHUNK + halo, :]
    u = _silu(acc)
    xs = u[:, 0:width]
    b_all = u[:, width:width + SSM_GROUPS * SSM_STATE]
    c_all = u[:, width + SSM_GROUPS * SSM_STATE:]

    lane = lax.broadcasted_iota(jnp.int32, (CHUNK, LANES), 1)
    row = lax.broadcasted_iota(jnp.int32, (CHUNK, LANES), 0)
    is_dt = lane < SMALL_F
    is_f = jnp.logical_and(lane >= SMALL_F, lane < SMALL_F + ATT_HEADS)
    t = sm_ref[...] + brow_ref[...]
    sp = _softplus(jnp.where(is_dt, t, -t))
    a_row = -jnp.exp(alog_ref[...])
    dt_tile = jnp.where(is_dt, sp, 0.0)
    val = jnp.where(is_dt, sp * a_row, jnp.where(is_f, -sp, 0.0))
    tril = row >= lane
    cs = jnp.dot(tril.astype(F32), val, precision=HIGHEST, preferred_element_type=F32)

    cs_glob = cs + car_ref[...]
    cc_ref[...] = cs_glob
    cr_ref[...] = cs_glob.T
    car_ref[...] = car_ref[...] + jnp.where(is_f[0:1, :], cs[CHUNK - 1:CHUNK, :], 0.0)

    cs_t = cs.T
    dt_t = dt_tile.T
    full = jnp.dot(jnp.concatenate([cs, dt_tile], axis=0), e_ref[...],
                   precision=HIGHEST, preferred_element_type=F32)
    acs_full = full[0:CHUNK]
    dt_full = full[CHUNK:]
    atot_full = acs_full[CHUNK - 1:CHUNK, :]
    dec_out = jnp.exp(acs_full)
    dec_end = jnp.exp(atot_full - acs_full)
    chunk_dec = jnp.exp(atot_full)
    xs_b = xs.astype(BF16)
    xw = (xs * (dec_end * dt_full)).astype(BF16)
    lane_lo = lane < SSM_HEAD_DIM

    ydiag_blocks = []
    yoff_blocks = []
    for g in range(SSM_GROUPS):
        bg = b_all[:, g * SSM_STATE:(g + 1) * SSM_STATE]
        cg = c_all[:, g * SSM_STATE:(g + 1) * SSM_STATE].astype(BF16)
        cbm = lax.dot_general(cg, bg.astype(BF16), (((1,), (1,)), ((), ())),
                              preferred_element_type=F32)
        for j in range(HEADS_PER_GROUP // HEADS_PER_BLOCK):
            h0 = g * HEADS_PER_GROUP + HEADS_PER_BLOCK * j
            xp = xs_b[:, h0 * SSM_HEAD_DIM:h0 * SSM_HEAD_DIM + LANES]
            res = []
            for h in range(h0, h0 + HEADS_PER_BLOCK):
                seg = cs[:, h:h + 1] - cs_t[h:h + 1, :]
                dec = jnp.exp(jnp.where(tril, seg, -1e30))
                sc = (cbm * dec * dt_t[h:h + 1, :]).astype(BF16)
                res.append(jnp.dot(sc, xp, preferred_element_type=F32))
            ydiag_blocks.append(jnp.where(lane_lo, res[0], res[1]))
        st = st_ref[g]
        yoff_blocks.append(jnp.dot(cg, st.astype(BF16), preferred_element_type=F32)
                           * dec_out[:, g * gw:(g + 1) * gw])
        st_ref[g] = st * chunk_dec[:, g * gw:(g + 1) * gw] + jnp.dot(
            bg.T.astype(BF16), xw[:, g * gw:(g + 1) * gw], preferred_element_type=F32)
    y = (jnp.concatenate(ydiag_blocks, axis=1) + jnp.concatenate(yoff_blocks, axis=1)
         + dsk_ref[...] * xs)
    gated = y * _silu(z_ref[...].astype(F32))
    ms = jnp.mean(gated * gated, axis=-1, keepdims=True)
    y_ref[...] = (gated * lax.rsqrt(ms + RMS_EPS) * nw_ref[...]).astype(BF16)


def _ssd(proj, small, conv_w, conv_b, brow, alog_row, dsk_full, norm_w, expand, batch, seq):
    n_tok = proj.shape[0]
    width = SSM_HEADS * SSM_HEAD_DIM
    bc_w = 2 * SSM_GROUPS * SSM_STATE
    conv_dim = width + bc_w
    nc = seq // CHUNK
    tok = lambda b, c: (b * nc + c, 0)
    const = lambda b, c: (0, 0)
    return pl.pallas_call(
        _ssd_kernel,
        grid=(batch, nc),
        in_specs=[pl.BlockSpec((CHUNK, width), tok),
                  pl.BlockSpec((CHUNK, width), lambda b, c: (b * nc + c, 1)),
                  pl.BlockSpec((CHUNK, bc_w), lambda b, c: (b * nc + c, 2 * width // bc_w)),
                  pl.BlockSpec((CHUNK, LANES), tok),
                  pl.BlockSpec((CONV_WIDTH, conv_dim), const),
                  pl.BlockSpec((1, conv_dim), const),
                  pl.BlockSpec((1, LANES), const),
                  pl.BlockSpec((1, LANES), const),
                  pl.BlockSpec((1, width), const),
                  pl.BlockSpec((1, width), const),
                  pl.BlockSpec((LANES, width), const)],
        out_specs=[pl.BlockSpec((CHUNK, width), tok),
                   pl.BlockSpec((CHUNK, LANES), tok),
                   pl.BlockSpec((None, LANES, CHUNK), lambda b, c: (b, 0, c))],
        out_shape=[jax.ShapeDtypeStruct((n_tok, width), BF16),
                   jax.ShapeDtypeStruct((n_tok, LANES), F32),
                   jax.ShapeDtypeStruct((batch, LANES, seq), F32)],
        scratch_shapes=[pltpu.VMEM((CHUNK + SUBLANES, conv_dim), F32),
                        pltpu.VMEM((SSM_GROUPS, SSM_STATE, HEADS_PER_GROUP * SSM_HEAD_DIM), F32),
                        pltpu.VMEM((1, LANES), F32)],
        compiler_params=pltpu.CompilerParams(
            dimension_semantics=("arbitrary", "arbitrary"), vmem_limit_bytes=VMEM_LIMIT_BYTES),
        name="ssd",
    )(proj, proj, proj, small, conv_w, conv_b, brow, alog_row, dsk_full, norm_w, expand)


def _attn_kernel(q_ref, k_ref, v_ref, cc_ref, cr_ref, o_ref, *, tq):
    pair = pl.program_id(1)
    qi = pl.program_id(2)
    lane = lax.broadcasted_iota(jnp.int32, (tq, LANES), 1)
    lane_lo = lane < ATT_HEAD_DIM
    causal = (lax.broadcasted_iota(jnp.int32, (tq, tq), 0)
              >= lax.broadcasted_iota(jnp.int32, (tq, tq), 1))
    qs = q_ref[...] * (ATT_HEAD_DIM ** -0.5)
    cc = cc_ref[...]
    outs = []
    for i in range(HEADS_PER_BLOCK):
        qh = jnp.where(lane_lo if i == 0 else jnp.logical_not(lane_lo), qs, 0.0).astype(BF16)
        head_lane = SMALL_F + HEADS_PER_BLOCK * pair + i
        cq = jnp.sum(jnp.where(lane == head_lane, cc, 0.0), axis=1, keepdims=True)

        def step(ki, carry, masked, qh=qh, cq=cq, i=i):
            m, l, acc = carry
            start = pl.multiple_of(ki * tq, tq)
            k = k_ref[pl.ds(start, tq), :]
            v = v_ref[pl.ds(start, tq), :]
            ck = cr_ref[i, :, pl.ds(start, tq)]
            s = lax.dot_general(qh, k, (((1,), (1,)), ((), ())), preferred_element_type=F32)
            s = s + (cq - ck)
            if masked:
                s = jnp.where(causal, s, -1e30)
            m_new = jnp.maximum(m, jnp.max(s, axis=1, keepdims=True))
            alpha = jnp.exp(m - m_new)
            p = jnp.exp(s - m_new)
            l = alpha * l + jnp.sum(p, axis=1, keepdims=True)
            acc = alpha * acc + jnp.dot(p.astype(BF16), v, preferred_element_type=F32)
            return m_new, l, acc

        init = (jnp.full((tq, 1), -1e30, F32), jnp.zeros((tq, 1), F32), jnp.zeros((tq, LANES), F32))
        carry = lax.fori_loop(0, qi, functools.partial(step, masked=False), init)
        _, l, acc = step(qi, carry, masked=True)
        outs.append(acc / l)
    o_ref[...] = jnp.where(lane_lo, outs[0], outs[1]).astype(BF16)


def _attn(proj, cum_col, cum_row, batch, seq, q_col, k_col, v_col, tq=256):
    n_tok = proj.shape[0]
    nq = seq // tq
    n_pairs = ATT_HEADS // HEADS_PER_BLOCK
    return pl.pallas_call(
        functools.partial(_attn_kernel, tq=tq),
        grid=(batch, n_pairs, nq),
        in_specs=[pl.BlockSpec((tq, LANES), lambda b, p, i: (b * nq + i, q_col + p)),
                  pl.BlockSpec((seq, LANES), lambda b, p, i: (b, k_col + p)),
                  pl.BlockSpec((seq, LANES), lambda b, p, i: (b, v_col + p)),
                  pl.BlockSpec((tq, LANES), lambda b, p, i: (b * nq + i, 0)),
                  pl.BlockSpec((None, HEADS_PER_BLOCK, 1, seq), lambda b, p, i: (b, p, 0, 0))],
        out_specs=pl.BlockSpec((tq, LANES), lambda b, p, i: (b * nq + i, p)),
        out_shape=jax.ShapeDtypeStruct((n_tok, ATT_HEADS * ATT_HEAD_DIM), BF16),
        compiler_params=pltpu.CompilerParams(
            dimension_semantics=("arbitrary", "arbitrary", "arbitrary"),
            vmem_limit_bytes=VMEM_LIMIT_BYTES),
        name="attn",
    )(proj, proj, proj, cum_col, cum_row)


def _tail_kernel(x_ref, ys_ref, ya_ref, mod_ref, wo_ref, anw_ref, g1_ref, b1_ref, w1_ref, w2_ref,
                 g2_ref, b2_ref, o_ref, *, alpha, ff_chunk):
    ssm_w = ys_ref.shape[1]
    ya = ya_ref[...].astype(F32)
    ya = ya * lax.rsqrt(jnp.mean(ya * ya, axis=-1, keepdims=True) + RMS_EPS) * anw_ref[...]
    y = (jnp.dot(ys_ref[...], wo_ref[0:ssm_w, :], preferred_element_type=F32)
         + jnp.dot(ya.astype(BF16), wo_ref[ssm_w:, :], preferred_element_type=F32))
    x1 = _layer_norm(alpha * x_ref[...] + (1.0 + mod_ref[2]) * y, g1_ref[...], b1_ref[...])
    h = (x1 * (1.0 + mod_ref[4]) + mod_ref[3]).astype(BF16)
    ff = jnp.zeros(x1.shape, F32)
    for c in range(w1_ref.shape[1] // ff_chunk):
        a = jnp.dot(h, w1_ref[:, c * ff_chunk:(c + 1) * ff_chunk], preferred_element_type=F32)
        a = jnp.maximum(a, 0.0)
        ff = ff + jnp.dot((a * a).astype(BF16), w2_ref[c * ff_chunk:(c + 1) * ff_chunk, :],
                          preferred_element_type=F32)
    o_ref[...] = _layer_norm(alpha * x1 + (1.0 + mod_ref[5]) * ff, g2_ref[...], b2_ref[...])


def _tail(x2, y_ssm, y_att, mod4, w_out, anw, g1, b1, w1, w2, g2, b2, seq, alpha, tm=512,
          ff_chunk=1024):
    n_tok, d = x2.shape
    mix = w_out.shape[0]
    d_ff = w1.shape[1]
    tiles_per_seq = seq // tm
    tok = lambda i: (i, 0)
    const = lambda i: (0, 0)
    resident = functools.partial(pl.BlockSpec, index_map=const, pipeline_mode=pl.Buffered(1))
    return pl.pallas_call(
        functools.partial(_tail_kernel, alpha=alpha, ff_chunk=ff_chunk),
        grid=(n_tok // tm,),
        in_specs=[pl.BlockSpec((tm, d), tok),
                  pl.BlockSpec((tm, y_ssm.shape[1]), tok),
                  pl.BlockSpec((tm, y_att.shape[1]), tok),
                  pl.BlockSpec((None, 6, 1, d), lambda i: (i // tiles_per_seq, 0, 0, 0)),
                  resident((mix, d)),
                  pl.BlockSpec((1, y_att.shape[1]), const),
                  pl.BlockSpec((1, d), const),
                  pl.BlockSpec((1, d), const),
                  resident((d, d_ff)),
                  resident((d_ff, d)),
                  pl.BlockSpec((1, d), const),
                  pl.BlockSpec((1, d), const)],
        out_specs=pl.BlockSpec((tm, d), tok),
        out_shape=jax.ShapeDtypeStruct((n_tok, d), F32),
        compiler_params=pltpu.CompilerParams(
            dimension_semantics=("arbitrary",), vmem_limit_bytes=VMEM_LIMIT_BYTES),
        name="tail",
    )(x2, y_ssm, y_att, mod4, w_out, anw, g1, b1, w1, w2, g2, b2)


def _pad_lanes(v):
    return jnp.pad(v, ((0, 0), (0, LANES - v.shape[1])))


def kernel(x, c, w_ada, b_ada, w_in, conv_w, conv_b, dt_bias, a_log, d_skip, ssm_norm_w, f_bias,
           attn_norm_w, w_out, ln1_g, ln1_b, w_ff_in, w_ff_out, ln2_g, ln2_b):
    batch, seq, d = x.shape
    depth = w_ada.shape[0]
    alpha = (2.0 * depth) ** 0.25
    ssm_w = SSM_HEADS * SSM_HEAD_DIM
    att_w = ATT_HEADS * ATT_HEAD_DIM
    conv_dim = ssm_w + 2 * SSM_GROUPS * SSM_STATE
    o_xbc = ssm_w
    o_dt = o_xbc + conv_dim
    o_q = o_dt + SSM_HEADS
    o_k = o_q + att_w
    o_v = o_k + att_w
    o_f = o_v + att_w
    q_col = (ssm_w + conv_dim) // LANES
    k_col = q_col + att_w // LANES
    v_col = k_col + att_w // LANES
    expand = (jnp.arange(LANES)[:, None] == (jnp.arange(ssm_w)[None, :] // SSM_HEAD_DIM)).astype(F32)

    x2 = x.reshape(batch * seq, d)
    c_pad = jnp.pad(c, ((0, SUBLANES - batch % SUBLANES if batch % SUBLANES else 0), (0, 0)))
    for l in range(depth):
        mod = _ada(c_pad, w_ada[l], b_ada[l][None, :])
        mod4 = mod[:batch].reshape(batch, 6, 1, d)
        wl = w_in[l]
        w_main = jnp.concatenate([wl[:, :o_dt], wl[:, o_q:o_f]], axis=1).astype(BF16)
        w_small = _pad_lanes(jnp.concatenate([wl[:, o_dt:o_q], wl[:, o_f:]], axis=1)).astype(BF16)
        proj, small = _inproj(x2, mod4, w_main, w_small, seq)

        brow = _pad_lanes(jnp.concatenate([dt_bias[l], f_bias[l]])[None, :])
        alog_row = _pad_lanes(a_log[l][None, :])
        dsk_full = jnp.repeat(d_skip[l], SSM_HEAD_DIM)[None, :]
        y_ssm, cum_col, cum_row = _ssd(proj, small, conv_w[l], conv_b[l][None, :], brow, alog_row,
                                       dsk_full, ssm_norm_w[l][None, :], expand, batch, seq)
        cum_row = cum_row[:, SMALL_F:SMALL_F + ATT_HEADS, :].reshape(batch, ATT_HEADS, 1, seq)
        y_att = _attn(proj, cum_col, cum_row, batch, seq, q_col, k_col, v_col)

        x2 = _tail(x2, y_ssm, y_att, mod4, w_out[l].astype(BF16), attn_norm_w[l][None, :],
                   ln1_g[l][None, :], ln1_b[l][None, :], w_ff_in[l].astype(BF16),
                   w_ff_out[l].astype(BF16), ln2_g[l][None, :], ln2_b[l][None, :], seq, alpha)
    return x2.reshape(batch, seq, d)
```

```python
import functools

import jax
import jax.numpy as jnp
from jax import lax
from jax.experimental import pallas as pl
from jax.experimental.pallas import tpu as pltpu

F32 = jnp.float32
BF16 = jnp.bfloat16
HIGHEST = lax.Precision.HIGHEST

LANES = 128
SUBLANES = 8
VMEM_LIMIT_BYTES = 56 * 1024 * 1024

SSM_HEADS = 16
SSM_HEAD_DIM = 64
SSM_GROUPS = 2
SSM_STATE = 128
CONV_WIDTH = 4
CHUNK = 128
ATT_HEADS = 16
ATT_HEAD_DIM = 64
LN_EPS = 1e-5
RMS_EPS = 1e-5

HEADS_PER_GROUP = SSM_HEADS // SSM_GROUPS
HEADS_PER_BLOCK = LANES // ATT_HEAD_DIM
SMALL_DT = 0
SMALL_F = SSM_HEADS


def _silu(v):
    return v * (1.0 / (1.0 + jnp.exp(-v)))


def _softplus(v):
    return jnp.maximum(v, 0.0) + jnp.log(1.0 + jnp.exp(-jnp.abs(v)))


def _layer_norm(v, g, b):
    mu = jnp.mean(v, axis=-1, keepdims=True)
    d = v - mu
    var = jnp.mean(d * d, axis=-1, keepdims=True)
    return d * lax.rsqrt(var + LN_EPS) * g + b


def _ada_kernel(c_ref, w_ref, b_ref, o_ref):
    ca = _silu(c_ref[...]).astype(BF16)
    o_ref[...] = jnp.dot(ca, w_ref[...].astype(BF16), preferred_element_type=F32) + b_ref[...]


def _ada(c_pad, w, b, tn=512):
    rows, d = c_pad.shape
    n = w.shape[1]
    return pl.pallas_call(
        _ada_kernel,
        grid=(n // tn,),
        in_specs=[pl.BlockSpec((rows, d), lambda j: (0, 0)),
                  pl.BlockSpec((d, tn), lambda j: (0, j)),
                  pl.BlockSpec((1, tn), lambda j: (0, j))],
        out_specs=pl.BlockSpec((rows, tn), lambda j: (0, j)),
        out_shape=jax.ShapeDtypeStruct((rows, n), F32),
        name="ada",
    )(c_pad, w, b)


def _inproj_kernel(x_ref, mod_ref, w_ref, ws_ref, cs_ref, o_ref, os_ref, h_ref):
    @pl.when(pl.program_id(1) == 0)
    def _():
        h = (x_ref[...] * (1.0 + mod_ref[1]) + mod_ref[0]).astype(BF16)
        h_ref[...] = h
        os_ref[...] = jnp.dot(h, ws_ref[...], preferred_element_type=F32)

    acc = jnp.dot(h_ref[...], w_ref[...], preferred_element_type=F32)
    o_ref[...] = (acc * cs_ref[...]).astype(BF16)


def _inproj(x2, mod4, w_main, w_small, col_scale, seq, tm=1024, tn=512):
    n_tok, d = x2.shape
    n_main = w_main.shape[1]
    tiles_per_seq = seq // tm
    return pl.pallas_call(
        _inproj_kernel,
        grid=(n_tok // tm, n_main // tn),
        in_specs=[pl.BlockSpec((tm, d), lambda i, j: (i, 0)),
                  pl.BlockSpec((None, 6, 1, d), lambda i, j: (i // tiles_per_seq, 0, 0, 0)),
                  pl.BlockSpec((d, tn), lambda i, j: (0, j)),
                  pl.BlockSpec((d, LANES), lambda i, j: (0, 0)),
                  pl.BlockSpec((1, tn), lambda i, j: (0, j))],
        out_specs=[pl.BlockSpec((tm, tn), lambda i, j: (i, j)),
                   pl.BlockSpec((tm, LANES), lambda i, j: (i, 0))],
        out_shape=[jax.ShapeDtypeStruct((n_tok, n_main), BF16),
                   jax.ShapeDtypeStruct((n_tok, LANES), F32)],
        scratch_shapes=[pltpu.VMEM((tm, d), BF16)],
        compiler_params=pltpu.CompilerParams(
            dimension_semantics=("arbitrary", "arbitrary"), vmem_limit_bytes=VMEM_LIMIT_BYTES),
        name="inproj",
    )(x2, mod4, w_main, w_small, col_scale)


def _ssd_kernel(z_ref, x_ref, bc_ref, sm_ref, cw_ref, cb_ref, brow_ref, alog_ref, dsk_ref, nw_ref,
                e_ref, y_ref, cc_ref, xe_ref, st_ref, car_ref):
    width = SSM_HEADS * SSM_HEAD_DIM
    gw = HEADS_PER_GROUP * SSM_HEAD_DIM
    halo = SUBLANES

    @pl.when(pl.program_id(1) == 0)
    def _():
        xe_ref[0:halo, :] = jnp.zeros((halo, xe_ref.shape[1]), F32)
        st_ref[...] = jnp.zeros(st_ref.shape, F32)
        car_ref[...] = jnp.zeros(car_ref.shape, F32)

    xe_ref[halo:halo + CHUNK, 0:width] = x_ref[...].astype(F32)
    xe_ref[halo:halo + CHUNK, width:] = bc_ref[...].astype(F32)
    acc = jnp.broadcast_to(cb_ref[...], (CHUNK, xe_ref.shape[1]))
    for k in range(CONV_WIDTH):
        off = halo - (CONV_WIDTH - 1) + k
        acc = acc + cw_ref[k:k + 1, :] * xe_ref[off:off + CHUNK, :]
    xe_ref[0:halo, :] = xe_ref[CHUNK:CHUNK + halo, :]
    u = _silu(acc)
    xs = u[:, 0:width]
    b_all = u[:, width:width + SSM_GROUPS * SSM_STATE]
    c_all = u[:, width + SSM_GROUPS * SSM_STATE:]

    lane = lax.broadcasted_iota(jnp.int32, (CHUNK, LANES), 1)
    row = lax.broadcasted_iota(jnp.int32, (CHUNK, LANES), 0)
    is_dt = lane < SMALL_F
    is_f = jnp.logical_and(lane >= SMALL_F, lane < SMALL_F + ATT_HEADS)
    t = sm_ref[...] + brow_ref[...]
    sp = _softplus(jnp.where(is_dt, t, -t))
    a_row = -jnp.exp(alog_ref[...])
    dt_tile = jnp.where(is_dt, sp, 0.0)
    val = jnp.where(is_dt, sp * a_row, jnp.where(is_f, -sp, 0.0))
    tril = row >= lane
    cs = jnp.dot(tril.astype(F32), val, precision=HIGHEST, preferred_element_type=F32)

    cc_ref[...] = cs + car_ref[...]
    car_ref[...] = car_ref[...] + jnp.where(is_f[0:1, :], cs[CHUNK - 1:CHUNK, :], 0.0)

    cs_t = cs.T
    dt_t = dt_tile.T
    full = jnp.dot(jnp.concatenate([cs, dt_tile], axis=0), e_ref[...],
                   precision=HIGHEST, preferred_element_type=F32)
    acs_full = full[0:CHUNK]
    dt_full = full[CHUNK:]
    atot_full = acs_full[CHUNK - 1:CHUNK, :]
    dec_out = jnp.exp(acs_full)
    dec_end = jnp.exp(atot_full - acs_full)
    chunk_dec = jnp.exp(atot_full)
    xs_b = xs.astype(BF16)
    xw = (xs * (dec_end * dt_full)).astype(BF16)
    lane_lo = lane < SSM_HEAD_DIM

    ydiag_blocks = []
    yoff_blocks = []
    for g in range(SSM_GROUPS):
        bg = b_all[:, g * SSM_STATE:(g + 1) * SSM_STATE]
        cg = c_all[:, g * SSM_STATE:(g + 1) * SSM_STATE].astype(BF16)
        cbm = lax.dot_general(cg, bg.astype(BF16), (((1,), (1,)), ((), ())),
                              preferred_element_type=F32)
        for j in range(HEADS_PER_GROUP // HEADS_PER_BLOCK):
            h0 = g * HEADS_PER_GROUP + HEADS_PER_BLOCK * j
            xp = xs_b[:, h0 * SSM_HEAD_DIM:h0 * SSM_HEAD_DIM + LANES]
            res = []
            for h in range(h0, h0 + HEADS_PER_BLOCK):
                seg = cs[:, h:h + 1] - cs_t[h:h + 1, :]
                dec = jnp.exp(jnp.where(tril, seg, -1e30))
                sc = (cbm * dec * dt_t[h:h + 1, :]).astype(BF16)
                res.append(jnp.dot(sc, xp, preferred_element_type=F32))
            ydiag_blocks.append(jnp.where(lane_lo, res[0], res[1]))
        st = st_ref[g]
        yoff_blocks.append(jnp.dot(cg, st.astype(BF16), preferred_element_type=F32)
                           * dec_out[:, g * gw:(g + 1) * gw])
        st_ref[g] = st * chunk_dec[:, g * gw:(g + 1) * gw] + jnp.dot(
            bg.T.astype(BF16), xw[:, g * gw:(g + 1) * gw], preferred_element_type=F32)
    y = (jnp.concatenate(ydiag_blocks, axis=1) + jnp.concatenate(yoff_blocks, axis=1)
         + dsk_ref[...] * xs)
    gated = y * _silu(z_ref[...].astype(F32))
    ms = jnp.mean(gated * gated, axis=-1, keepdims=True)
    y_ref[...] = (gated * lax.rsqrt(ms + RMS_EPS) * nw_ref[...]).astype(BF16)


def _ssd(proj, small, conv_w, conv_b, brow, alog_row, dsk_full, norm_w, expand, batch, seq):
    n_tok = proj.shape[0]
    width = SSM_HEADS * SSM_HEAD_DIM
    bc_w = 2 * SSM_GROUPS * SSM_STATE
    conv_dim = width + bc_w
    nc = seq // CHUNK
    tok = lambda b, c: (b * nc + c, 0)
    const = lambda b, c: (0, 0)
    return pl.pallas_call(
        _ssd_kernel,
        grid=(batch, nc),
        in_specs=[pl.BlockSpec((CHUNK, width), tok),
                  pl.BlockSpec((CHUNK, width), lambda b, c: (b * nc + c, 1)),
                  pl.BlockSpec((CHUNK, bc_w), lambda b, c: (b * nc + c, 2 * width // bc_w)),
                  pl.BlockSpec((CHUNK, LANES), tok),
                  pl.BlockSpec((CONV_WIDTH, conv_dim), const),
                  pl.BlockSpec((1, conv_dim), const),
                  pl.BlockSpec((1, LANES), const),
                  pl.BlockSpec((1, LANES), const),
                  pl.BlockSpec((1, width), const),
                  pl.BlockSpec((1, width), const),
                  pl.BlockSpec((LANES, width), const)],
        out_specs=[pl.BlockSpec((CHUNK, width), tok),
                   pl.BlockSpec((CHUNK, LANES), tok)],
        out_shape=[jax.ShapeDtypeStruct((n_tok, width), BF16),
                   jax.ShapeDtypeStruct((n_tok, LANES), F32)],
        scratch_shapes=[pltpu.VMEM((CHUNK + SUBLANES, conv_dim), F32),
                        pltpu.VMEM((SSM_GROUPS, SSM_STATE, HEADS_PER_GROUP * SSM_HEAD_DIM), F32),
                        pltpu.VMEM((1, LANES), F32)],
        compiler_params=pltpu.CompilerParams(
            dimension_semantics=("arbitrary", "arbitrary"), vmem_limit_bytes=VMEM_LIMIT_BYTES),
        name="ssd",
    )(proj, proj, proj, small, conv_w, conv_b, brow, alog_row, dsk_full, norm_w, expand)


ATT_TK = 256
ATT_SUB = 256
BF16_SUBLANES = 16
ATT_VT_ROWS = ATT_HEAD_DIM + BF16_SUBLANES
LOG2E = 1.4426950408889634


def _split3(c):
    hi = c.astype(BF16).astype(F32)
    r = c - hi
    mid = r.astype(BF16).astype(F32)
    return hi, mid, r - mid


def _bias_features(c_col, lane, first_lane, key_side):
    hi, mid, lo = _split3(c_col)
    one = jnp.ones_like(c_col)
    parts = [one, one, one, -hi, -mid, -lo] if key_side else [hi, mid, lo, one, one, one]
    out = jnp.zeros(lane.shape, F32)
    for t, part in enumerate(parts):
        out = jnp.where(lane == first_lane + t, part, out)
    return out


def _head_column(cc, lane, head_lane):
    return jnp.sum(jnp.where(lane == head_lane, cc, 0.0), axis=1, keepdims=True)


def _attn_kernel(q_ref, k_ref, v_ref, ccq_ref, cck_ref, o_ref, kaug_ref, vt_ref, st_ref, acc_ref,
                 m_ref, *, tq):
    pair = pl.program_id(1)
    qi = pl.program_id(2)
    seq = k_ref.shape[0]
    n_sub = tq // ATT_SUB
    assert n_sub % 2 == 0
    hd = ATT_HEAD_DIM
    feat_lane = [hd, 0]

    def head_lanes(lane, i):
        return (lane < hd) if i == 0 else (lane >= hd)

    @pl.when(qi == 0)
    def _():
        def prep(r, carry):
            start = pl.multiple_of(r * ATT_TK, ATT_TK)
            rows = pl.ds(start, ATT_TK)
            lane = lax.broadcasted_iota(jnp.int32, (ATT_TK, LANES), 1)
            kf = k_ref[rows, :].astype(F32)
            cc = cck_ref[rows, :] * LOG2E
            vt = v_ref[rows, :].astype(F32).T.astype(BF16)
            for i in range(HEADS_PER_BLOCK):
                ck = _head_column(cc, lane, SMALL_F + HEADS_PER_BLOCK * pair + i)
                feat = _bias_features(ck, lane, feat_lane[i], key_side=True)
                kaug_ref[i, rows, :] = jnp.where(head_lanes(lane, i), kf, feat).astype(BF16)
                vt_ref[i, 0:hd, rows] = vt[i * hd:(i + 1) * hd, :]
                vt_ref[i, hd:, rows] = jnp.ones((ATT_VT_ROWS - hd, ATT_TK), BF16)
            return carry

        lax.fori_loop(0, seq // ATT_TK, prep, 0)

    lane = lax.broadcasted_iota(jnp.int32, (tq, LANES), 1)
    qf = q_ref[...].astype(F32)
    ccq = ccq_ref[...] * LOG2E
    q_aug = []
    for i in range(HEADS_PER_BLOCK):
        cq = _head_column(ccq, lane, SMALL_F + HEADS_PER_BLOCK * pair + i)
        feat = _bias_features(cq, lane, feat_lane[i], key_side=False)
        q_aug.append(jnp.where(head_lanes(lane, i), qf, feat).astype(BF16))
    causal = (lax.broadcasted_iota(jnp.int32, (ATT_TK, ATT_SUB), 0)
              <= lax.broadcasted_iota(jnp.int32, (ATT_TK, ATT_SUB), 1))
    chains = [(i, s) for i in range(HEADS_PER_BLOCK) for s in range(n_sub)]

    def scores(c, block):
        i, s = chains[c]
        start = pl.multiple_of(block * ATT_TK, ATT_TK)
        kb = kaug_ref[i, pl.ds(start, ATT_TK), :]
        qa = q_aug[i][s * ATT_SUB:(s + 1) * ATT_SUB, :]
        return lax.dot_general(kb, qa, (((1,), (1,)), ((), ())), preferred_element_type=F32)

    def update(c, block, st, masked):
        if masked:
            st = jnp.where(causal, st, -1e30)
        m = m_ref[c]
        m_new = jnp.maximum(m, jnp.max(st, axis=0, keepdims=True))
        pt = jnp.exp2(st - m_new).astype(BF16)
        start = pl.multiple_of(block * ATT_TK, ATT_TK)
        vt = vt_ref[chains[c][0], :, pl.ds(start, ATT_TK)]
        acc_ref[c] = (jnp.exp2(m - m_new) * acc_ref[c]
                      + jnp.dot(vt, pt, preferred_element_type=F32))
        m_ref[c] = m_new

    every = list(range(len(chains)))
    for c in every:
        m_ref[c] = jnp.full((1, ATT_SUB), -1e30, F32)
        acc_ref[c] = jnp.zeros((ATT_VT_ROWS, ATT_SUB), F32)
        st_ref[0, c] = scores(c, 0)

    def half_step(block, slot):
        for c in every:
            st_ref[1 - slot, c] = scores(c, block + 1)
        for c in every:
            update(c, block, st_ref[slot, c], False)

    def body(j, carry):
        half_step(2 * j, 0)
        half_step(2 * j + 1, 1)
        return carry

    first_diag = qi * n_sub
    lax.fori_loop(0, qi * (n_sub // 2), body, 0)
    for extra in range(n_sub):
        live = [c for c in every if chains[c][1] >= extra]
        sts = {c: st_ref[0, c] if extra == 0 else scores(c, first_diag + extra) for c in live}
        for c in live:
            update(c, first_diag + extra, sts[c], chains[c][1] == extra)
    heads = [[None] * n_sub for _ in range(HEADS_PER_BLOCK)]
    for c, (i, s) in enumerate(chains):
        acc = acc_ref[c]
        heads[i][s] = acc[0:hd, :] * (1.0 / acc[hd:hd + 1, :])
    o_t = jnp.concatenate([jnp.concatenate(h, axis=1) for h in heads], axis=0)
    o_ref[...] = o_t.T.astype(BF16)


def _attn(proj, cum_col, batch, seq, q_col, k_col, v_col, tq=512):
    n_tok = proj.shape[0]
    nq = seq // tq
    n_pairs = ATT_HEADS // HEADS_PER_BLOCK
    n_chains = HEADS_PER_BLOCK * (tq // ATT_SUB)
    return pl.pallas_call(
        functools.partial(_attn_kernel, tq=tq),
        grid=(batch, n_pairs, nq),
        in_specs=[pl.BlockSpec((tq, LANES), lambda b, p, i: (b * nq + i, q_col + p)),
                  pl.BlockSpec((seq, LANES), lambda b, p, i: (b, k_col + p)),
                  pl.BlockSpec((seq, LANES), lambda b, p, i: (b, v_col + p)),
                  pl.BlockSpec((tq, LANES), lambda b, p, i: (b * nq + i, 0)),
                  pl.BlockSpec((seq, LANES), lambda b, p, i: (b, 0))],
        out_specs=pl.BlockSpec((tq, LANES), lambda b, p, i: (b * nq + i, p)),
        out_shape=jax.ShapeDtypeStruct((n_tok, ATT_HEADS * ATT_HEAD_DIM), BF16),
        scratch_shapes=[pltpu.VMEM((HEADS_PER_BLOCK, seq, LANES), BF16),
                        pltpu.VMEM((HEADS_PER_BLOCK, ATT_VT_ROWS, seq), BF16),
                        pltpu.VMEM((2, n_chains, ATT_TK, ATT_SUB), F32),
                        pltpu.VMEM((n_chains, ATT_VT_ROWS, ATT_SUB), F32),
                        pltpu.VMEM((n_chains, 1, ATT_SUB), F32)],
        compiler_params=pltpu.CompilerParams(
            dimension_semantics=("arbitrary", "arbitrary", "arbitrary"),
            vmem_limit_bytes=VMEM_LIMIT_BYTES),
        name="attn",
    )(proj, proj, proj, cum_col, cum_col)


def _tail_kernel(x_ref, ys_ref, ya_ref, mod_ref, wo_ref, anw_ref, g1_ref, b1_ref, w1_ref, w2_ref,
                 g2_ref, b2_ref, o_ref, *, alpha, ff_chunk):
    ssm_w = ys_ref.shape[1]
    ya = ya_ref[...].astype(F32)
    ya = ya * lax.rsqrt(jnp.mean(ya * ya, axis=-1, keepdims=True) + RMS_EPS) * anw_ref[...]
    y = (jnp.dot(ys_ref[...], wo_ref[0:ssm_w, :], preferred_element_type=F32)
         + jnp.dot(ya.astype(BF16), wo_ref[ssm_w:, :], preferred_element_type=F32))
    x1 = _layer_norm(alpha * x_ref[...] + (1.0 + mod_ref[2]) * y, g1_ref[...], b1_ref[...])
    h = (x1 * (1.0 + mod_ref[4]) + mod_ref[3]).astype(BF16)
    ff = jnp.zeros(x1.shape, F32)
    for c in range(w1_ref.shape[1] // ff_chunk):
        a = jnp.dot(h, w1_ref[:, c * ff_chunk:(c + 1) * ff_chunk], preferred_element_type=F32)
        a = jnp.maximum(a, 0.0)
        ff = ff + jnp.dot((a * a).astype(BF16), w2_ref[c * ff_chunk:(c + 1) * ff_chunk, :],
                          preferred_element_type=F32)
    o_ref[...] = _layer_norm(alpha * x1 + (1.0 + mod_ref[5]) * ff, g2_ref[...], b2_ref[...])


def _tail(x2, y_ssm, y_att, mod4, w_out, anw, g1, b1, w1, w2, g2, b2, seq, alpha, tm=512,
          ff_chunk=1024):
    n_tok, d = x2.shape
    mix = w_out.shape[0]
    d_ff = w1.shape[1]
    tiles_per_seq = seq // tm
    tok = lambda i: (i, 0)
    const = lambda i: (0, 0)
    resident = functools.partial(pl.BlockSpec, index_map=const, pipeline_mode=pl.Buffered(1))
    return pl.pallas_call(
        functools.partial(_tail_kernel, alpha=alpha, ff_chunk=ff_chunk),
        grid=(n_tok // tm,),
        in_specs=[pl.BlockSpec((tm, d), tok),
                  pl.BlockSpec((tm, y_ssm.shape[1]), tok),
                  pl.BlockSpec((tm, y_att.shape[1]), tok),
                  pl.BlockSpec((None, 6, 1, d), lambda i: (i // tiles_per_seq, 0, 0, 0)),
                  resident((mix, d)),
                  pl.BlockSpec((1, y_att.shape[1]), const),
                  pl.BlockSpec((1, d), const),
                  pl.BlockSpec((1, d), const),
                  resident((d, d_ff)),
                  resident((d_ff, d)),
                  pl.BlockSpec((1, d), const),
                  pl.BlockSpec((1, d), const)],
        out_specs=pl.BlockSpec((tm, d), tok),
        out_shape=jax.ShapeDtypeStruct((n_tok, d), F32),
        compiler_params=pltpu.CompilerParams(
            dimension_semantics=("arbitrary",), vmem_limit_bytes=VMEM_LIMIT_BYTES),
        name="tail",
    )(x2, y_ssm, y_att, mod4, w_out, anw, g1, b1, w1, w2, g2, b2)


def _pad_lanes(v):
    return jnp.pad(v, ((0, 0), (0, LANES - v.shape[1])))


def kernel(x, c, w_ada, b_ada, w_in, conv_w, conv_b, dt_bias, a_log, d_skip, ssm_norm_w, f_bias,
           attn_norm_w, w_out, ln1_g, ln1_b, w_ff_in, w_ff_out, ln2_g, ln2_b):
    batch, seq, d = x.shape
    depth = w_ada.shape[0]
    alpha = (2.0 * depth) ** 0.25
    ssm_w = SSM_HEADS * SSM_HEAD_DIM
    att_w = ATT_HEADS * ATT_HEAD_DIM
    conv_dim = ssm_w + 2 * SSM_GROUPS * SSM_STATE
    o_xbc = ssm_w
    o_dt = o_xbc + conv_dim
    o_q = o_dt + SSM_HEADS
    o_k = o_q + att_w
    o_v = o_k + att_w
    o_f = o_v + att_w
    q_col = (ssm_w + conv_dim) // LANES
    k_col = q_col + att_w // LANES
    v_col = k_col + att_w // LANES
    n_main = v_col * LANES + att_w
    main_col = jnp.arange(n_main)
    is_q = jnp.logical_and(main_col >= q_col * LANES, main_col < k_col * LANES)
    col_scale = jnp.where(is_q, ATT_HEAD_DIM ** -0.5 * LOG2E, 1.0).astype(F32)[None, :]
    expand = (jnp.arange(LANES)[:, None] == (jnp.arange(ssm_w)[None, :] // SSM_HEAD_DIM)).astype(F32)

    x2 = x.reshape(batch * seq, d)
    c_pad = jnp.pad(c, ((0, SUBLANES - batch % SUBLANES if batch % SUBLANES else 0), (0, 0)))
    for l in range(depth):
        mod = _ada(c_pad, w_ada[l], b_ada[l][None, :])
        mod4 = mod[:batch].reshape(batch, 6, 1, d)
        wl = w_in[l]
        w_main = jnp.concatenate([wl[:, :o_dt], wl[:, o_q:o_f]], axis=1).astype(BF16)
        w_small = _pad_lanes(jnp.concatenate([wl[:, o_dt:o_q], wl[:, o_f:]], axis=1)).astype(BF16)
        proj, small = _inproj(x2, mod4, w_main, w_small, col_scale, seq)

        brow = _pad_lanes(jnp.concatenate([dt_bias[l], f_bias[l]])[None, :])
        alog_row = _pad_lanes(a_log[l][None, :])
        dsk_full = jnp.repeat(d_skip[l], SSM_HEAD_DIM)[None, :]
        y_ssm, cum_col = _ssd(proj, small, conv_w[l], conv_b[l][None, :], brow, alog_row,
                              dsk_full, ssm_norm_w[l][None, :], expand, batch, seq)
        y_att = _attn(proj, cum_col, batch, seq, q_col, k_col, v_col)

        x2 = _tail(x2, y_ssm, y_att, mod4, w_out[l].astype(BF16), attn_norm_w[l][None, :],
                   ln1_g[l][None, :], ln1_b[l][None, :], w_ff_in[l].astype(BF16),
                   w_ff_out[l].astype(BF16), ln2_g[l][None, :], ln2_b[l][None, :], seq, alpha)
    return x2.reshape(batch, seq, d)
```

```python
import functools

import jax
import jax.numpy as jnp
import numpy as np
from jax import lax
from jax.experimental import pallas as pl
from jax.experimental.pallas import tpu as pltpu

F32 = jnp.float32
BF16 = jnp.bfloat16
HIGHEST = lax.Precision.HIGHEST

LANES = 128
SUBLANES = 8
BF16_SUBLANES = 16
VMEM_LIMIT_BYTES = 56 * 1024 * 1024

SSM_HEADS = 16
SSM_HEAD_DIM = 64
SSM_GROUPS = 2
SSM_STATE = 128
CONV_WIDTH = 4
CHUNK = 128
ATT_HEADS = 16
ATT_HEAD_DIM = 64
LN_EPS = 1e-5
RMS_EPS = 1e-5

HEADS_PER_GROUP = SSM_HEADS // SSM_GROUPS
HEADS_PER_BLOCK = LANES // ATT_HEAD_DIM
SMALL_DT = 0
SMALL_F = SSM_HEADS
LOG2E = 1.4426950408889634
BIAS_PIECES = 3
BIAS_LANES_PER_HEAD = 2 * BIAS_PIECES


def _silu(v):
    return v * (1.0 / (1.0 + jnp.exp(-v)))


def _softplus(v):
    return jnp.maximum(v, 0.0) + jnp.log(1.0 + jnp.exp(-jnp.abs(v)))


def _split3(c):
    hi = c.astype(BF16).astype(F32)
    r = c - hi
    mid = r.astype(BF16).astype(F32)
    return hi, mid, r - mid


def _pieces(c):
    return jnp.concatenate(_split3(c), axis=1).astype(BF16)


def _layer_norm(v, g, b):
    mu = jnp.mean(v, axis=-1, keepdims=True)
    d = v - mu
    var = jnp.mean(d * d, axis=-1, keepdims=True)
    return d * lax.rsqrt(var + LN_EPS) * g + b


def _ada_kernel(c_ref, w_ref, b_ref, o_ref):
    ca = _silu(c_ref[...]).astype(BF16)
    o_ref[...] = jnp.dot(ca, w_ref[...].astype(BF16), preferred_element_type=F32) + b_ref[...]


def _ada(c_pad, w, b, tn=512):
    rows, d = c_pad.shape
    n = w.shape[1]
    return pl.pallas_call(
        _ada_kernel,
        grid=(n // tn,),
        in_specs=[pl.BlockSpec((rows, d), lambda j: (0, 0)),
                  pl.BlockSpec((d, tn), lambda j: (0, j)),
                  pl.BlockSpec((1, tn), lambda j: (0, j))],
        out_specs=pl.BlockSpec((rows, tn), lambda j: (0, j)),
        out_shape=jax.ShapeDtypeStruct((rows, n), F32),
        name="ada",
    )(c_pad, w, b)


def _inproj_kernel(x_ref, mod_ref, w_ref, ws_ref, o_ref, os_ref, *, tn, q_lo, q_hi, q_scale):
    h = (x_ref[...] * (1.0 + mod_ref[1]) + mod_ref[0]).astype(BF16)
    os_ref[...] = jnp.dot(h, ws_ref[...], preferred_element_type=F32)
    for c0 in range(0, w_ref.shape[1], tn):
        acc = jnp.dot(h, w_ref[:, c0:c0 + tn], preferred_element_type=F32)
        if q_lo <= c0 < q_hi:
            acc = acc * q_scale
        o_ref[:, c0:c0 + tn] = acc.astype(BF16)


def _inproj(x2, mod4, w_main, w_small, seq, q_lo, q_hi, q_scale, tm=512, tn=512):
    n_tok, d = x2.shape
    n_main = w_main.shape[1]
    assert q_lo % tn == 0 and q_hi % tn == 0 and n_main % tn == 0
    tiles_per_seq = seq // tm
    const = lambda i: (0, 0)
    resident = functools.partial(pl.BlockSpec, index_map=const, pipeline_mode=pl.Buffered(1))
    return pl.pallas_call(
        functools.partial(_inproj_kernel, tn=tn, q_lo=q_lo, q_hi=q_hi, q_scale=q_scale),
        grid=(n_tok // tm,),
        in_specs=[pl.BlockSpec((tm, d), lambda i: (i, 0)),
                  pl.BlockSpec((None, 6, 1, d), lambda i: (i // tiles_per_seq, 0, 0, 0)),
                  resident((d, n_main)),
                  resident((d, LANES))],
        out_specs=[pl.BlockSpec((tm, n_main), lambda i: (i, 0)),
                   pl.BlockSpec((tm, LANES), lambda i: (i, 0))],
        out_shape=[jax.ShapeDtypeStruct((n_tok, n_main), BF16),
                   jax.ShapeDtypeStruct((n_tok, LANES), F32)],
        compiler_params=pltpu.CompilerParams(
            dimension_semantics=("arbitrary",), vmem_limit_bytes=VMEM_LIMIT_BYTES),
        name="inproj",
    )(x2, mod4, w_main, w_small)


def _ssd_kernel(z_ref, x_ref, bc_ref, sm_ref, cw_ref, cb_ref, brow_ref, alog_ref, dsk_ref, nw_ref,
                e_ref, place_ref, ones_ref, shift_ref, xh_ref, bch_ref, y_ref, fq_ref, fk_ref, st_ref,
                car_ref):
    width = SSM_HEADS * SSM_HEAD_DIM
    gw = HEADS_PER_GROUP * SSM_HEAD_DIM
    first_chunk = pl.program_id(1) == 0

    @pl.when(first_chunk)
    def _():
        st_ref[...] = jnp.zeros(st_ref.shape, F32)
        car_ref[...] = jnp.zeros(car_ref.shape, F32)

    cur = jnp.concatenate([x_ref[...], bc_ref[...]], axis=1)
    prev = jnp.concatenate([xh_ref[...], bch_ref[...]], axis=1)
    prev = jnp.where(first_chunk, jnp.zeros_like(prev), prev)
    pad = jnp.zeros((CHUNK - BF16_SUBLANES, cur.shape[1]), BF16)
    shifted = jnp.dot(shift_ref[...], jnp.concatenate([pad, prev, cur], axis=0),
                      preferred_element_type=F32)
    acc = cb_ref[...] + cw_ref[CONV_WIDTH - 1:CONV_WIDTH, :] * cur.astype(F32)
    for k in range(1, CONV_WIDTH):
        acc = acc + (cw_ref[CONV_WIDTH - 1 - k:CONV_WIDTH - k, :]
                     * shifted[(k - 1) * CHUNK:k * CHUNK, :])
    u = _silu(acc)
    xs = u[:, 0:width]
    b_all = u[:, width:width + SSM_GROUPS * SSM_STATE]
    c_all = u[:, width + SSM_GROUPS * SSM_STATE:]

    lane = lax.broadcasted_iota(jnp.int32, (CHUNK, LANES), 1)
    row = lax.broadcasted_iota(jnp.int32, (CHUNK, LANES), 0)
    is_dt = lane < SMALL_F
    is_f = jnp.logical_and(lane >= SMALL_F, lane < SMALL_F + ATT_HEADS)
    t = sm_ref[...] + brow_ref[...]
    sp = _softplus(jnp.where(is_dt, t, -t))
    a_row = -jnp.exp(alog_ref[...])
    dt_tile = jnp.where(is_dt, sp, 0.0)
    val = jnp.where(is_dt, sp * a_row, jnp.where(is_f, -sp, 0.0))
    tril = row >= lane
    cs3 = jnp.dot(tril.astype(BF16), _pieces(val), preferred_element_type=F32)
    cs = cs3[:, 0:LANES] + cs3[:, LANES:2 * LANES] + cs3[:, 2 * LANES:]

    feats = jnp.dot(_pieces((cs + car_ref[...]) * LOG2E), place_ref[...],
                    preferred_element_type=F32) + ones_ref[...]
    fq_ref[...] = feats[:, 0:LANES].astype(BF16)
    fk_ref[...] = feats[:, LANES:].astype(BF16)
    car_ref[...] = car_ref[...] + jnp.where(is_f[0:1, :], cs[CHUNK - 1:CHUNK, :], 0.0)

    cs_t = cs.T
    dt_t = dt_tile.T
    full = jnp.dot(_pieces(jnp.concatenate([cs, dt_tile], axis=0)), e_ref[...],
                   preferred_element_type=F32)
    acs_full = full[0:CHUNK]
    dt_full = full[CHUNK:]
    atot_full = acs_full[CHUNK - 1:CHUNK, :]
    dec_out = jnp.exp(acs_full)
    dec_end = jnp.exp(atot_full - acs_full)
    chunk_dec = jnp.exp(atot_full)
    xs_b = xs.astype(BF16)
    xw = (xs * (dec_end * dt_full)).astype(BF16)
    lane_lo = lane < SSM_HEAD_DIM

    ydiag_blocks = []
    yoff_blocks = []
    for g in range(SSM_GROUPS):
        bg = b_all[:, g * SSM_STATE:(g + 1) * SSM_STATE]
        cg = c_all[:, g * SSM_STATE:(g + 1) * SSM_STATE].astype(BF16)
        cbm = lax.dot_general(cg, bg.astype(BF16), (((1,), (1,)), ((), ())),
                              preferred_element_type=F32)
        for j in range(HEADS_PER_GROUP // HEADS_PER_BLOCK):
            h0 = g * HEADS_PER_GROUP + HEADS_PER_BLOCK * j
            xp = xs_b[:, h0 * SSM_HEAD_DIM:h0 * SSM_HEAD_DIM + LANES]
            res = []
            for h in range(h0, h0 + HEADS_PER_BLOCK):
                seg = cs[:, h:h + 1] - cs_t[h:h + 1, :]
                dec = jnp.exp(jnp.where(tril, seg, -1e30))
                sc = (cbm * dec * dt_t[h:h + 1, :]).astype(BF16)
                res.append(jnp.dot(sc, xp, preferred_element_type=F32))
            ydiag_blocks.append(jnp.where(lane_lo, res[0], res[1]))
        st = st_ref[g]
        yoff_blocks.append(jnp.dot(cg, st.astype(BF16), preferred_element_type=F32)
                           * dec_out[:, g * gw:(g + 1) * gw])
        st_ref[g] = st * chunk_dec[:, g * gw:(g + 1) * gw] + jnp.dot(
            bg.T.astype(BF16), xw[:, g * gw:(g + 1) * gw], preferred_element_type=F32)
    y = (jnp.concatenate(ydiag_blocks, axis=1) + jnp.concatenate(yoff_blocks, axis=1)
         + dsk_ref[...] * xs)
    gated = y * _silu(z_ref[...].astype(F32))
    ms = jnp.mean(gated * gated, axis=-1, keepdims=True)
    y_ref[...] = (gated * lax.rsqrt(ms + RMS_EPS) * nw_ref[...]).astype(BF16)


def _bias_routing():
    place = np.zeros((BIAS_PIECES * LANES, 2 * LANES), np.float32)
    ones_row = np.zeros((1, 2 * LANES), np.float32)
    for h in range(ATT_HEADS):
        base = BIAS_LANES_PER_HEAD * h
        for t in range(BIAS_PIECES):
            place[t * LANES + SMALL_F + h, base + t] = 1.0
            place[t * LANES + SMALL_F + h, LANES + base + BIAS_PIECES + t] = -1.0
            ones_row[0, base + BIAS_PIECES + t] = 1.0
            ones_row[0, LANES + base + t] = 1.0
    return jnp.asarray(place, BF16), jnp.asarray(ones_row, F32)


def _conv_shifts():
    shift = np.zeros(((CONV_WIDTH - 1) * CHUNK, 2 * CHUNK), np.float32)
    for k in range(1, CONV_WIDTH):
        for t in range(CHUNK):
            shift[(k - 1) * CHUNK + t, CHUNK + t - k] = 1.0
    return jnp.asarray(shift, BF16)


def _head_expansion():
    expand = np.zeros((BIAS_PIECES * LANES, SSM_HEADS * SSM_HEAD_DIM), np.float32)
    for t in range(BIAS_PIECES):
        for h in range(SSM_HEADS):
            expand[t * LANES + SMALL_DT + h, h * SSM_HEAD_DIM:(h + 1) * SSM_HEAD_DIM] = 1.0
    return jnp.asarray(expand, BF16)


def _ssd(proj, small, conv_w, conv_b, brow, alog_row, dsk_full, norm_w, batch, seq):
    place, ones_row = _bias_routing()
    shift = _conv_shifts()
    expand = _head_expansion()
    n_tok = proj.shape[0]
    width = SSM_HEADS * SSM_HEAD_DIM
    bc_w = 2 * SSM_GROUPS * SSM_STATE
    conv_dim = width + bc_w
    nc = seq // CHUNK
    tok = lambda b, c: (b * nc + c, 0)
    const = lambda b, c: (0, 0)
    halo_block = lambda b, c: jnp.maximum((b * nc + c) * (CHUNK // BF16_SUBLANES) - 1, 0)
    return pl.pallas_call(
        _ssd_kernel,
        grid=(batch, nc),
        in_specs=[pl.BlockSpec((CHUNK, width), tok),
                  pl.BlockSpec((CHUNK, width), lambda b, c: (b * nc + c, 1)),
                  pl.BlockSpec((CHUNK, bc_w), lambda b, c: (b * nc + c, 2 * width // bc_w)),
                  pl.BlockSpec((CHUNK, LANES), tok),
                  pl.BlockSpec((CONV_WIDTH, conv_dim), const),
                  pl.BlockSpec((1, conv_dim), const),
                  pl.BlockSpec((1, LANES), const),
                  pl.BlockSpec((1, LANES), const),
                  pl.BlockSpec((1, width), const),
                  pl.BlockSpec((1, width), const),
                  pl.BlockSpec(expand.shape, const),
                  pl.BlockSpec(place.shape, const),
                  pl.BlockSpec(ones_row.shape, const),
                  pl.BlockSpec(shift.shape, const),
                  pl.BlockSpec((BF16_SUBLANES, width), lambda b, c: (halo_block(b, c), 1)),
                  pl.BlockSpec((BF16_SUBLANES, bc_w),
                               lambda b, c: (halo_block(b, c), 2 * width // bc_w))],
        out_specs=[pl.BlockSpec((CHUNK, width), tok),
                   pl.BlockSpec((CHUNK, LANES), tok),
                   pl.BlockSpec((CHUNK, LANES), tok)],
        out_shape=[jax.ShapeDtypeStruct((n_tok, width), BF16),
                   jax.ShapeDtypeStruct((n_tok, LANES), BF16),
                   jax.ShapeDtypeStruct((n_tok, LANES), BF16)],
        scratch_shapes=[pltpu.VMEM((SSM_GROUPS, SSM_STATE, HEADS_PER_GROUP * SSM_HEAD_DIM), F32),
                        pltpu.VMEM((1, LANES), F32)],
        compiler_params=pltpu.CompilerParams(
            dimension_semantics=("arbitrary", "arbitrary"), vmem_limit_bytes=VMEM_LIMIT_BYTES),
        name="ssd",
    )(proj, proj, proj, small, conv_w, conv_b, brow, alog_row, dsk_full, norm_w, expand, place,
      ones_row, shift, proj, proj)


ATT_TK = 256
ATT_SUB = 256
ATT_VT_ROWS = ATT_HEAD_DIM + BF16_SUBLANES


def _attn_kernel(q_ref, k_ref, v_ref, fq_ref, fk_ref, o_ref, vt_ref, st_ref, acc_ref, m_ref, *, tq):
    pair = pl.program_id(1)
    qi = pl.program_id(2)
    seq = k_ref.shape[0]
    n_sub = tq // ATT_SUB
    assert n_sub % 2 == 0
    hd = ATT_HEAD_DIM

    @pl.when(qi == 0)
    def _():
        def prep(r, carry):
            rows = pl.ds(pl.multiple_of(r * ATT_TK, ATT_TK), ATT_TK)
            vt = v_ref[rows, :].astype(F32).T.astype(BF16)
            for i in range(HEADS_PER_BLOCK):
                vt_ref[i, 0:hd, rows] = vt[i * hd:(i + 1) * hd, :]
                vt_ref[i, hd:, rows] = jnp.ones((ATT_VT_ROWS - hd, ATT_TK), BF16)
            return carry

        lax.fori_loop(0, seq // ATT_TK, prep, 0)

    lane = lax.broadcasted_iota(jnp.int32, (tq, LANES), 1)
    q = q_ref[...]
    fq = fq_ref[...]
    q_aug = []
    for i in range(HEADS_PER_BLOCK):
        first = BIAS_LANES_PER_HEAD * (HEADS_PER_BLOCK * pair + i)
        own_q = (lane < hd) if i == 0 else (lane >= hd)
        own_f = jnp.logical_and(lane >= first, lane < first + BIAS_LANES_PER_HEAD)
        q_aug.append(jnp.concatenate([jnp.where(own_q, q, jnp.zeros_like(q)),
                                      jnp.where(own_f, fq, jnp.zeros_like(fq))], axis=1))
    causal = (lax.broadcasted_iota(jnp.int32, (ATT_TK, ATT_SUB), 0)
              <= lax.broadcasted_iota(jnp.int32, (ATT_TK, ATT_SUB), 1))
    chains = [(i, s) for i in range(HEADS_PER_BLOCK) for s in range(n_sub)]

    def scores(c, block):
        i, s = chains[c]
        rows = pl.ds(pl.multiple_of(block * ATT_TK, ATT_TK), ATT_TK)
        kb = jnp.concatenate([k_ref[rows, :], fk_ref[rows, :]], axis=1)
        qa = q_aug[i][s * ATT_SUB:(s + 1) * ATT_SUB, :]
        return lax.dot_general(kb, qa, (((1,), (1,)), ((), ())), preferred_element_type=F32)

    def update(c, block, st, masked):
        if masked:
            st = jnp.where(causal, st, -1e30)
        m = m_ref[c]
        m_new = jnp.maximum(m, jnp.max(st, axis=0, keepdims=True))
        pt = jnp.exp2(st - m_new).astype(BF16)
        start = pl.multiple_of(block * ATT_TK, ATT_TK)
        vt = vt_ref[chains[c][0], :, pl.ds(start, ATT_TK)]
        acc_ref[c] = (jnp.exp2(m - m_new) * acc_ref[c]
                      + jnp.dot(vt, pt, preferred_element_type=F32))
        m_ref[c] = m_new

    every = list(range(len(chains)))
    for c in every:
        m_ref[c] = jnp.full((1, ATT_SUB), -1e30, F32)
        acc_ref[c] = jnp.zeros((ATT_VT_ROWS, ATT_SUB), F32)

    first_diag = qi * n_sub
    diag = [(c, e) for e in range(n_sub) for c in every if chains[c][1] >= e]
    diag_scores = [scores(c, first_diag + e) for c, e in diag]
    for c in every:
        st_ref[0, c] = scores(c, 0)
    for (c, e), st in zip(diag, diag_scores):
        update(c, first_diag + e, st, chains[c][1] == e)

    def half_step(block, slot, prefetch=True):
        if prefetch:
            for c in every:
                st_ref[1 - slot, c] = scores(c, block + 1)
        for c in every:
            update(c, block, st_ref[slot, c], False)

    def body(j, carry):
        half_step(2 * j, 0)
        half_step(2 * j + 1, 1)
        return carry

    lax.fori_loop(0, qi * (n_sub // 2) - 1, body, 0)

    @pl.when(qi > 0)
    def _():
        half_step(first_diag - 2, 0)
        half_step(first_diag - 1, 1, prefetch=False)

    heads = [[None] * n_sub for _ in range(HEADS_PER_BLOCK)]
    for c, (i, s) in enumerate(chains):
        acc = acc_ref[c]
        heads[i][s] = acc[0:hd, :] * (1.0 / acc[hd:hd + 1, :])
    o_t = jnp.concatenate([jnp.concatenate(h, axis=1) for h in heads], axis=0)
    o_ref[...] = o_t.T.astype(BF16)


def _attn(proj, feat_q, feat_k, batch, seq, q_col, k_col, v_col, tq=1024):
    n_tok = proj.shape[0]
    nq = seq // tq
    n_pairs = ATT_HEADS // HEADS_PER_BLOCK
    n_chains = HEADS_PER_BLOCK * (tq // ATT_SUB)
    return pl.pallas_call(
        functools.partial(_attn_kernel, tq=tq),
        grid=(batch, n_pairs, nq),
        in_specs=[pl.BlockSpec((tq, LANES), lambda b, p, i: (b * nq + i, q_col + p)),
                  pl.BlockSpec((seq, LANES), lambda b, p, i: (b, k_col + p)),
                  pl.BlockSpec((seq, LANES), lambda b, p, i: (b, v_col + p)),
                  pl.BlockSpec((tq, LANES), lambda b, p, i: (b * nq + i, 0)),
                  pl.BlockSpec((seq, LANES), lambda b, p, i: (b, 0))],
        out_specs=pl.BlockSpec((tq, LANES), lambda b, p, i: (b * nq + i, p)),
        out_shape=jax.ShapeDtypeStruct((n_tok, ATT_HEADS * ATT_HEAD_DIM), BF16),
        scratch_shapes=[pltpu.VMEM((HEADS_PER_BLOCK, ATT_VT_ROWS, seq), BF16),
                        pltpu.VMEM((2, n_chains, ATT_TK, ATT_SUB), F32),
                        pltpu.VMEM((n_chains, ATT_VT_ROWS, ATT_SUB), F32),
                        pltpu.VMEM((n_chains, 1, ATT_SUB), F32)],
        compiler_params=pltpu.CompilerParams(
            dimension_semantics=("arbitrary", "arbitrary", "arbitrary"),
            vmem_limit_bytes=VMEM_LIMIT_BYTES),
        name="attn",
    )(proj, proj, proj, feat_q, feat_k)


def _tail_kernel(x_ref, ys_ref, ya_ref, mod_ref, wo_ref, anw_ref, g1_ref, b1_ref, w1_ref, w2_ref,
                 g2_ref, b2_ref, o_ref, *, alpha, ff_chunk):
    ssm_w = ys_ref.shape[1]
    ya = ya_ref[...].astype(F32)
    ya = ya * lax.rsqrt(jnp.mean(ya * ya, axis=-1, keepdims=True) + RMS_EPS) * anw_ref[...]
    y = (jnp.dot(ys_ref[...], wo_ref[0:ssm_w, :], preferred_element_type=F32)
         + jnp.dot(ya.astype(BF16), wo_ref[ssm_w:, :], preferred_element_type=F32))
    x1 = _layer_norm(alpha * x_ref[...] + (1.0 + mod_ref[2]) * y, g1_ref[...], b1_ref[...])
    h = (x1 * (1.0 + mod_ref[4]) + mod_ref[3]).astype(BF16)
    ff = jnp.zeros(x1.shape, F32)
    for c in range(w1_ref.shape[1] // ff_chunk):
        a = jnp.dot(h, w1_ref[:, c * ff_chunk:(c + 1) * ff_chunk], preferred_element_type=F32)
        a = jnp.maximum(a, 0.0)
        ff = ff + jnp.dot((a * a).astype(BF16), w2_ref[c * ff_chunk:(c + 1) * ff_chunk, :],
                          preferred_element_type=F32)
    o_ref[...] = _layer_norm(alpha * x1 + (1.0 + mod_ref[5]) * ff, g2_ref[...], b2_ref[...])


def _tail(x2, y_ssm, y_att, mod4, w_out, anw, g1, b1, w1, w2, g2, b2, seq, alpha, tm=512,
          ff_chunk=1024):
    n_tok, d = x2.shape
    mix = w_out.shape[0]
    d_ff = w1.shape[1]
    tiles_per_seq = seq // tm
    tok = lambda i: (i, 0)
    const = lambda i: (0, 0)
    resident = functools.partial(pl.BlockSpec, index_map=const, pipeline_mode=pl.Buffered(1))
    return pl.pallas_call(
        functools.partial(_tail_kernel, alpha=alpha, ff_chunk=ff_chunk),
        grid=(n_tok // tm,),
        in_specs=[pl.BlockSpec((tm, d), tok),
                  pl.BlockSpec((tm, y_ssm.shape[1]), tok),
                  pl.BlockSpec((tm, y_att.shape[1]), tok),
                  pl.BlockSpec((None, 6, 1, d), lambda i: (i // tiles_per_seq, 0, 0, 0)),
                  resident((mix, d)),
                  pl.BlockSpec((1, y_att.shape[1]), const),
                  pl.BlockSpec((1, d), const),
                  pl.BlockSpec((1, d), const),
                  resident((d, d_ff)),
                  resident((d_ff, d)),
                  pl.BlockSpec((1, d), const),
                  pl.BlockSpec((1, d), const)],
        out_specs=pl.BlockSpec((tm, d), tok),
        out_shape=jax.ShapeDtypeStruct((n_tok, d), F32),
        compiler_params=pltpu.CompilerParams(
            dimension_semantics=("arbitrary",), vmem_limit_bytes=VMEM_LIMIT_BYTES),
        name="tail",
    )(x2, y_ssm, y_att, mod4, w_out, anw, g1, b1, w1, w2, g2, b2)


def _pad_lanes(v):
    return jnp.pad(v, ((0, 0), (0, LANES - v.shape[1])))


def kernel(x, c, w_ada, b_ada, w_in, conv_w, conv_b, dt_bias, a_log, d_skip, ssm_norm_w, f_bias,
           attn_norm_w, w_out, ln1_g, ln1_b, w_ff_in, w_ff_out, ln2_g, ln2_b):
    batch, seq, d = x.shape
    depth = w_ada.shape[0]
    alpha = (2.0 * depth) ** 0.25
    ssm_w = SSM_HEADS * SSM_HEAD_DIM
    att_w = ATT_HEADS * ATT_HEAD_DIM
    conv_dim = ssm_w + 2 * SSM_GROUPS * SSM_STATE
    o_xbc = ssm_w
    o_dt = o_xbc + conv_dim
    o_q = o_dt + SSM_HEADS
    o_k = o_q + att_w
    o_v = o_k + att_w
    o_f = o_v + att_w
    q_col = (ssm_w + conv_dim) // LANES
    k_col = q_col + att_w // LANES
    v_col = k_col + att_w // LANES

    x2 = x.reshape(batch * seq, d)
    c_pad = jnp.pad(c, ((0, SUBLANES - batch % SUBLANES if batch % SUBLANES else 0), (0, 0)))
    for l in range(depth):
        mod = _ada(c_pad, w_ada[l], b_ada[l][None, :])
        mod4 = mod[:batch].reshape(batch, 6, 1, d)
        wl = w_in[l]
        w_main = jnp.concatenate([wl[:, :o_dt], wl[:, o_q:o_f]], axis=1).astype(BF16)
        w_small = _pad_lanes(jnp.concatenate([wl[:, o_dt:o_q], wl[:, o_f:]], axis=1)).astype(BF16)
        proj, small = _inproj(x2, mod4, w_main, w_small, seq, q_col * LANES, k_col * LANES,
                              ATT_HEAD_DIM ** -0.5 * LOG2E)

        brow = _pad_lanes(jnp.concatenate([dt_bias[l], f_bias[l]])[None, :])
        alog_row = _pad_lanes(a_log[l][None, :])
        dsk_full = jnp.repeat(d_skip[l], SSM_HEAD_DIM)[None, :]
        y_ssm, feat_q, feat_k = _ssd(proj, small, conv_w[l], conv_b[l][None, :], brow, alog_row,
                                     dsk_full, ssm_norm_w[l][None, :], batch, seq)
        y_att = _attn(proj, feat_q, feat_k, batch, seq, q_col, k_col, v_col)

        x2 = _tail(x2, y_ssm, y_att, mod4, w_out[l].astype(BF16), attn_norm_w[l][None, :],
                   ln1_g[l][None, :], ln1_b[l][None, :], w_ff_in[l].astype(BF16),
                   w_ff_out[l].astype(BF16), ln2_g[l][None, :], ln2_b[l][None, :], seq, alpha)
    return x2.reshape(batch, seq, d)
```

```python
import functools

import jax
import jax.numpy as jnp
import numpy as np
from jax import lax
from jax.experimental import pallas as pl
from jax.experimental.pallas import tpu as pltpu

F32 = jnp.float32
BF16 = jnp.bfloat16
HIGHEST = lax.Precision.HIGHEST

LANES = 128
SUBLANES = 8
BF16_SUBLANES = 16
VMEM_LIMIT_BYTES = 56 * 1024 * 1024

SSM_HEADS = 16
SSM_HEAD_DIM = 64
SSM_GROUPS = 2
SSM_STATE = 128
CONV_WIDTH = 4
CHUNK = 128
ATT_HEADS = 16
ATT_HEAD_DIM = 64
LN_EPS = 1e-5
RMS_EPS = 1e-5

HEADS_PER_GROUP = SSM_HEADS // SSM_GROUPS
HEADS_PER_BLOCK = LANES // ATT_HEAD_DIM
SMALL_DT = 0
SMALL_F = SSM_HEADS
LOG2E = 1.4426950408889634
BIAS_PIECES = 3
BIAS_LANES_PER_HEAD = 2 * BIAS_PIECES


def _silu(v):
    return v * (1.0 / (1.0 + jnp.exp(-v)))


def _softplus(v):
    return jnp.maximum(v, 0.0) + jnp.log(1.0 + jnp.exp(-jnp.abs(v)))


def _split3(c):
    hi = c.astype(BF16).astype(F32)
    r = c - hi
    mid = r.astype(BF16).astype(F32)
    return hi, mid, r - mid


def _pieces(c):
    return jnp.concatenate(_split3(c), axis=1).astype(BF16)


def _layer_norm(v, g, b):
    mu = jnp.mean(v, axis=-1, keepdims=True)
    d = v - mu
    var = jnp.mean(d * d, axis=-1, keepdims=True)
    return d * lax.rsqrt(var + LN_EPS) * g + b


def _ada_kernel(c_ref, w_ref, b_ref, o_ref):
    ca = _silu(c_ref[...]).astype(BF16)
    o_ref[...] = jnp.dot(ca, w_ref[...].astype(BF16), preferred_element_type=F32) + b_ref[...]


def _ada(c_pad, w, b, tn=512):
    rows, d = c_pad.shape
    n = w.shape[1]
    return pl.pallas_call(
        _ada_kernel,
        grid=(n // tn,),
        in_specs=[pl.BlockSpec((rows, d), lambda j: (0, 0)),
                  pl.BlockSpec((d, tn), lambda j: (0, j)),
                  pl.BlockSpec((1, tn), lambda j: (0, j))],
        out_specs=pl.BlockSpec((rows, tn), lambda j: (0, j)),
        out_shape=jax.ShapeDtypeStruct((rows, n), F32),
        name="ada",
    )(c_pad, w, b)


def _inproj_kernel(x_ref, mod_ref, w_ref, ws_ref, o_ref, os_ref, *, tn, q_lo, q_hi, q_scale):
    h = (x_ref[...] * (1.0 + mod_ref[1]) + mod_ref[0]).astype(BF16)
    os_ref[...] = jnp.dot(h, ws_ref[...], preferred_element_type=F32)
    for c0 in range(0, w_ref.shape[1], tn):
        acc = jnp.dot(h, w_ref[:, c0:c0 + tn], preferred_element_type=F32)
        if q_lo <= c0 < q_hi:
            acc = acc * q_scale
        o_ref[:, c0:c0 + tn] = acc.astype(BF16)


def _inproj(x2, mod4, w_main, w_small, seq, q_lo, q_hi, q_scale, tm=512, tn=512):
    n_tok, d = x2.shape
    n_main = w_main.shape[1]
    assert q_lo % tn == 0 and q_hi % tn == 0 and n_main % tn == 0
    tiles_per_seq = seq // tm
    const = lambda i: (0, 0)
    resident = functools.partial(pl.BlockSpec, index_map=const, pipeline_mode=pl.Buffered(1))
    return pl.pallas_call(
        functools.partial(_inproj_kernel, tn=tn, q_lo=q_lo, q_hi=q_hi, q_scale=q_scale),
        grid=(n_tok // tm,),
        in_specs=[pl.BlockSpec((tm, d), lambda i: (i, 0)),
                  pl.BlockSpec((None, 6, 1, d), lambda i: (i // tiles_per_seq, 0, 0, 0)),
                  resident((d, n_main)),
                  resident((d, LANES))],
        out_specs=[pl.BlockSpec((tm, n_main), lambda i: (i, 0)),
                   pl.BlockSpec((tm, LANES), lambda i: (i, 0))],
        out_shape=[jax.ShapeDtypeStruct((n_tok, n_main), BF16),
                   jax.ShapeDtypeStruct((n_tok, LANES), F32)],
        compiler_params=pltpu.CompilerParams(
            dimension_semantics=("arbitrary",), vmem_limit_bytes=VMEM_LIMIT_BYTES),
        name="inproj",
    )(x2, mod4, w_main, w_small)


def _ssd_kernel(z_ref, x_ref, bc_ref, sm_ref, cw_ref, cb_ref, brow_ref, alog_ref, dsk_ref, nw_ref,
                e_ref, place_ref, ones_ref, shift_ref, xh_ref, bch_ref, y_ref, fq_ref, fk_ref, st_ref,
                car_ref):
    width = SSM_HEADS * SSM_HEAD_DIM
    gw = HEADS_PER_GROUP * SSM_HEAD_DIM
    first_chunk = pl.program_id(1) == 0

    @pl.when(first_chunk)
    def _():
        st_ref[...] = jnp.zeros(st_ref.shape, F32)
        car_ref[...] = jnp.zeros(car_ref.shape, F32)

    cur = jnp.concatenate([x_ref[...], bc_ref[...]], axis=1)
    prev = jnp.concatenate([xh_ref[...], bch_ref[...]], axis=1)
    prev = jnp.where(first_chunk, jnp.zeros_like(prev), prev)
    pad = jnp.zeros((CHUNK - BF16_SUBLANES, cur.shape[1]), BF16)
    shifted = jnp.dot(shift_ref[...], jnp.concatenate([pad, prev, cur], axis=0),
                      preferred_element_type=F32)
    acc = cb_ref[...] + cw_ref[CONV_WIDTH - 1:CONV_WIDTH, :] * cur.astype(F32)
    for k in range(1, CONV_WIDTH):
        acc = acc + (cw_ref[CONV_WIDTH - 1 - k:CONV_WIDTH - k, :]
                     * shifted[(k - 1) * CHUNK:k * CHUNK, :])
    u = _silu(acc)
    xs = u[:, 0:width]
    b_all = u[:, width:width + SSM_GROUPS * SSM_STATE]
    c_all = u[:, width + SSM_GROUPS * SSM_STATE:]

    lane = lax.broadcasted_iota(jnp.int32, (CHUNK, LANES), 1)
    row = lax.broadcasted_iota(jnp.int32, (CHUNK, LANES), 0)
    is_dt = lane < SMALL_F
    is_f = jnp.logical_and(lane >= SMALL_F, lane < SMALL_F + ATT_HEADS)
    t = sm_ref[...] + brow_ref[...]
    sp = _softplus(jnp.where(is_dt, t, -t))
    a_row = -jnp.exp(alog_ref[...])
    dt_tile = jnp.where(is_dt, sp, 0.0)
    val = jnp.where(is_dt, sp * a_row, jnp.where(is_f, -sp, 0.0))
    tril = row >= lane
    cs3 = jnp.dot(tril.astype(BF16), _pieces(val), preferred_element_type=F32)
    cs = cs3[:, 0:LANES] + cs3[:, LANES:2 * LANES] + cs3[:, 2 * LANES:]

    feats = jnp.dot(_pieces((cs + car_ref[...]) * LOG2E), place_ref[...],
                    preferred_element_type=F32) + ones_ref[...]
    fq_ref[...] = feats[:, 0:LANES].astype(BF16)
    fk_ref[...] = feats[:, LANES:].astype(BF16)
    car_ref[...] = car_ref[...] + jnp.where(is_f[0:1, :], cs[CHUNK - 1:CHUNK, :], 0.0)

    cs_t = cs.T
    dt_t = dt_tile.T
    full = jnp.dot(_pieces(jnp.concatenate([cs, dt_tile], axis=0)), e_ref[...],
                   preferred_element_type=F32)
    acs_full = full[0:CHUNK]
    dt_full = full[CHUNK:]
    atot_full = acs_full[CHUNK - 1:CHUNK, :]
    dec_out = jnp.exp(acs_full)
    dec_end = jnp.exp(atot_full - acs_full)
    chunk_dec = jnp.exp(atot_full)
    xs_b = xs.astype(BF16)
    xw = (xs * (dec_end * dt_full)).astype(BF16)
    lane_lo = lane < SSM_HEAD_DIM

    ydiag_blocks = []
    yoff_blocks = []
    for g in range(SSM_GROUPS):
        bg = b_all[:, g * SSM_STATE:(g + 1) * SSM_STATE]
        cg = c_all[:, g * SSM_STATE:(g + 1) * SSM_STATE].astype(BF16)
        cbm = lax.dot_general(cg, bg.astype(BF16), (((1,), (1,)), ((), ())),
                              preferred_element_type=F32)
        for j in range(HEADS_PER_GROUP // HEADS_PER_BLOCK):
            h0 = g * HEADS_PER_GROUP + HEADS_PER_BLOCK * j
            xp = xs_b[:, h0 * SSM_HEAD_DIM:h0 * SSM_HEAD_DIM + LANES]
            res = []
            for h in range(h0, h0 + HEADS_PER_BLOCK):
                seg = cs[:, h:h + 1] - cs_t[h:h + 1, :]
                dec = jnp.exp(jnp.where(tril, seg, -1e30))
                sc = (cbm * dec * dt_t[h:h + 1, :]).astype(BF16)
                res.append(jnp.dot(sc, xp, preferred_element_type=F32))
            ydiag_blocks.append(jnp.where(lane_lo, res[0], res[1]))
        st = st_ref[g]
        yoff_blocks.append(jnp.dot(cg, st.astype(BF16), preferred_element_type=F32)
                           * dec_out[:, g * gw:(g + 1) * gw])
        st_ref[g] = st * chunk_dec[:, g * gw:(g + 1) * gw] + jnp.dot(
            bg.T.astype(BF16), xw[:, g * gw:(g + 1) * gw], preferred_element_type=F32)
    y = (jnp.concatenate(ydiag_blocks, axis=1) + jnp.concatenate(yoff_blocks, axis=1)
         + dsk_ref[...] * xs)
    gated = y * _silu(z_ref[...].astype(F32))
    ms = jnp.mean(gated * gated, axis=-1, keepdims=True)
    y_ref[...] = (gated * lax.rsqrt(ms + RMS_EPS) * nw_ref[...]).astype(BF16)


def _bias_routing():
    place = np.zeros((BIAS_PIECES * LANES, 2 * LANES), np.float32)
    ones_row = np.zeros((1, 2 * LANES), np.float32)
    for h in range(ATT_HEADS):
        base = BIAS_LANES_PER_HEAD * h
        for t in range(BIAS_PIECES):
            place[t * LANES + SMALL_F + h, base + t] = 1.0
            place[t * LANES + SMALL_F + h, LANES + base + BIAS_PIECES + t] = -1.0
            ones_row[0, base + BIAS_PIECES + t] = 1.0
            ones_row[0, LANES + base + t] = 1.0
    return jnp.asarray(place, BF16), jnp.asarray(ones_row, F32)


def _conv_shifts():
    shift = np.zeros(((CONV_WIDTH - 1) * CHUNK, 2 * CHUNK), np.float32)
    for k in range(1, CONV_WIDTH):
        for t in range(CHUNK):
            shift[(k - 1) * CHUNK + t, CHUNK + t - k] = 1.0
    return jnp.asarray(shift, BF16)


def _head_expansion():
    expand = np.zeros((BIAS_PIECES * LANES, SSM_HEADS * SSM_HEAD_DIM), np.float32)
    for t in range(BIAS_PIECES):
        for h in range(SSM_HEADS):
            expand[t * LANES + SMALL_DT + h, h * SSM_HEAD_DIM:(h + 1) * SSM_HEAD_DIM] = 1.0
    return jnp.asarray(expand, BF16)


def _ssd(proj, small, conv_w, conv_b, brow, alog_row, dsk_full, norm_w, batch, seq):
    place, ones_row = _bias_routing()
    shift = _conv_shifts()
    expand = _head_expansion()
    n_tok = proj.shape[0]
    width = SSM_HEADS * SSM_HEAD_DIM
    bc_w = 2 * SSM_GROUPS * SSM_STATE
    conv_dim = width + bc_w
    nc = seq // CHUNK
    tok = lambda b, c: (b * nc + c, 0)
    const = lambda b, c: (0, 0)
    halo_block = lambda b, c: jnp.maximum((b * nc + c) * (CHUNK // BF16_SUBLANES) - 1, 0)
    return pl.pallas_call(
        _ssd_kernel,
        grid=(batch, nc),
        in_specs=[pl.BlockSpec((CHUNK, width), tok),
                  pl.BlockSpec((CHUNK, width), lambda b, c: (b * nc + c, 1)),
                  pl.BlockSpec((CHUNK, bc_w), lambda b, c: (b * nc + c, 2 * width // bc_w)),
                  pl.BlockSpec((CHUNK, LANES), tok),
                  pl.BlockSpec((CONV_WIDTH, conv_dim), const),
                  pl.BlockSpec((1, conv_dim), const),
                  pl.BlockSpec((1, LANES), const),
                  pl.BlockSpec((1, LANES), const),
                  pl.BlockSpec((1, width), const),
                  pl.BlockSpec((1, width), const),
                  pl.BlockSpec(expand.shape, const),
                  pl.BlockSpec(place.shape, const),
                  pl.BlockSpec(ones_row.shape, const),
                  pl.BlockSpec(shift.shape, const),
                  pl.BlockSpec((BF16_SUBLANES, width), lambda b, c: (halo_block(b, c), 1)),
                  pl.BlockSpec((BF16_SUBLANES, bc_w),
                               lambda b, c: (halo_block(b, c), 2 * width // bc_w))],
        out_specs=[pl.BlockSpec((CHUNK, width), tok),
                   pl.BlockSpec((CHUNK, LANES), tok),
                   pl.BlockSpec((CHUNK, LANES), tok)],
        out_shape=[jax.ShapeDtypeStruct((n_tok, width), BF16),
                   jax.ShapeDtypeStruct((n_tok, LANES), BF16),
                   jax.ShapeDtypeStruct((n_tok, LANES), BF16)],
        scratch_shapes=[pltpu.VMEM((SSM_GROUPS, SSM_STATE, HEADS_PER_GROUP * SSM_HEAD_DIM), F32),
                        pltpu.VMEM((1, LANES), F32)],
        compiler_params=pltpu.CompilerParams(
            dimension_semantics=("arbitrary", "arbitrary"), vmem_limit_bytes=VMEM_LIMIT_BYTES),
        name="ssd",
    )(proj, proj, proj, small, conv_w, conv_b, brow, alog_row, dsk_full, norm_w, expand, place,
      ones_row, shift, proj, proj)


ATT_TK = 256
ATT_SUB = 256
ATT_UNROLL = 4
ATT_VT_ROWS = ATT_HEAD_DIM + BF16_SUBLANES


def _attn_kernel(q_ref, k_ref, v_ref, fq_ref, fk_ref, o_ref, vt_ref, st_ref, mx_ref, acc_ref, m_ref,
                 *, tq):
    pair = pl.program_id(1)
    qi = pl.program_id(2)
    seq = k_ref.shape[0]
    n_sub = tq // ATT_SUB
    assert n_sub % ATT_UNROLL == 0
    hd = ATT_HEAD_DIM

    @pl.when(qi == 0)
    def _():
        def prep(r, carry):
            rows = pl.ds(pl.multiple_of(r * ATT_TK, ATT_TK), ATT_TK)
            vt = v_ref[rows, :].astype(F32).T.astype(BF16)
            for i in range(HEADS_PER_BLOCK):
                vt_ref[i, 0:hd, rows] = vt[i * hd:(i + 1) * hd, :]
                vt_ref[i, hd:, rows] = jnp.ones((ATT_VT_ROWS - hd, ATT_TK), BF16)
            return carry

        lax.fori_loop(0, seq // ATT_TK, prep, 0)

    lane = lax.broadcasted_iota(jnp.int32, (tq, LANES), 1)
    q = q_ref[...]
    fq = fq_ref[...]
    q_aug = []
    for i in range(HEADS_PER_BLOCK):
        first = BIAS_LANES_PER_HEAD * (HEADS_PER_BLOCK * pair + i)
        own_q = (lane < hd) if i == 0 else (lane >= hd)
        own_f = jnp.logical_and(lane >= first, lane < first + BIAS_LANES_PER_HEAD)
        q_aug.append(jnp.concatenate([jnp.where(own_q, q, jnp.zeros_like(q)),
                                      jnp.where(own_f, fq, jnp.zeros_like(fq))], axis=1))
    causal = (lax.broadcasted_iota(jnp.int32, (ATT_TK, ATT_SUB), 0)
              <= lax.broadcasted_iota(jnp.int32, (ATT_TK, ATT_SUB), 1))
    chains = [(i, s) for i in range(HEADS_PER_BLOCK) for s in range(n_sub)]

    def head_scores(i, block, s_from=0):
        rows = pl.ds(pl.multiple_of(block * ATT_TK, ATT_TK), ATT_TK)
        kb = jnp.concatenate([k_ref[rows, :], fk_ref[rows, :]], axis=1)
        qa = q_aug[i][s_from * ATT_SUB:, :]
        return lax.dot_general(kb, qa, (((1,), (1,)), ((), ())), preferred_element_type=F32)

    def update(c, block, st, st_max):
        m = m_ref[c]
        m_new = jnp.maximum(m, st_max)
        pt = jnp.exp2(st - m_new).astype(BF16)
        start = pl.multiple_of(block * ATT_TK, ATT_TK)
        vt = vt_ref[chains[c][0], :, pl.ds(start, ATT_TK)]
        acc_ref[c] = (jnp.exp2(m - m_new) * acc_ref[c]
                      + jnp.dot(vt, pt, preferred_element_type=F32))
        m_ref[c] = m_new

    def stash(slot, block):
        for i in range(HEADS_PER_BLOCK):
            st = head_scores(i, block)
            for s in range(n_sub):
                part = st[:, s * ATT_SUB:(s + 1) * ATT_SUB]
                st_ref[slot, i * n_sub + s] = part
                mx_ref[slot, i * n_sub + s] = jnp.max(part, axis=0, keepdims=True)

    every = list(range(len(chains)))
    for c in every:
        m_ref[c] = jnp.full((1, ATT_SUB), -1e30, F32)
        acc_ref[c] = jnp.zeros((ATT_VT_ROWS, ATT_SUB), F32)

    first_diag = qi * n_sub
    diag = []
    for e in range(n_sub):
        for i in range(HEADS_PER_BLOCK):
            st = head_scores(i, first_diag + e, s_from=e)
            for s in range(e, n_sub):
                part = st[:, (s - e) * ATT_SUB:(s - e + 1) * ATT_SUB]
                if s == e:
                    part = jnp.where(causal, part, -1e30)
                diag.append((i * n_sub + s, first_diag + e, part))
    stash(0, 0)
    for c, block, part in diag:
        update(c, block, part, jnp.max(part, axis=0, keepdims=True))

    def half_step(block, slot, prefetch=True):
        if prefetch:
            stash(1 - slot, block + 1)
        for c in every:
            update(c, block, st_ref[slot, c], mx_ref[slot, c])

    def body(j, carry):
        for t in range(ATT_UNROLL):
            half_step(ATT_UNROLL * j + t, t % 2)
        return carry

    lax.fori_loop(0, qi * (n_sub // ATT_UNROLL) - 1, body, 0)

    @pl.when(qi > 0)
    def _():
        for t in range(ATT_UNROLL):
            half_step(first_diag - ATT_UNROLL + t, t % 2, prefetch=t < ATT_UNROLL - 1)

    heads = [[None] * n_sub for _ in range(HEADS_PER_BLOCK)]
    for c, (i, s) in enumerate(chains):
        acc = acc_ref[c]
        heads[i][s] = acc[0:hd, :] * (1.0 / acc[hd:hd + 1, :])
    o_t = jnp.concatenate([jnp.concatenate(h, axis=1) for h in heads], axis=0)
    o_ref[...] = o_t.T.astype(BF16)


def _attn(proj, feat_q, feat_k, batch, seq, q_col, k_col, v_col, tq=1024):
    n_tok = proj.shape[0]
    nq = seq // tq
    n_pairs = ATT_HEADS // HEADS_PER_BLOCK
    n_chains = HEADS_PER_BLOCK * (tq // ATT_SUB)
    return pl.pallas_call(
        functools.partial(_attn_kernel, tq=tq),
        grid=(batch, n_pairs, nq),
        in_specs=[pl.BlockSpec((tq, LANES), lambda b, p, i: (b * nq + i, q_col + p)),
                  pl.BlockSpec((seq, LANES), lambda b, p, i: (b, k_col + p)),
                  pl.BlockSpec((seq, LANES), lambda b, p, i: (b, v_col + p)),
                  pl.BlockSpec((tq, LANES), lambda b, p, i: (b * nq + i, 0)),
                  pl.BlockSpec((seq, LANES), lambda b, p, i: (b, 0))],
        out_specs=pl.BlockSpec((tq, LANES), lambda b, p, i: (b * nq + i, p)),
        out_shape=jax.ShapeDtypeStruct((n_tok, ATT_HEADS * ATT_HEAD_DIM), BF16),
        scratch_shapes=[pltpu.VMEM((HEADS_PER_BLOCK, ATT_VT_ROWS, seq), BF16),
                        pltpu.VMEM((2, n_chains, ATT_TK, ATT_SUB), F32),
                        pltpu.VMEM((2, n_chains, 1, ATT_SUB), F32),
                        pltpu.VMEM((n_chains, ATT_VT_ROWS, ATT_SUB), F32),
                        pltpu.VMEM((n_chains, 1, ATT_SUB), F32)],
        compiler_params=pltpu.CompilerParams(
            dimension_semantics=("arbitrary", "arbitrary", "arbitrary"),
            vmem_limit_bytes=VMEM_LIMIT_BYTES),
        name="attn",
    )(proj, proj, proj, feat_q, feat_k)


def _tail_kernel(x_ref, ys_ref, ya_ref, mod_ref, wo_ref, anw_ref, g1_ref, b1_ref, w1_ref, w2_ref,
                 g2_ref, b2_ref, o_ref, *, alpha, ff_chunk):
    ssm_w = ys_ref.shape[1]
    ya = ya_ref[...].astype(F32)
    ya = ya * lax.rsqrt(jnp.mean(ya * ya, axis=-1, keepdims=True) + RMS_EPS) * anw_ref[...]
    y = (jnp.dot(ys_ref[...], wo_ref[0:ssm_w, :], preferred_element_type=F32)
         + jnp.dot(ya.astype(BF16), wo_ref[ssm_w:, :], preferred_element_type=F32))
    x1 = _layer_norm(alpha * x_ref[...] + (1.0 + mod_ref[2]) * y, g1_ref[...], b1_ref[...])
    h = (x1 * (1.0 + mod_ref[4]) + mod_ref[3]).astype(BF16)
    ff = jnp.zeros(x1.shape, F32)
    for c in range(w1_ref.shape[1] // ff_chunk):
        a = jnp.dot(h, w1_ref[:, c * ff_chunk:(c + 1) * ff_chunk], preferred_element_type=F32)
        a = jnp.maximum(a, 0.0)
        ff = ff + jnp.dot((a * a).astype(BF16), w2_ref[c * ff_chunk:(c + 1) * ff_chunk, :],
                          preferred_element_type=F32)
    o_ref[...] = _layer_norm(alpha * x1 + (1.0 + mod_ref[5]) * ff, g2_ref[...], b2_ref[...])


def _tail(x2, y_ssm, y_att, mod4, w_out, anw, g1, b1, w1, w2, g2, b2, seq, alpha, tm=512,
          ff_chunk=1024):
    n_tok, d = x2.shape
    mix = w_out.shape[0]
    d_ff = w1.shape[1]
    tiles_per_seq = seq // tm
    tok = lambda i: (i, 0)
    const = lambda i: (0, 0)
    resident = functools.partial(pl.BlockSpec, index_map=const, pipeline_mode=pl.Buffered(1))
    return pl.pallas_call(
        functools.partial(_tail_kernel, alpha=alpha, ff_chunk=ff_chunk),
        grid=(n_tok // tm,),
        in_specs=[pl.BlockSpec((tm, d), tok),
                  pl.BlockSpec((tm, y_ssm.shape[1]), tok),
                  pl.BlockSpec((tm, y_att.shape[1]), tok),
                  pl.BlockSpec((None, 6, 1, d), lambda i: (i // tiles_per_seq, 0, 0, 0)),
                  resident((mix, d)),
                  pl.BlockSpec((1, y_att.shape[1]), const),
                  pl.BlockSpec((1, d), const),
                  pl.BlockSpec((1, d), const),
                  resident((d, d_ff)),
                  resident((d_ff, d)),
                  pl.BlockSpec((1, d), const),
                  pl.BlockSpec((1, d), const)],
        out_specs=pl.BlockSpec((tm, d), tok),
        out_shape=jax.ShapeDtypeStruct((n_tok, d), F32),
        compiler_params=pltpu.CompilerParams(
            dimension_semantics=("arbitrary",), vmem_limit_bytes=VMEM_LIMIT_BYTES),
        name="tail",
    )(x2, y_ssm, y_att, mod4, w_out, anw, g1, b1, w1, w2, g2, b2)


def _pad_lanes(v):
    return jnp.pad(v, ((0, 0), (0, LANES - v.shape[1])))


def kernel(x, c, w_ada, b_ada, w_in, conv_w, conv_b, dt_bias, a_log, d_skip, ssm_norm_w, f_bias,
           attn_norm_w, w_out, ln1_g, ln1_b, w_ff_in, w_ff_out, ln2_g, ln2_b):
    batch, seq, d = x.shape
    depth = w_ada.shape[0]
    alpha = (2.0 * depth) ** 0.25
    ssm_w = SSM_HEADS * SSM_HEAD_DIM
    att_w = ATT_HEADS * ATT_HEAD_DIM
    conv_dim = ssm_w + 2 * SSM_GROUPS * SSM_STATE
    o_xbc = ssm_w
    o_dt = o_xbc + conv_dim
    o_q = o_dt + SSM_HEADS
    o_k = o_q + att_w
    o_v = o_k + att_w
    o_f = o_v + att_w
    q_col = (ssm_w + conv_dim) // LANES
    k_col = q_col + att_w // LANES
    v_col = k_col + att_w // LANES

    x2 = x.reshape(batch * seq, d)
    c_pad = jnp.pad(c, ((0, SUBLANES - batch % SUBLANES if batch % SUBLANES else 0), (0, 0)))
    for l in range(depth):
        mod = _ada(c_pad, w_ada[l], b_ada[l][None, :])
        mod4 = mod[:batch].reshape(batch, 6, 1, d)
        wl = w_in[l].astype(BF16)
        w_main = jnp.concatenate([wl[:, :o_dt], wl[:, o_q:o_f]], axis=1)
        w_small = _pad_lanes(jnp.concatenate([wl[:, o_dt:o_q], wl[:, o_f:]], axis=1))
        proj, small = _inproj(x2, mod4, w_main, w_small, seq, q_col * LANES, k_col * LANES,
                              ATT_HEAD_DIM ** -0.5 * LOG2E)

        brow = _pad_lanes(jnp.concatenate([dt_bias[l], f_bias[l]])[None, :])
        alog_row = _pad_lanes(a_log[l][None, :])
        dsk_full = jnp.repeat(d_skip[l], SSM_HEAD_DIM)[None, :]
        y_ssm, feat_q, feat_k = _ssd(proj, small, conv_w[l], conv_b[l][None, :], brow, alog_row,
                                     dsk_full, ssm_norm_w[l][None, :], batch, seq)
        y_att = _attn(proj, feat_q, feat_k, batch, seq, q_col, k_col, v_col)

        x2 = _tail(x2, y_ssm, y_att, mod4, w_out[l].astype(BF16), attn_norm_w[l][None, :],
                   ln1_g[l][None, :], ln1_b[l][None, :], w_ff_in[l].astype(BF16),
                   w_ff_out[l].astype(BF16), ln2_g[l][None, :], ln2_b[l][None, :], seq, alpha)
    return x2.reshape(batch, seq, d)
```

```python
import functools

import jax
import jax.numpy as jnp
import numpy as np
from jax import lax
from jax.experimental import pallas as pl
from jax.experimental.pallas import tpu as pltpu

F32 = jnp.float32
BF16 = jnp.bfloat16
HIGHEST = lax.Precision.HIGHEST

LANES = 128
SUBLANES = 8
BF16_SUBLANES = 16
VMEM_LIMIT_BYTES = 56 * 1024 * 1024

SSM_HEADS = 16
SSM_HEAD_DIM = 64
SSM_GROUPS = 2
SSM_STATE = 128
CONV_WIDTH = 4
CHUNK = 128
ATT_HEADS = 16
ATT_HEAD_DIM = 64
LN_EPS = 1e-5
RMS_EPS = 1e-5

HEADS_PER_GROUP = SSM_HEADS // SSM_GROUPS
HEADS_PER_BLOCK = LANES // ATT_HEAD_DIM
SMALL_DT = 0
SMALL_F = SSM_HEADS
LOG2E = 1.4426950408889634
BIAS_PIECES = 3
BIAS_LANES_PER_HEAD = 2 * BIAS_PIECES


def _silu(v):
    return v * (1.0 / (1.0 + jnp.exp(-v)))


def _softplus(v):
    return jnp.maximum(v, 0.0) + jnp.log(1.0 + jnp.exp(-jnp.abs(v)))


def _split3(c):
    hi = c.astype(BF16).astype(F32)
    r = c - hi
    mid = r.astype(BF16).astype(F32)
    return hi, mid, r - mid


def _pieces(c):
    return jnp.concatenate(_split3(c), axis=1).astype(BF16)


def _layer_norm(v, g, b):
    mu = jnp.mean(v, axis=-1, keepdims=True)
    d = v - mu
    var = jnp.mean(d * d, axis=-1, keepdims=True)
    return d * lax.rsqrt(var + LN_EPS) * g + b


def _ada_kernel(c_ref, w_ref, b_ref, o_ref):
    ca = _silu(c_ref[...]).astype(BF16)
    o_ref[...] = jnp.dot(ca, w_ref[...].astype(BF16), preferred_element_type=F32) + b_ref[...]


def _ada(c_pad, w, b, tn=512):
    rows, d = c_pad.shape
    n = w.shape[1]
    return pl.pallas_call(
        _ada_kernel,
        grid=(n // tn,),
        in_specs=[pl.BlockSpec((rows, d), lambda j: (0, 0)),
                  pl.BlockSpec((d, tn), lambda j: (0, j)),
                  pl.BlockSpec((1, tn), lambda j: (0, j))],
        out_specs=pl.BlockSpec((rows, tn), lambda j: (0, j)),
        out_shape=jax.ShapeDtypeStruct((rows, n), F32),
        name="ada",
    )(c_pad, w, b)


def _inproj_kernel(x_ref, mod_ref, wt_ref, wst_ref, o_ref, os_ref, *, tn, z_hi, q_lo, q_hi, q_scale):
    nt = (((1,), (1,)), ((), ()))
    h = (x_ref[...] * (1.0 + mod_ref[1]) + mod_ref[0]).astype(BF16)
    os_ref[...] = lax.dot_general(h, wst_ref[...], nt, preferred_element_type=F32)
    for c0 in range(0, wt_ref.shape[0], tn):
        acc = lax.dot_general(h, wt_ref[c0:c0 + tn, :], nt, preferred_element_type=F32)
        if c0 < z_hi:
            acc = _silu(acc)
        elif q_lo <= c0 < q_hi:
            acc = acc * q_scale
        o_ref[:, c0:c0 + tn] = acc.astype(BF16)


def _inproj(x2, mod4, wt_main, wt_small, seq, z_hi, q_lo, q_hi, q_scale, tm=512, tn=512):
    n_tok, d = x2.shape
    n_main = wt_main.shape[0]
    assert z_hi % tn == 0 and q_lo % tn == 0 and q_hi % tn == 0 and n_main % tn == 0
    tiles_per_seq = seq // tm
    const = lambda i: (0, 0)
    resident = functools.partial(pl.BlockSpec, index_map=const, pipeline_mode=pl.Buffered(1))
    return pl.pallas_call(
        functools.partial(_inproj_kernel, tn=tn, z_hi=z_hi, q_lo=q_lo, q_hi=q_hi, q_scale=q_scale),
        grid=(n_tok // tm,),
        in_specs=[pl.BlockSpec((tm, d), lambda i: (i, 0)),
                  pl.BlockSpec((None, 6, 1, d), lambda i: (i // tiles_per_seq, 0, 0, 0)),
                  resident((n_main, d)),
                  resident((LANES, d))],
        out_specs=[pl.BlockSpec((tm, n_main), lambda i: (i, 0)),
                   pl.BlockSpec((tm, LANES), lambda i: (i, 0))],
        out_shape=[jax.ShapeDtypeStruct((n_tok, n_main), BF16),
                   jax.ShapeDtypeStruct((n_tok, LANES), F32)],
        compiler_params=pltpu.CompilerParams(
            dimension_semantics=("arbitrary",), vmem_limit_bytes=VMEM_LIMIT_BYTES),
        name="inproj",
    )(x2, mod4, wt_main, wt_small)


def _ssd_kernel(g_ref, x_ref, bc_ref, sm_ref, cw_ref, cb_ref, brow_ref, alog_ref, dsk_ref, nw_ref,
                e_ref, place_ref, ones_ref, shift_ref, xh_ref, bch_ref, y_ref, fq_ref, fk_ref, st_ref,
                car_ref):
    width = SSM_HEADS * SSM_HEAD_DIM
    gw = HEADS_PER_GROUP * SSM_HEAD_DIM
    first_chunk = pl.program_id(1) == 0

    @pl.when(first_chunk)
    def _():
        st_ref[...] = jnp.zeros(st_ref.shape, F32)
        car_ref[...] = jnp.zeros(car_ref.shape, F32)

    cur = jnp.concatenate([x_ref[...], bc_ref[...]], axis=1)
    prev = jnp.concatenate([xh_ref[...], bch_ref[...]], axis=1)
    prev = jnp.where(first_chunk, jnp.zeros_like(prev), prev)
    pad = jnp.zeros((CHUNK - BF16_SUBLANES, cur.shape[1]), BF16)
    shifted = jnp.dot(shift_ref[...], jnp.concatenate([pad, prev, cur], axis=0),
                      preferred_element_type=F32)
    acc = cb_ref[...] + cw_ref[CONV_WIDTH - 1:CONV_WIDTH, :] * cur.astype(F32)
    for k in range(1, CONV_WIDTH):
        acc = acc + (cw_ref[CONV_WIDTH - 1 - k:CONV_WIDTH - k, :]
                     * shifted[(k - 1) * CHUNK:k * CHUNK, :])
    u = _silu(acc)
    xs = u[:, 0:width]
    b_all = u[:, width:width + SSM_GROUPS * SSM_STATE]
    c_all = u[:, width + SSM_GROUPS * SSM_STATE:]

    lane = lax.broadcasted_iota(jnp.int32, (CHUNK, LANES), 1)
    row = lax.broadcasted_iota(jnp.int32, (CHUNK, LANES), 0)
    is_dt = lane < SMALL_F
    is_f = jnp.logical_and(lane >= SMALL_F, lane < SMALL_F + ATT_HEADS)
    t = sm_ref[...] + brow_ref[...]
    sp = _softplus(jnp.where(is_dt, t, -t))
    a_row = -jnp.exp(alog_ref[...])
    dt_tile = jnp.where(is_dt, sp, 0.0)
    val = jnp.where(is_dt, sp * a_row, jnp.where(is_f, -sp, 0.0))
    tril = row >= lane
    cs3 = jnp.dot(tril.astype(BF16), _pieces(val), preferred_element_type=F32)
    cs = cs3[:, 0:LANES] + cs3[:, LANES:2 * LANES] + cs3[:, 2 * LANES:]

    feats = jnp.dot(_pieces((cs + car_ref[...]) * LOG2E), place_ref[...],
                    preferred_element_type=F32) + ones_ref[...]
    fq_ref[...] = feats[:, 0:LANES].astype(BF16)
    fk_ref[...] = feats[:, LANES:].astype(BF16)
    car_ref[...] = car_ref[...] + jnp.where(is_f[0:1, :], cs[CHUNK - 1:CHUNK, :], 0.0)

    cs_t = cs.T
    dt_t = dt_tile.T
    full = jnp.dot(_pieces(jnp.concatenate([cs, dt_tile], axis=0)), e_ref[...],
                   preferred_element_type=F32)
    acs_full = full[0:CHUNK]
    dt_full = full[CHUNK:]
    atot_full = acs_full[CHUNK - 1:CHUNK, :]
    dec_out = jnp.exp(acs_full)
    dec_end = jnp.exp(atot_full - acs_full)
    chunk_dec = jnp.exp(atot_full)
    xs_b = xs.astype(BF16)
    xw = (xs * (dec_end * dt_full)).astype(BF16)
    lane_lo = lane < SSM_HEAD_DIM

    ydiag_blocks = []
    yoff_blocks = []
    for g in range(SSM_GROUPS):
        bg = b_all[:, g * SSM_STATE:(g + 1) * SSM_STATE]
        cg = c_all[:, g * SSM_STATE:(g + 1) * SSM_STATE].astype(BF16)
        cbm = lax.dot_general(cg, bg.astype(BF16), (((1,), (1,)), ((), ())),
                              preferred_element_type=F32)
        for j in range(HEADS_PER_GROUP // HEADS_PER_BLOCK):
            h0 = g * HEADS_PER_GROUP + HEADS_PER_BLOCK * j
            xp = xs_b[:, h0 * SSM_HEAD_DIM:h0 * SSM_HEAD_DIM + LANES]
            res = []
            for h in range(h0, h0 + HEADS_PER_BLOCK):
                seg = cs[:, h:h + 1] - cs_t[h:h + 1, :]
                dec = jnp.exp(jnp.where(tril, seg, -1e30))
                sc = (cbm * dec * dt_t[h:h + 1, :]).astype(BF16)
                res.append(jnp.dot(sc, xp, preferred_element_type=F32))
            ydiag_blocks.append(jnp.where(lane_lo, res[0], res[1]))
        st = st_ref[g]
        yoff_blocks.append(jnp.dot(cg, st.astype(BF16), preferred_element_type=F32)
                           * dec_out[:, g * gw:(g + 1) * gw])
        st_ref[g] = st * chunk_dec[:, g * gw:(g + 1) * gw] + jnp.dot(
            bg.T.astype(BF16), xw[:, g * gw:(g + 1) * gw], preferred_element_type=F32)
    y = (jnp.concatenate(ydiag_blocks, axis=1) + jnp.concatenate(yoff_blocks, axis=1)
         + dsk_ref[...] * xs)
    gated = y * g_ref[...].astype(F32)
    ms = jnp.mean(gated * gated, axis=-1, keepdims=True)
    y_ref[...] = (gated * lax.rsqrt(ms + RMS_EPS) * nw_ref[...]).astype(BF16)


def _bias_routing():
    place = np.zeros((BIAS_PIECES * LANES, 2 * LANES), np.float32)
    ones_row = np.zeros((1, 2 * LANES), np.float32)
    for h in range(ATT_HEADS):
        base = BIAS_LANES_PER_HEAD * h
        for t in range(BIAS_PIECES):
            place[t * LANES + SMALL_F + h, base + t] = 1.0
            place[t * LANES + SMALL_F + h, LANES + base + BIAS_PIECES + t] = -1.0
            ones_row[0, base + BIAS_PIECES + t] = 1.0
            ones_row[0, LANES + base + t] = 1.0
    return jnp.asarray(place, BF16), jnp.asarray(ones_row, F32)


def _head_expansion():
    expand = np.zeros((BIAS_PIECES * LANES, SSM_HEADS * SSM_HEAD_DIM), np.float32)
    for t in range(BIAS_PIECES):
        for h in range(SSM_HEADS):
            expand[t * LANES + SMALL_DT + h, h * SSM_HEAD_DIM:(h + 1) * SSM_HEAD_DIM] = 1.0
    return jnp.asarray(expand, BF16)


def _conv_shifts():
    shift = np.zeros(((CONV_WIDTH - 1) * CHUNK, 2 * CHUNK), np.float32)
    for k in range(1, CONV_WIDTH):
        for t in range(CHUNK):
            shift[(k - 1) * CHUNK + t, CHUNK + t - k] = 1.0
    return jnp.asarray(shift, BF16)


def _ssd(proj, small, conv_w, conv_b, brow, alog_row, dsk_full, norm_w, batch, seq):
    place, ones_row = _bias_routing()
    shift = _conv_shifts()
    expand = _head_expansion()
    n_tok = proj.shape[0]
    width = SSM_HEADS * SSM_HEAD_DIM
    bc_w = 2 * SSM_GROUPS * SSM_STATE
    conv_dim = width + bc_w
    nc = seq // CHUNK
    tok = lambda b, c: (b * nc + c, 0)
    const = lambda b, c: (0, 0)
    halo_block = lambda b, c: jnp.maximum((b * nc + c) * (CHUNK // BF16_SUBLANES) - 1, 0)
    return pl.pallas_call(
        _ssd_kernel,
        grid=(batch, nc),
        in_specs=[pl.BlockSpec((CHUNK, width), tok),
                  pl.BlockSpec((CHUNK, width), lambda b, c: (b * nc + c, 1)),
                  pl.BlockSpec((CHUNK, bc_w), lambda b, c: (b * nc + c, 2 * width // bc_w)),
                  pl.BlockSpec((CHUNK, LANES), tok),
                  pl.BlockSpec((CONV_WIDTH, conv_dim), const),
                  pl.BlockSpec((1, conv_dim), const),
                  pl.BlockSpec((1, LANES), const),
                  pl.BlockSpec((1, LANES), const),
                  pl.BlockSpec((1, width), const),
                  pl.BlockSpec((1, width), const),
                  pl.BlockSpec(expand.shape, const),
                  pl.BlockSpec(place.shape, const),
                  pl.BlockSpec(ones_row.shape, const),
                  pl.BlockSpec(shift.shape, const),
                  pl.BlockSpec((BF16_SUBLANES, width), lambda b, c: (halo_block(b, c), 1)),
                  pl.BlockSpec((BF16_SUBLANES, bc_w),
                               lambda b, c: (halo_block(b, c), 2 * width // bc_w))],
        out_specs=[pl.BlockSpec((CHUNK, width), tok),
                   pl.BlockSpec((CHUNK, LANES), tok),
                   pl.BlockSpec((CHUNK, LANES), tok)],
        out_shape=[jax.ShapeDtypeStruct((n_tok, width), BF16),
                   jax.ShapeDtypeStruct((n_tok, LANES), BF16),
                   jax.ShapeDtypeStruct((n_tok, LANES), BF16)],
        scratch_shapes=[pltpu.VMEM((SSM_GROUPS, SSM_STATE, HEADS_PER_GROUP * SSM_HEAD_DIM), F32),
                        pltpu.VMEM((1, LANES), F32)],
        compiler_params=pltpu.CompilerParams(
            dimension_semantics=("arbitrary", "arbitrary"), vmem_limit_bytes=VMEM_LIMIT_BYTES),
        name="ssd",
    )(proj, proj, proj, small, conv_w, conv_b, brow, alog_row, dsk_full, norm_w, expand, place,
      ones_row, shift, proj, proj)


ATT_TK = 256
ATT_SUB = 256
ATT_UNROLL = 4
ATT_VT_ROWS = ATT_HEAD_DIM + BF16_SUBLANES


def _attn_kernel(q_ref, k_ref, v_ref, fq_ref, fk_ref, o_ref, vt_ref, st_ref, mx_ref, acc_ref, m_ref,
                 *, tq):
    pair = pl.program_id(1)
    qi = pl.program_id(2)
    seq = k_ref.shape[0]
    n_sub = tq // ATT_SUB
    assert n_sub % ATT_UNROLL == 0
    hd = ATT_HEAD_DIM

    @pl.when(qi == 0)
    def _():
        def prep(r, carry):
            rows = pl.ds(pl.multiple_of(r * ATT_TK, ATT_TK), ATT_TK)
            vt = v_ref[rows, :].astype(F32).T.astype(BF16)
            for i in range(HEADS_PER_BLOCK):
                vt_ref[i, 0:hd, rows] = vt[i * hd:(i + 1) * hd, :]
                vt_ref[i, hd:, rows] = jnp.ones((ATT_VT_ROWS - hd, ATT_TK), BF16)
            return carry

        lax.fori_loop(0, seq // ATT_TK, prep, 0)

    lane = lax.broadcasted_iota(jnp.int32, (tq, LANES), 1)
    q = q_ref[...]
    fq = fq_ref[...]
    q_aug = []
    for i in range(HEADS_PER_BLOCK):
        first = BIAS_LANES_PER_HEAD * (HEADS_PER_BLOCK * pair + i)
        own_q = (lane < hd) if i == 0 else (lane >= hd)
        own_f = jnp.logical_and(lane >= first, lane < first + BIAS_LANES_PER_HEAD)
        q_aug.append(jnp.concatenate([jnp.where(own_q, q, jnp.zeros_like(q)),
                                      jnp.where(own_f, fq, jnp.zeros_like(fq))], axis=1))
    causal = (lax.broadcasted_iota(jnp.int32, (ATT_TK, ATT_SUB), 0)
              <= lax.broadcasted_iota(jnp.int32, (ATT_TK, ATT_SUB), 1))
    chains = [(i, s) for i in range(HEADS_PER_BLOCK) for s in range(n_sub)]

    def head_scores(i, block, s_from=0):
        rows = pl.ds(pl.multiple_of(block * ATT_TK, ATT_TK), ATT_TK)
        kb = jnp.concatenate([k_ref[rows, :], fk_ref[rows, :]], axis=1)
        qa = q_aug[i][s_from * ATT_SUB:, :]
        return lax.dot_general(kb, qa, (((1,), (1,)), ((), ())), preferred_element_type=F32)

    def update(c, block, st, st_max):
        m = m_ref[c]
        m_new = jnp.maximum(m, st_max)
        pt = jnp.exp2(st - m_new).astype(BF16)
        start = pl.multiple_of(block * ATT_TK, ATT_TK)
        vt = vt_ref[chains[c][0], :, pl.ds(start, ATT_TK)]
        acc_ref[c] = (jnp.exp2(m - m_new) * acc_ref[c]
                      + jnp.dot(vt, pt, preferred_element_type=F32))
        m_ref[c] = m_new

    def stash(slot, block):
        for i in range(HEADS_PER_BLOCK):
            st = head_scores(i, block)
            for s in range(n_sub):
                part = st[:, s * ATT_SUB:(s + 1) * ATT_SUB]
                st_ref[slot, i * n_sub + s] = part
                mx_ref[slot, i * n_sub + s] = jnp.max(part, axis=0, keepdims=True)

    every = list(range(len(chains)))
    for c in every:
        m_ref[c] = jnp.full((1, ATT_SUB), -1e30, F32)
        acc_ref[c] = jnp.zeros((ATT_VT_ROWS, ATT_SUB), F32)

    first_diag = qi * n_sub
    diag = []
    for e in range(n_sub):
        for i in range(HEADS_PER_BLOCK):
            st = head_scores(i, first_diag + e, s_from=e)
            for s in range(e, n_sub):
                part = st[:, (s - e) * ATT_SUB:(s - e + 1) * ATT_SUB]
                if s == e:
                    part = jnp.where(causal, part, -1e30)
                diag.append((i * n_sub + s, first_diag + e, part))
    stash(0, 0)
    for c, block, part in diag:
        update(c, block, part, jnp.max(part, axis=0, keepdims=True))

    def half_step(block, slot, prefetch=True):
        if prefetch:
            stash(1 - slot, block + 1)
        for c in every:
            update(c, block, st_ref[slot, c], mx_ref[slot, c])

    def body(j, carry):
        for t in range(ATT_UNROLL):
            half_step(ATT_UNROLL * j + t, t % 2)
        return carry

    lax.fori_loop(0, qi * (n_sub // ATT_UNROLL) - 1, body, 0)

    @pl.when(qi > 0)
    def _():
        for t in range(ATT_UNROLL):
            half_step(first_diag - ATT_UNROLL + t, t % 2, prefetch=t < ATT_UNROLL - 1)

    heads = [[None] * n_sub for _ in range(HEADS_PER_BLOCK)]
    for c, (i, s) in enumerate(chains):
        acc = acc_ref[c]
        heads[i][s] = acc[0:hd, :] * (1.0 / acc[hd:hd + 1, :])
    o_t = jnp.concatenate([jnp.concatenate(h, axis=1) for h in heads], axis=0)
    o_ref[...] = o_t.T.astype(BF16)


def _attn(proj, feat_q, feat_k, batch, seq, q_col, k_col, v_col, tq=1024):
    n_tok = proj.shape[0]
    nq = seq // tq
    n_pairs = ATT_HEADS // HEADS_PER_BLOCK
    n_chains = HEADS_PER_BLOCK * (tq // ATT_SUB)
    return pl.pallas_call(
        functools.partial(_attn_kernel, tq=tq),
        grid=(batch, n_pairs, nq),
        in_specs=[pl.BlockSpec((tq, LANES), lambda b, p, i: (b * nq + i, q_col + p)),
                  pl.BlockSpec((seq, LANES), lambda b, p, i: (b, k_col + p)),
                  pl.BlockSpec((seq, LANES), lambda b, p, i: (b, v_col + p)),
                  pl.BlockSpec((tq, LANES), lambda b, p, i: (b * nq + i, 0)),
                  pl.BlockSpec((seq, LANES), lambda b, p, i: (b, 0))],
        out_specs=pl.BlockSpec((tq, LANES), lambda b, p, i: (b * nq + i, p)),
        out_shape=jax.ShapeDtypeStruct((n_tok, ATT_HEADS * ATT_HEAD_DIM), BF16),
        scratch_shapes=[pltpu.VMEM((HEADS_PER_BLOCK, ATT_VT_ROWS, seq), BF16),
                        pltpu.VMEM((2, n_chains, ATT_TK, ATT_SUB), F32),
                        pltpu.VMEM((2, n_chains, 1, ATT_SUB), F32),
                        pltpu.VMEM((n_chains, ATT_VT_ROWS, ATT_SUB), F32),
                        pltpu.VMEM((n_chains, 1, ATT_SUB), F32)],
        compiler_params=pltpu.CompilerParams(
            dimension_semantics=("arbitrary", "arbitrary", "arbitrary"),
            vmem_limit_bytes=VMEM_LIMIT_BYTES),
        name="attn",
    )(proj, proj, proj, feat_q, feat_k)


def _tail_kernel(x_ref, ys_ref, ya_ref, mod_ref, wo_ref, anw_ref, g1_ref, b1_ref, w1_ref, w2_ref,
                 g2_ref, b2_ref, o_ref, *, alpha, ff_chunk):
    ssm_w = ys_ref.shape[1]
    ya = ya_ref[...].astype(F32)
    ya = ya * lax.rsqrt(jnp.mean(ya * ya, axis=-1, keepdims=True) + RMS_EPS) * anw_ref[...]
    y = (jnp.dot(ys_ref[...], wo_ref[0:ssm_w, :], preferred_element_type=F32)
         + jnp.dot(ya.astype(BF16), wo_ref[ssm_w:, :], preferred_element_type=F32))
    x1 = _layer_norm(alpha * x_ref[...] + (1.0 + mod_ref[2]) * y, g1_ref[...], b1_ref[...])
    h = (x1 * (1.0 + mod_ref[4]) + mod_ref[3]).astype(BF16)
    ff = jnp.zeros(x1.shape, F32)
    for c in range(w1_ref.shape[1] // ff_chunk):
        a = jnp.dot(h, w1_ref[:, c * ff_chunk:(c + 1) * ff_chunk], preferred_element_type=F32)
        a = jnp.maximum(a, 0.0)
        ff = ff + jnp.dot((a * a).astype(BF16), w2_ref[c * ff_chunk:(c + 1) * ff_chunk, :],
                          preferred_element_type=F32)
    o_ref[...] = _layer_norm(alpha * x1 + (1.0 + mod_ref[5]) * ff, g2_ref[...], b2_ref[...])


def _tail(x2, y_ssm, y_att, mod4, w_out, anw, g1, b1, w1, w2, g2, b2, seq, alpha, tm=512,
          ff_chunk=1024):
    n_tok, d = x2.shape
    mix = w_out.shape[0]
    d_ff = w1.shape[1]
    tiles_per_seq = seq // tm
    tok = lambda i: (i, 0)
    const = lambda i: (0, 0)
    resident = functools.partial(pl.BlockSpec, index_map=const, pipeline_mode=pl.Buffered(1))
    return pl.pallas_call(
        functools.partial(_tail_kernel, alpha=alpha, ff_chunk=ff_chunk),
        grid=(n_tok // tm,),
        in_specs=[pl.BlockSpec((tm, d), tok),
                  pl.BlockSpec((tm, y_ssm.shape[1]), tok),
                  pl.BlockSpec((tm, y_att.shape[1]), tok),
                  pl.BlockSpec((None, 6, 1, d), lambda i: (i // tiles_per_seq, 0, 0, 0)),
                  resident((mix, d)),
                  pl.BlockSpec((1, y_att.shape[1]), const),
                  pl.BlockSpec((1, d), const),
                  pl.BlockSpec((1, d), const),
                  resident((d, d_ff)),
                  resident((d_ff, d)),
                  pl.BlockSpec((1, d), const),
                  pl.BlockSpec((1, d), const)],
        out_specs=pl.BlockSpec((tm, d), tok),
        out_shape=jax.ShapeDtypeStruct((n_tok, d), F32),
        compiler_params=pltpu.CompilerParams(
            dimension_semantics=("arbitrary",), vmem_limit_bytes=VMEM_LIMIT_BYTES),
        name="tail",
    )(x2, y_ssm, y_att, mod4, w_out, anw, g1, b1, w1, w2, g2, b2)


def _pad_lanes(v):
    return jnp.pad(v, ((0, 0), (0, LANES - v.shape[1])))


def kernel(x, c, w_ada, b_ada, w_in, conv_w, conv_b, dt_bias, a_log, d_skip, ssm_norm_w, f_bias,
           attn_norm_w, w_out, ln1_g, ln1_b, w_ff_in, w_ff_out, ln2_g, ln2_b):
    batch, seq, d = x.shape
    depth = w_ada.shape[0]
    alpha = (2.0 * depth) ** 0.25
    ssm_w = SSM_HEADS * SSM_HEAD_DIM
    att_w = ATT_HEADS * ATT_HEAD_DIM
    conv_dim = ssm_w + 2 * SSM_GROUPS * SSM_STATE
    o_xbc = ssm_w
    o_dt = o_xbc + conv_dim
    o_q = o_dt + SSM_HEADS
    o_k = o_q + att_w
    o_v = o_k + att_w
    o_f = o_v + att_w
    q_col = (ssm_w + conv_dim) // LANES
    k_col = q_col + att_w // LANES
    v_col = k_col + att_w // LANES

    x2 = x.reshape(batch * seq, d)
    c_pad = jnp.pad(c, ((0, SUBLANES - batch % SUBLANES if batch % SUBLANES else 0), (0, 0)))
    for l in range(depth):
        mod = _ada(c_pad, w_ada[l], b_ada[l][None, :])
        mod4 = mod[:batch].reshape(batch, 6, 1, d)
        wt = jnp.swapaxes(w_in[l], 0, 1).astype(BF16)
        wt_main = jnp.concatenate([wt[:o_dt], wt[o_q:o_f]], axis=0)
        wt_small = jnp.concatenate([wt[o_dt:o_q], wt[o_f:]], axis=0)
        wt_small = jnp.pad(wt_small, ((0, LANES - wt_small.shape[0]), (0, 0)))
        proj, small = _inproj(x2, mod4, wt_main, wt_small, seq, ssm_w, q_col * LANES,
                              k_col * LANES, ATT_HEAD_DIM ** -0.5 * LOG2E)

        brow = _pad_lanes(jnp.concatenate([dt_bias[l], f_bias[l]])[None, :])
        alog_row = _pad_lanes(a_log[l][None, :])
        dsk_full = jnp.repeat(d_skip[l], SSM_HEAD_DIM)[None, :]
        y_ssm, feat_q, feat_k = _ssd(proj, small, conv_w[l], conv_b[l][None, :], brow, alog_row,
                                     dsk_full, ssm_norm_w[l][None, :], batch, seq)
        y_att = _attn(proj, feat_q, feat_k, batch, seq, q_col, k_col, v_col)

        x2 = _tail(x2, y_ssm, y_att, mod4, w_out[l].astype(BF16), attn_norm_w[l][None, :],
                   ln1_g[l][None, :], ln1_b[l][None, :], w_ff_in[l].astype(BF16),
                   w_ff_out[l].astype(BF16), ln2_g[l][None, :], ln2_b[l][None, :], seq, alpha)
    return x2.reshape(batch, seq, d)
```

```python
import functools

import jax
import jax.numpy as jnp
import numpy as np
from jax import lax
from jax.experimental import pallas as pl
from jax.experimental.pallas import tpu as pltpu

F32 = jnp.float32
BF16 = jnp.bfloat16
HIGHEST = lax.Precision.HIGHEST

LANES = 128
SUBLANES = 8
BF16_SUBLANES = 16
VMEM_LIMIT_BYTES = 56 * 1024 * 1024

SSM_HEADS = 16
SSM_HEAD_DIM = 64
SSM_GROUPS = 2
SSM_STATE = 128
CONV_WIDTH = 4
CHUNK = 128
ATT_HEADS = 16
ATT_HEAD_DIM = 64
LN_EPS = 1e-5
RMS_EPS = 1e-5

HEADS_PER_GROUP = SSM_HEADS // SSM_GROUPS
HEADS_PER_BLOCK = LANES // ATT_HEAD_DIM
SMALL_DT = 0
SMALL_F = SSM_HEADS
LOG2E = 1.4426950408889634
BIAS_PIECES = 3
BIAS_LANES_PER_HEAD = 2 * BIAS_PIECES


def _silu(v):
    return v * (1.0 / (1.0 + jnp.exp(-v)))


def _softplus(v):
    return jnp.maximum(v, 0.0) + jnp.log(1.0 + jnp.exp(-jnp.abs(v)))


def _split3(c):
    hi = c.astype(BF16).astype(F32)
    r = c - hi
    mid = r.astype(BF16).astype(F32)
    return hi, mid, r - mid


def _pieces(c):
    return jnp.concatenate(_split3(c), axis=1).astype(BF16)


def _layer_norm(v, g, b):
    mu = jnp.mean(v, axis=-1, keepdims=True)
    d = v - mu
    var = jnp.mean(d * d, axis=-1, keepdims=True)
    return d * lax.rsqrt(var + LN_EPS) * g + b


def _ada_kernel(c_ref, w_ref, b_ref, o_ref):
    ca = _silu(c_ref[...]).astype(BF16)
    o_ref[...] = jnp.dot(ca, w_ref[...].astype(BF16), preferred_element_type=F32) + b_ref[...]


def _ada(c_pad, w, b, tn=512):
    rows, d = c_pad.shape
    n = w.shape[1]
    return pl.pallas_call(
        _ada_kernel,
        grid=(n // tn,),
        in_specs=[pl.BlockSpec((rows, d), lambda j: (0, 0)),
                  pl.BlockSpec((d, tn), lambda j: (0, j)),
                  pl.BlockSpec((1, tn), lambda j: (0, j))],
        out_specs=pl.BlockSpec((rows, tn), lambda j: (0, j)),
        out_shape=jax.ShapeDtypeStruct((rows, n), F32),
        name="ada",
    )(c_pad, w, b)


def _inproj_kernel(x_ref, mod_ref, wt_ref, wst_ref, o_ref, os_ref, ovt_ref, *, tn, z_hi, q_lo, q_hi,
                   q_scale):
    nt = (((1,), (1,)), ((), ()))
    v_lo = o_ref.shape[1]
    h = (x_ref[...] * (1.0 + mod_ref[1]) + mod_ref[0]).astype(BF16)
    os_ref[...] = lax.dot_general(h, wst_ref[...], nt, preferred_element_type=F32)
    for c0 in range(0, wt_ref.shape[0], tn):
        acc = lax.dot_general(h, wt_ref[c0:c0 + tn, :], nt, preferred_element_type=F32)
        if c0 >= v_lo:
            ovt_ref[c0 - v_lo:c0 - v_lo + tn, :] = acc.T.astype(BF16)
            continue
        if c0 < z_hi:
            acc = _silu(acc)
        elif q_lo <= c0 < q_hi:
            acc = acc * q_scale
        o_ref[:, c0:c0 + tn] = acc.astype(BF16)


def _inproj(x2, mod4, wt_main, wt_small, batch, seq, z_hi, q_lo, q_hi, v_lo, q_scale, tm=512,
            tn=512):
    n_tok, d = x2.shape
    n_main = wt_main.shape[0]
    assert all(c % tn == 0 for c in (z_hi, q_lo, q_hi, v_lo, n_main))
    tiles_per_seq = seq // tm
    const = lambda i: (0, 0)
    resident = functools.partial(pl.BlockSpec, index_map=const, pipeline_mode=pl.Buffered(1))
    return pl.pallas_call(
        functools.partial(_inproj_kernel, tn=tn, z_hi=z_hi, q_lo=q_lo, q_hi=q_hi, q_scale=q_scale),
        grid=(n_tok // tm,),
        in_specs=[pl.BlockSpec((tm, d), lambda i: (i, 0)),
                  pl.BlockSpec((None, 6, 1, d), lambda i: (i // tiles_per_seq, 0, 0, 0)),
                  resident((n_main, d)),
                  resident((LANES, d))],
        out_specs=[pl.BlockSpec((tm, v_lo), lambda i: (i, 0)),
                   pl.BlockSpec((tm, LANES), lambda i: (i, 0)),
                   pl.BlockSpec((None, n_main - v_lo, tm),
                                lambda i: (i // tiles_per_seq, 0, i % tiles_per_seq))],
        out_shape=[jax.ShapeDtypeStruct((n_tok, v_lo), BF16),
                   jax.ShapeDtypeStruct((n_tok, LANES), F32),
                   jax.ShapeDtypeStruct((batch, n_main - v_lo, seq), BF16)],
        compiler_params=pltpu.CompilerParams(
            dimension_semantics=("arbitrary",), vmem_limit_bytes=VMEM_LIMIT_BYTES),
        name="inproj",
    )(x2, mod4, wt_main, wt_small)


def _ssd_kernel(g_ref, x_ref, bc_ref, sm_ref, cw_ref, cb_ref, brow_ref, alog_ref, dsk_ref, nw_ref,
                e_ref, place_ref, ones_ref, shift_ref, xh_ref, bch_ref, y_ref, fq_ref, fk_ref, st_ref,
                car_ref):
    width = SSM_HEADS * SSM_HEAD_DIM
    gw = HEADS_PER_GROUP * SSM_HEAD_DIM
    first_chunk = pl.program_id(1) == 0

    @pl.when(first_chunk)
    def _():
        st_ref[...] = jnp.zeros(st_ref.shape, F32)
        car_ref[...] = jnp.zeros(car_ref.shape, F32)

    cur = jnp.concatenate([x_ref[...], bc_ref[...]], axis=1)
    prev = jnp.concatenate([xh_ref[...], bch_ref[...]], axis=1)
    prev = jnp.where(first_chunk, jnp.zeros_like(prev), prev)
    pad = jnp.zeros((CHUNK - BF16_SUBLANES, cur.shape[1]), BF16)
    shifted = jnp.dot(shift_ref[...], jnp.concatenate([pad, prev, cur], axis=0),
                      preferred_element_type=F32)
    acc = cb_ref[...] + cw_ref[CONV_WIDTH - 1:CONV_WIDTH, :] * cur.astype(F32)
    for k in range(1, CONV_WIDTH):
        acc = acc + (cw_ref[CONV_WIDTH - 1 - k:CONV_WIDTH - k, :]
                     * shifted[(k - 1) * CHUNK:k * CHUNK, :])
    u = _silu(acc)
    xs = u[:, 0:width]
    b_all = u[:, width:width + SSM_GROUPS * SSM_STATE]
    c_all = u[:, width + SSM_GROUPS * SSM_STATE:]

    lane = lax.broadcasted_iota(jnp.int32, (CHUNK, LANES), 1)
    row = lax.broadcasted_iota(jnp.int32, (CHUNK, LANES), 0)
    is_dt = lane < SMALL_F
    is_f = jnp.logical_and(lane >= SMALL_F, lane < SMALL_F + ATT_HEADS)
    t = sm_ref[...] + brow_ref[...]
    sp = _softplus(jnp.where(is_dt, t, -t))
    a_row = -jnp.exp(alog_ref[...])
    dt_tile = jnp.where(is_dt, sp, 0.0)
    val = jnp.where(is_dt, sp * a_row, jnp.where(is_f, -sp, 0.0))
    tril = row >= lane
    cs3 = jnp.dot(tril.astype(BF16), _pieces(val), preferred_element_type=F32)
    cs = cs3[:, 0:LANES] + cs3[:, LANES:2 * LANES] + cs3[:, 2 * LANES:]

    feats = jnp.dot(_pieces((cs + car_ref[...]) * LOG2E), place_ref[...],
                    preferred_element_type=F32) + ones_ref[...]
    fq_ref[...] = feats[:, 0:LANES].astype(BF16)
    fk_ref[...] = feats[:, LANES:].astype(BF16)
    car_ref[...] = car_ref[...] + jnp.where(is_f[0:1, :], cs[CHUNK - 1:CHUNK, :], 0.0)

    cs_t = cs.T
    dt_t = dt_tile.T
    full = jnp.dot(_pieces(jnp.concatenate([cs, dt_tile], axis=0)), e_ref[...],
                   preferred_element_type=F32)
    acs_full = full[0:CHUNK]
    dt_full = full[CHUNK:]
    atot_full = acs_full[CHUNK - 1:CHUNK, :]
    dec_out = jnp.exp(acs_full)
    dec_end = jnp.exp(atot_full - acs_full)
    chunk_dec = jnp.exp(atot_full)
    xs_b = xs.astype(BF16)
    xw = (xs * (dec_end * dt_full)).astype(BF16)
    lane_lo = lane < SSM_HEAD_DIM

    ydiag_blocks = []
    yoff_blocks = []
    for g in range(SSM_GROUPS):
        bg = b_all[:, g * SSM_STATE:(g + 1) * SSM_STATE]
        cg = c_all[:, g * SSM_STATE:(g + 1) * SSM_STATE].astype(BF16)
        cbm = lax.dot_general(cg, bg.astype(BF16), (((1,), (1,)), ((), ())),
                              preferred_element_type=F32)
        for j in range(HEADS_PER_GROUP // HEADS_PER_BLOCK):
            h0 = g * HEADS_PER_GROUP + HEADS_PER_BLOCK * j
            xp = xs_b[:, h0 * SSM_HEAD_DIM:h0 * SSM_HEAD_DIM + LANES]
            res = []
            for h in range(h0, h0 + HEADS_PER_BLOCK):
                seg = cs[:, h:h + 1] - cs_t[h:h + 1, :]
                dec = jnp.exp(jnp.where(tril, seg, -1e30))
                sc = (cbm * dec * dt_t[h:h + 1, :]).astype(BF16)
                res.append(jnp.dot(sc, xp, preferred_element_type=F32))
            ydiag_blocks.append(jnp.where(lane_lo, res[0], res[1]))
        st = st_ref[g]
        yoff_blocks.append(jnp.dot(cg, st.astype(BF16), preferred_element_type=F32)
                           * dec_out[:, g * gw:(g + 1) * gw])
        st_ref[g] = st * chunk_dec[:, g * gw:(g + 1) * gw] + jnp.dot(
            bg.T.astype(BF16), xw[:, g * gw:(g + 1) * gw], preferred_element_type=F32)
    y = (jnp.concatenate(ydiag_blocks, axis=1) + jnp.concatenate(yoff_blocks, axis=1)
         + dsk_ref[...] * xs)
    gated = y * g_ref[...].astype(F32)
    ms = jnp.mean(gated * gated, axis=-1, keepdims=True)
    y_ref[...] = (gated * lax.rsqrt(ms + RMS_EPS) * nw_ref[...]).astype(BF16)


def _bias_routing():
    place = np.zeros((BIAS_PIECES * LANES, 2 * LANES), np.float32)
    ones_row = np.zeros((1, 2 * LANES), np.float32)
    for h in range(ATT_HEADS):
        base = BIAS_LANES_PER_HEAD * h
        for t in range(BIAS_PIECES):
            place[t * LANES + SMALL_F + h, base + t] = 1.0
            place[t * LANES + SMALL_F + h, LANES + base + BIAS_PIECES + t] = -1.0
            ones_row[0, base + BIAS_PIECES + t] = 1.0
            ones_row[0, LANES + base + t] = 1.0
    return jnp.asarray(place, BF16), jnp.asarray(ones_row, F32)


def _head_expansion():
    expand = np.zeros((BIAS_PIECES * LANES, SSM_HEADS * SSM_HEAD_DIM), np.float32)
    for t in range(BIAS_PIECES):
        for h in range(SSM_HEADS):
            expand[t * LANES + SMALL_DT + h, h * SSM_HEAD_DIM:(h + 1) * SSM_HEAD_DIM] = 1.0
    return jnp.asarray(expand, BF16)


def _conv_shifts():
    shift = np.zeros(((CONV_WIDTH - 1) * CHUNK, 2 * CHUNK), np.float32)
    for k in range(1, CONV_WIDTH):
        for t in range(CHUNK):
            shift[(k - 1) * CHUNK + t, CHUNK + t - k] = 1.0
    return jnp.asarray(shift, BF16)


def _ssd(proj, small, conv_w, conv_b, brow, alog_row, dsk_full, norm_w, batch, seq):
    place, ones_row = _bias_routing()
    shift = _conv_shifts()
    expand = _head_expansion()
    n_tok = proj.shape[0]
    width = SSM_HEADS * SSM_HEAD_DIM
    bc_w = 2 * SSM_GROUPS * SSM_STATE
    conv_dim = width + bc_w
    nc = seq // CHUNK
    tok = lambda b, c: (b * nc + c, 0)
    const = lambda b, c: (0, 0)
    halo_block = lambda b, c: jnp.maximum((b * nc + c) * (CHUNK // BF16_SUBLANES) - 1, 0)
    return pl.pallas_call(
        _ssd_kernel,
        grid=(batch, nc),
        in_specs=[pl.BlockSpec((CHUNK, width), tok),
                  pl.BlockSpec((CHUNK, width), lambda b, c: (b * nc + c, 1)),
                  pl.BlockSpec((CHUNK, bc_w), lambda b, c: (b * nc + c, 2 * width // bc_w)),
                  pl.BlockSpec((CHUNK, LANES), tok),
                  pl.BlockSpec((CONV_WIDTH, conv_dim), const),
                  pl.BlockSpec((1, conv_dim), const),
                  pl.BlockSpec((1, LANES), const),
                  pl.BlockSpec((1, LANES), const),
                  pl.BlockSpec((1, width), const),
                  pl.BlockSpec((1, width), const),
                  pl.BlockSpec(expand.shape, const),
                  pl.BlockSpec(place.shape, const),
                  pl.BlockSpec(ones_row.shape, const),
                  pl.BlockSpec(shift.shape, const),
                  pl.BlockSpec((BF16_SUBLANES, width), lambda b, c: (halo_block(b, c), 1)),
                  pl.BlockSpec((BF16_SUBLANES, bc_w),
                               lambda b, c: (halo_block(b, c), 2 * width // bc_w))],
        out_specs=[pl.BlockSpec((CHUNK, width), tok),
                   pl.BlockSpec((CHUNK, LANES), tok),
                   pl.BlockSpec((CHUNK, LANES), tok)],
        out_shape=[jax.ShapeDtypeStruct((n_tok, width), BF16),
                   jax.ShapeDtypeStruct((n_tok, LANES), BF16),
                   jax.ShapeDtypeStruct((n_tok, LANES), BF16)],
        scratch_shapes=[pltpu.VMEM((SSM_GROUPS, SSM_STATE, HEADS_PER_GROUP * SSM_HEAD_DIM), F32),
                        pltpu.VMEM((1, LANES), F32)],
        compiler_params=pltpu.CompilerParams(
            dimension_semantics=("arbitrary", "arbitrary"), vmem_limit_bytes=VMEM_LIMIT_BYTES),
        name="ssd",
    )(proj, proj, proj, small, conv_w, conv_b, brow, alog_row, dsk_full, norm_w, expand, place,
      ones_row, shift, proj, proj)


ATT_TK = 256
ATT_SUB = 256
ATT_UNROLL = 4
ATT_VT_ROWS = ATT_HEAD_DIM + BF16_SUBLANES


def _attn_kernel(q_ref, k_ref, vt_ref, fq_ref, fk_ref, o_ref, st_ref, mx_ref, acc_ref, m_ref, *, tq):
    pair = pl.program_id(1)
    qi = pl.program_id(2)
    n_sub = tq // ATT_SUB
    assert n_sub % ATT_UNROLL == 0
    hd = ATT_HEAD_DIM
    ones_rows = jnp.ones((ATT_VT_ROWS - hd, ATT_TK), BF16)

    lane = lax.broadcasted_iota(jnp.int32, (tq, LANES), 1)
    q = q_ref[...]
    fq = fq_ref[...]
    q_aug = []
    for i in range(HEADS_PER_BLOCK):
        first = BIAS_LANES_PER_HEAD * (HEADS_PER_BLOCK * pair + i)
        own_q = (lane < hd) if i == 0 else (lane >= hd)
        own_f = jnp.logical_and(lane >= first, lane < first + BIAS_LANES_PER_HEAD)
        q_aug.append(jnp.concatenate([jnp.where(own_q, q, jnp.zeros_like(q)),
                                      jnp.where(own_f, fq, jnp.zeros_like(fq))], axis=1))
    causal = (lax.broadcasted_iota(jnp.int32, (ATT_TK, ATT_SUB), 0)
              <= lax.broadcasted_iota(jnp.int32, (ATT_TK, ATT_SUB), 1))
    chains = [(i, s) for i in range(HEADS_PER_BLOCK) for s in range(n_sub)]

    def head_scores(i, block, s_from=0):
        rows = pl.ds(pl.multiple_of(block * ATT_TK, ATT_TK), ATT_TK)
        kb = jnp.concatenate([k_ref[rows, :], fk_ref[rows, :]], axis=1)
        qa = q_aug[i][s_from * ATT_SUB:, :]
        return lax.dot_general(kb, qa, (((1,), (1,)), ((), ())), preferred_element_type=F32)

    def update(c, block, st, st_max):
        m = m_ref[c]
        m_new = jnp.maximum(m, st_max)
        pt = jnp.exp2(st - m_new).astype(BF16)
        start = pl.multiple_of(block * ATT_TK, ATT_TK)
        i = chains[c][0]
        vt = jnp.concatenate([vt_ref[i * hd:(i + 1) * hd, pl.ds(start, ATT_TK)], ones_rows], axis=0)
        acc_ref[c] = (jnp.exp2(m - m_new) * acc_ref[c]
                      + jnp.dot(vt, pt, preferred_element_type=F32))
        m_ref[c] = m_new

    def stash(slot, block):
        for i in range(HEADS_PER_BLOCK):
            st = head_scores(i, block)
            for s in range(n_sub):
                part = st[:, s * ATT_SUB:(s + 1) * ATT_SUB]
                st_ref[slot, i * n_sub + s] = part
                mx_ref[slot, i * n_sub + s] = jnp.max(part, axis=0, keepdims=True)

    every = list(range(len(chains)))
    for c in every:
        m_ref[c] = jnp.full((1, ATT_SUB), -1e30, F32)
        acc_ref[c] = jnp.zeros((ATT_VT_ROWS, ATT_SUB), F32)

    first_diag = qi * n_sub
    diag = []
    for e in range(n_sub):
        for i in range(HEADS_PER_BLOCK):
            st = head_scores(i, first_diag + e, s_from=e)
            for s in range(e, n_sub):
                part = st[:, (s - e) * ATT_SUB:(s - e + 1) * ATT_SUB]
                if s == e:
                    part = jnp.where(causal, part, -1e30)
                diag.append((i * n_sub + s, first_diag + e, part))
    stash(0, 0)
    for c, block, part in diag:
        update(c, block, part, jnp.max(part, axis=0, keepdims=True))

    def half_step(block, slot, prefetch=True):
        if prefetch:
            stash(1 - slot, block + 1)
        for c in every:
            update(c, block, st_ref[slot, c], mx_ref[slot, c])

    def body(j, carry):
        for t in range(ATT_UNROLL):
            half_step(ATT_UNROLL * j + t, t % 2)
        return carry

    lax.fori_loop(0, qi * (n_sub // ATT_UNROLL) - 1, body, 0)

    @pl.when(qi > 0)
    def _():
        for t in range(ATT_UNROLL):
            half_step(first_diag - ATT_UNROLL + t, t % 2, prefetch=t < ATT_UNROLL - 1)

    heads = [[None] * n_sub for _ in range(HEADS_PER_BLOCK)]
    for c, (i, s) in enumerate(chains):
        acc = acc_ref[c]
        heads[i][s] = acc[0:hd, :] * (1.0 / acc[hd:hd + 1, :])
    o_t = jnp.concatenate([jnp.concatenate(h, axis=1) for h in heads], axis=0)
    o_ref[...] = o_t.astype(BF16)


def _attn(proj, v_t, feat_q, feat_k, batch, seq, q_col, k_col, tq=1024):
    n_tok = proj.shape[0]
    nq = seq // tq
    n_pairs = ATT_HEADS // HEADS_PER_BLOCK
    n_chains = HEADS_PER_BLOCK * (tq // ATT_SUB)
    return pl.pallas_call(
        functools.partial(_attn_kernel, tq=tq),
        grid=(batch, n_pairs, nq),
        in_specs=[pl.BlockSpec((tq, LANES), lambda b, p, i: (b * nq + i, q_col + p)),
                  pl.BlockSpec((seq, LANES), lambda b, p, i: (b, k_col + p)),
                  pl.BlockSpec((None, LANES, seq), lambda b, p, i: (b, p, 0)),
                  pl.BlockSpec((tq, LANES), lambda b, p, i: (b * nq + i, 0)),
                  pl.BlockSpec((seq, LANES), lambda b, p, i: (b, 0))],
        out_specs=pl.BlockSpec((None, LANES, tq), lambda b, p, i: (b, p, i)),
        out_shape=jax.ShapeDtypeStruct((batch, ATT_HEADS * ATT_HEAD_DIM, seq), BF16),
        scratch_shapes=[pltpu.VMEM((2, n_chains, ATT_TK, ATT_SUB), F32),
                        pltpu.VMEM((2, n_chains, 1, ATT_SUB), F32),
                        pltpu.VMEM((n_chains, ATT_VT_ROWS, ATT_SUB), F32),
                        pltpu.VMEM((n_chains, 1, ATT_SUB), F32)],
        compiler_params=pltpu.CompilerParams(
            dimension_semantics=("arbitrary", "arbitrary", "arbitrary"),
            vmem_limit_bytes=VMEM_LIMIT_BYTES),
        name="attn",
    )(proj, proj, v_t, feat_q, feat_k)


def _tail_kernel(x_ref, ys_ref, ya_ref, mod_ref, wo_ref, anw_ref, g1_ref, b1_ref, w1_ref, w2_ref,
                 g2_ref, b2_ref, o_ref, *, alpha, ff_chunk):
    ssm_w = ys_ref.shape[1]
    ya = ya_ref[...].astype(F32).T
    ya = ya * lax.rsqrt(jnp.mean(ya * ya, axis=-1, keepdims=True) + RMS_EPS) * anw_ref[...]
    y = (jnp.dot(ys_ref[...], wo_ref[0:ssm_w, :], preferred_element_type=F32)
         + jnp.dot(ya.astype(BF16), wo_ref[ssm_w:, :], preferred_element_type=F32))
    x1 = _layer_norm(alpha * x_ref[...] + (1.0 + mod_ref[2]) * y, g1_ref[...], b1_ref[...])
    h = (x1 * (1.0 + mod_ref[4]) + mod_ref[3]).astype(BF16)
    ff = jnp.zeros(x1.shape, F32)
    for c in range(w1_ref.shape[1] // ff_chunk):
        a = jnp.dot(h, w1_ref[:, c * ff_chunk:(c + 1) * ff_chunk], preferred_element_type=F32)
        a = jnp.maximum(a, 0.0)
        ff = ff + jnp.dot((a * a).astype(BF16), w2_ref[c * ff_chunk:(c + 1) * ff_chunk, :],
                          preferred_element_type=F32)
    o_ref[...] = _layer_norm(alpha * x1 + (1.0 + mod_ref[5]) * ff, g2_ref[...], b2_ref[...])


def _tail(x2, y_ssm, y_att_t, mod4, w_out, anw, g1, b1, w1, w2, g2, b2, seq, alpha, tm=512,
          ff_chunk=1024):
    n_tok, d = x2.shape
    mix = w_out.shape[0]
    d_ff = w1.shape[1]
    tiles_per_seq = seq // tm
    tok = lambda i: (i, 0)
    const = lambda i: (0, 0)
    resident = functools.partial(pl.BlockSpec, index_map=const, pipeline_mode=pl.Buffered(1))
    return pl.pallas_call(
        functools.partial(_tail_kernel, alpha=alpha, ff_chunk=ff_chunk),
        grid=(n_tok // tm,),
        in_specs=[pl.BlockSpec((tm, d), tok),
                  pl.BlockSpec((tm, y_ssm.shape[1]), tok),
                  pl.BlockSpec((None, y_att_t.shape[1], tm),
                               lambda i: (i // tiles_per_seq, 0, i % tiles_per_seq)),
                  pl.BlockSpec((None, 6, 1, d), lambda i: (i // tiles_per_seq, 0, 0, 0)),
                  resident((mix, d)),
                  pl.BlockSpec((1, y_att_t.shape[1]), const),
                  pl.BlockSpec((1, d), const),
                  pl.BlockSpec((1, d), const),
                  resident((d, d_ff)),
                  resident((d_ff, d)),
                  pl.BlockSpec((1, d), const),
                  pl.BlockSpec((1, d), const)],
        out_specs=pl.BlockSpec((tm, d), tok),
        out_shape=jax.ShapeDtypeStruct((n_tok, d), F32),
        compiler_params=pltpu.CompilerParams(
            dimension_semantics=("arbitrary",), vmem_limit_bytes=VMEM_LIMIT_BYTES),
        name="tail",
    )(x2, y_ssm, y_att_t, mod4, w_out, anw, g1, b1, w1, w2, g2, b2)


def _pad_lanes(v):
    return jnp.pad(v, ((0, 0), (0, LANES - v.shape[1])))


def kernel(x, c, w_ada, b_ada, w_in, conv_w, conv_b, dt_bias, a_log, d_skip, ssm_norm_w, f_bias,
           attn_norm_w, w_out, ln1_g, ln1_b, w_ff_in, w_ff_out, ln2_g, ln2_b):
    batch, seq, d = x.shape
    depth = w_ada.shape[0]
    alpha = (2.0 * depth) ** 0.25
    ssm_w = SSM_HEADS * SSM_HEAD_DIM
    att_w = ATT_HEADS * ATT_HEAD_DIM
    conv_dim = ssm_w + 2 * SSM_GROUPS * SSM_STATE
    o_xbc = ssm_w
    o_dt = o_xbc + conv_dim
    o_q = o_dt + SSM_HEADS
    o_k = o_q + att_w
    o_v = o_k + att_w
    o_f = o_v + att_w
    q_col = (ssm_w + conv_dim) // LANES
    k_col = q_col + att_w // LANES
    v_col = k_col + att_w // LANES

    x2 = x.reshape(batch * seq, d)
    c_pad = jnp.pad(c, ((0, SUBLANES - batch % SUBLANES if batch % SUBLANES else 0), (0, 0)))
    for l in range(depth):
        mod = _ada(c_pad, w_ada[l], b_ada[l][None, :])
        mod4 = mod[:batch].reshape(batch, 6, 1, d)
        wt = jnp.swapaxes(w_in[l], 0, 1).astype(BF16)
        wt_main = jnp.concatenate([wt[:o_dt], wt[o_q:o_f]], axis=0)
        wt_small = jnp.concatenate([wt[o_dt:o_q], wt[o_f:]], axis=0)
        wt_small = jnp.pad(wt_small, ((0, LANES - wt_small.shape[0]), (0, 0)))
        proj, small, v_t = _inproj(x2, mod4, wt_main, wt_small, batch, seq, ssm_w, q_col * LANES,
                                   k_col * LANES, v_col * LANES, ATT_HEAD_DIM ** -0.5 * LOG2E)

        brow = _pad_lanes(jnp.concatenate([dt_bias[l], f_bias[l]])[None, :])
        alog_row = _pad_lanes(a_log[l][None, :])
        dsk_full = jnp.repeat(d_skip[l], SSM_HEAD_DIM)[None, :]
        y_ssm, feat_q, feat_k = _ssd(proj, small, conv_w[l], conv_b[l][None, :], brow, alog_row,
                                     dsk_full, ssm_norm_w[l][None, :], batch, seq)
        y_att_t = _attn(proj, v_t, feat_q, feat_k, batch, seq, q_col, k_col)

        x2 = _tail(x2, y_ssm, y_att_t, mod4, w_out[l].astype(BF16), attn_norm_w[l][None, :],
                   ln1_g[l][None, :], ln1_b[l][None, :], w_ff_in[l].astype(BF16),
                   w_ff_out[l].astype(BF16), ln2_g[l][None, :], ln2_b[l][None, :], seq, alpha)
    return x2.reshape(batch, seq, d)
```

```python
import functools

import jax
import jax.numpy as jnp
import numpy as np
from jax import lax
from jax.experimental import pallas as pl
from jax.experimental.pallas import tpu as pltpu

F32 = jnp.float32
BF16 = jnp.bfloat16
HIGHEST = lax.Precision.HIGHEST

LANES = 128
SUBLANES = 8
BF16_SUBLANES = 16
VMEM_LIMIT_BYTES = 56 * 1024 * 1024

SSM_HEADS = 16
SSM_HEAD_DIM = 64
SSM_GROUPS = 2
SSM_STATE = 128
CONV_WIDTH = 4
CHUNK = 128
ATT_HEADS = 16
ATT_HEAD_DIM = 64
LN_EPS = 1e-5
RMS_EPS = 1e-5

HEADS_PER_GROUP = SSM_HEADS // SSM_GROUPS
HEADS_PER_BLOCK = LANES // ATT_HEAD_DIM
SMALL_DT = 0
SMALL_F = SSM_HEADS
LOG2E = 1.4426950408889634
BIAS_PIECES = 3
BIAS_LANES_PER_HEAD = 2 * BIAS_PIECES


def _silu(v):
    return v * (1.0 / (1.0 + jnp.exp(-v)))


def _softplus(v):
    return jnp.maximum(v, 0.0) + jnp.log(1.0 + jnp.exp(-jnp.abs(v)))


def _split3(c):
    hi = c.astype(BF16).astype(F32)
    r = c - hi
    mid = r.astype(BF16).astype(F32)
    return hi, mid, r - mid


def _pieces(c):
    return jnp.concatenate(_split3(c), axis=1).astype(BF16)


def _layer_norm(v, g, b):
    mu = jnp.mean(v, axis=-1, keepdims=True)
    d = v - mu
    var = jnp.mean(d * d, axis=-1, keepdims=True)
    return d * lax.rsqrt(var + LN_EPS) * g + b


def _ada_kernel(c_ref, w_ref, b_ref, o_ref):
    ca = _silu(c_ref[...]).astype(BF16)
    o_ref[...] = jnp.dot(ca, w_ref[...].astype(BF16), preferred_element_type=F32) + b_ref[...]


def _ada(c_pad, w, b, tn=512):
    rows, d = c_pad.shape
    n = w.shape[1]
    return pl.pallas_call(
        _ada_kernel,
        grid=(n // tn,),
        in_specs=[pl.BlockSpec((rows, d), lambda j: (0, 0)),
                  pl.BlockSpec((d, tn), lambda j: (0, j)),
                  pl.BlockSpec((1, tn), lambda j: (0, j))],
        out_specs=pl.BlockSpec((rows, tn), lambda j: (0, j)),
        out_shape=jax.ShapeDtypeStruct((rows, n), F32),
        name="ada",
    )(c_pad, w, b)


def _inproj_kernel(x_ref, mod_ref, wt_ref, wst_ref, brow_ref, place_ref, ones_ref, o_ref, os_ref,
                   ovt_ref, fq_ref, fk_ref, car_ref, *, tn, tiles_per_seq, z_hi, q_lo, q_hi, q_scale):
    nt = (((1,), (1,)), ((), ()))
    tm = x_ref.shape[0]
    v_lo = o_ref.shape[1]

    @pl.when(pl.program_id(0) == 0)
    def _():
        car_ref[...] = jnp.zeros(car_ref.shape, F32)

    h = (x_ref[...] * (1.0 + mod_ref[1]) + mod_ref[0]).astype(BF16)
    small = lax.dot_general(h, wst_ref[...], nt, preferred_element_type=F32)
    os_ref[...] = small

    lane = lax.broadcasted_iota(jnp.int32, (tm, LANES), 1)
    is_f = jnp.logical_and(lane >= SMALL_F, lane < SMALL_F + ATT_HEADS)
    log_f = jnp.where(is_f, -_softplus(-(small + brow_ref[...])), 0.0)
    tril = (lax.broadcasted_iota(jnp.int32, (tm, tm), 0)
            >= lax.broadcasted_iota(jnp.int32, (tm, tm), 1)).astype(BF16)
    cs3 = jnp.dot(tril, _pieces(log_f), preferred_element_type=F32)
    seq_start = pl.program_id(0) % tiles_per_seq == 0
    cum = (cs3[:, 0:LANES] + cs3[:, LANES:2 * LANES] + cs3[:, 2 * LANES:]
           + jnp.where(seq_start, 0.0, car_ref[...]))
    feats = jnp.dot(_pieces(cum * LOG2E), place_ref[...], preferred_element_type=F32) + ones_ref[...]
    fq_ref[...] = feats[:, 0:LANES].astype(BF16)
    fk_ref[...] = feats[:, LANES:].astype(BF16)
    car_ref[...] = cum[tm - 1:tm, :]

    for c0 in range(0, wt_ref.shape[0], tn):
        acc = lax.dot_general(h, wt_ref[c0:c0 + tn, :], nt, preferred_element_type=F32)
        if c0 >= v_lo:
            ovt_ref[c0 - v_lo:c0 - v_lo + tn, :] = acc.T.astype(BF16)
            continue
        if c0 < z_hi:
            acc = _silu(acc)
        elif q_lo <= c0 < q_hi:
            acc = acc * q_scale
        o_ref[:, c0:c0 + tn] = acc.astype(BF16)


def _inproj(x2, mod4, wt_main, wt_small, brow, batch, seq, z_hi, q_lo, q_hi, v_lo, q_scale, tm=512,
            tn=512):
    n_tok, d = x2.shape
    n_main = wt_main.shape[0]
    assert all(c % tn == 0 for c in (z_hi, q_lo, q_hi, v_lo, n_main))
    tiles_per_seq = seq // tm
    place, ones_row = _bias_routing()
    const = lambda i: (0, 0)
    tok = lambda i: (i, 0)
    resident = functools.partial(pl.BlockSpec, index_map=const, pipeline_mode=pl.Buffered(1))
    return pl.pallas_call(
        functools.partial(_inproj_kernel, tn=tn, tiles_per_seq=tiles_per_seq, z_hi=z_hi, q_lo=q_lo,
                          q_hi=q_hi, q_scale=q_scale),
        grid=(n_tok // tm,),
        in_specs=[pl.BlockSpec((tm, d), tok),
                  pl.BlockSpec((None, 6, 1, d), lambda i: (i // tiles_per_seq, 0, 0, 0)),
                  resident((n_main, d)),
                  resident((LANES, d)),
                  pl.BlockSpec((1, LANES), const),
                  pl.BlockSpec(place.shape, const),
                  pl.BlockSpec(ones_row.shape, const)],
        out_specs=[pl.BlockSpec((tm, v_lo), tok),
                   pl.BlockSpec((tm, LANES), tok),
                   pl.BlockSpec((None, n_main - v_lo, tm),
                                lambda i: (i // tiles_per_seq, 0, i % tiles_per_seq)),
                   pl.BlockSpec((tm, LANES), tok),
                   pl.BlockSpec((tm, LANES), tok)],
        out_shape=[jax.ShapeDtypeStruct((n_tok, v_lo), BF16),
                   jax.ShapeDtypeStruct((n_tok, LANES), F32),
                   jax.ShapeDtypeStruct((batch, n_main - v_lo, seq), BF16),
                   jax.ShapeDtypeStruct((n_tok, LANES), BF16),
                   jax.ShapeDtypeStruct((n_tok, LANES), BF16)],
        scratch_shapes=[pltpu.VMEM((1, LANES), F32)],
        compiler_params=pltpu.CompilerParams(
            dimension_semantics=("arbitrary",), vmem_limit_bytes=VMEM_LIMIT_BYTES),
        name="inproj",
    )(x2, mod4, wt_main, wt_small, brow, place, ones_row)


def _ssd_rows(gate, conv, small, seq_start, brow_ref, alog_ref, dsk_ref, nw_ref, e_ref, st_ref):
    width = SSM_HEADS * SSM_HEAD_DIM
    gw = HEADS_PER_GROUP * SSM_HEAD_DIM
    u = _silu(conv)
    xs = u[:, 0:width]
    b_all = u[:, width:width + SSM_GROUPS * SSM_STATE]
    c_all = u[:, width + SSM_GROUPS * SSM_STATE:]

    lane = lax.broadcasted_iota(jnp.int32, (CHUNK, LANES), 1)
    row = lax.broadcasted_iota(jnp.int32, (CHUNK, LANES), 0)
    is_dt = lane < SMALL_F
    sp = _softplus(small + brow_ref[...])
    a_row = -jnp.exp(alog_ref[...])
    dt_tile = jnp.where(is_dt, sp, 0.0)
    val = jnp.where(is_dt, sp * a_row, 0.0)
    tril = row >= lane
    cs3 = jnp.dot(tril.astype(BF16), _pieces(val), preferred_element_type=F32)
    yield
    cs = cs3[:, 0:LANES] + cs3[:, LANES:2 * LANES] + cs3[:, 2 * LANES:]

    cs_t = cs.T
    dt_t = dt_tile.T
    full = jnp.dot(_pieces(jnp.concatenate([cs, dt_tile], axis=0)), e_ref[...],
                   preferred_element_type=F32)
    acs_full = full[0:CHUNK]
    dt_full = full[CHUNK:]
    atot_full = acs_full[CHUNK - 1:CHUNK, :]
    dec_out = jnp.exp(acs_full)
    dec_end = jnp.exp(atot_full - acs_full)
    chunk_dec = jnp.exp(atot_full)
    xs_b = xs.astype(BF16)
    xw = (xs * (dec_end * dt_full)).astype(BF16)
    lane_lo = lane < SSM_HEAD_DIM

    bgs = [b_all[:, g * SSM_STATE:(g + 1) * SSM_STATE] for g in range(SSM_GROUPS)]
    cgs = [c_all[:, g * SSM_STATE:(g + 1) * SSM_STATE].astype(BF16) for g in range(SSM_GROUPS)]
    cbms = [lax.dot_general(cg, bg.astype(BF16), (((1,), (1,)), ((), ())),
                            preferred_element_type=F32) for cg, bg in zip(cgs, bgs)]
    yield
    ydiag_blocks = []
    yoff_blocks = []
    for g in range(SSM_GROUPS):
        bg, cg, cbm = bgs[g], cgs[g], cbms[g]
        for j in range(HEADS_PER_GROUP // HEADS_PER_BLOCK):
            h0 = g * HEADS_PER_GROUP + HEADS_PER_BLOCK * j
            xp = xs_b[:, h0 * SSM_HEAD_DIM:h0 * SSM_HEAD_DIM + LANES]
            res = []
            for h in range(h0, h0 + HEADS_PER_BLOCK):
                seg = cs[:, h:h + 1] - cs_t[h:h + 1, :]
                dec = jnp.exp(jnp.where(tril, seg, -1e30))
                sc = (cbm * dec * dt_t[h:h + 1, :]).astype(BF16)
                res.append(jnp.dot(sc, xp, preferred_element_type=F32))
            ydiag_blocks.append(jnp.where(lane_lo, res[0], res[1]))
        st = st_ref[g]
        if seq_start is not None:
            st = jnp.where(seq_start, 0.0, st)
        yoff_blocks.append(jnp.dot(cg, st.astype(BF16), preferred_element_type=F32)
                           * dec_out[:, g * gw:(g + 1) * gw])
        st_ref[g] = st * chunk_dec[:, g * gw:(g + 1) * gw] + jnp.dot(
            bg.T.astype(BF16), xw[:, g * gw:(g + 1) * gw], preferred_element_type=F32)
    y = (jnp.concatenate(ydiag_blocks, axis=1) + jnp.concatenate(yoff_blocks, axis=1)
         + dsk_ref[...] * xs)
    gated = y * gate.astype(F32)
    ms = jnp.mean(gated * gated, axis=-1, keepdims=True)
    yield (gated * lax.rsqrt(ms + RMS_EPS) * nw_ref[...]).astype(BF16)


def _bias_routing():
    place = np.zeros((BIAS_PIECES * LANES, 2 * LANES), np.float32)
    ones_row = np.zeros((1, 2 * LANES), np.float32)
    for h in range(ATT_HEADS):
        base = BIAS_LANES_PER_HEAD * h
        for t in range(BIAS_PIECES):
            place[t * LANES + SMALL_F + h, base + t] = 1.0
            place[t * LANES + SMALL_F + h, LANES + base + BIAS_PIECES + t] = -1.0
            ones_row[0, base + BIAS_PIECES + t] = 1.0
            ones_row[0, LANES + base + t] = 1.0
    return jnp.asarray(place, BF16), jnp.asarray(ones_row, F32)


def _head_expansion():
    expand = np.zeros((BIAS_PIECES * LANES, SSM_HEADS * SSM_HEAD_DIM), np.float32)
    for t in range(BIAS_PIECES):
        for h in range(SSM_HEADS):
            expand[t * LANES + SMALL_DT + h, h * SSM_HEAD_DIM:(h + 1) * SSM_HEAD_DIM] = 1.0
    return jnp.asarray(expand, BF16)


ATT_TK = 256
ATT_SUB = 256
ATT_UNROLL = 4
ATT_VT_ROWS = ATT_HEAD_DIM + BF16_SUBLANES


def _attn_kernel(q_ref, k_ref, vt_ref, fq_ref, fk_ref, o_ref, st_ref, mx_ref, acc_ref, m_ref, *, tq):
    pair = pl.program_id(1)
    qi = pl.program_id(2)
    n_sub = tq // ATT_SUB
    assert n_sub % ATT_UNROLL == 0
    hd = ATT_HEAD_DIM
    ones_rows = jnp.ones((ATT_VT_ROWS - hd, ATT_TK), BF16)

    lane = lax.broadcasted_iota(jnp.int32, (tq, LANES), 1)
    q = q_ref[...]
    fq = fq_ref[...]
    q_aug = []
    for i in range(HEADS_PER_BLOCK):
        first = BIAS_LANES_PER_HEAD * (HEADS_PER_BLOCK * pair + i)
        own_q = (lane < hd) if i == 0 else (lane >= hd)
        own_f = jnp.logical_and(lane >= first, lane < first + BIAS_LANES_PER_HEAD)
        q_aug.append(jnp.concatenate([jnp.where(own_q, q, jnp.zeros_like(q)),
                                      jnp.where(own_f, fq, jnp.zeros_like(fq))], axis=1))
    causal = (lax.broadcasted_iota(jnp.int32, (ATT_TK, ATT_SUB), 0)
              <= lax.broadcasted_iota(jnp.int32, (ATT_TK, ATT_SUB), 1))
    chains = [(i, s) for i in range(HEADS_PER_BLOCK) for s in range(n_sub)]

    def head_scores(i, block, s_from=0):
        rows = pl.ds(pl.multiple_of(block * ATT_TK, ATT_TK), ATT_TK)
        kb = jnp.concatenate([k_ref[rows, :], fk_ref[rows, :]], axis=1)
        qa = q_aug[i][s_from * ATT_SUB:, :]
        return lax.dot_general(kb, qa, (((1,), (1,)), ((), ())), preferred_element_type=F32)

    def update(c, block, st, st_max):
        m = m_ref[c]
        m_new = jnp.maximum(m, st_max)
        pt = jnp.exp2(st - m_new).astype(BF16)
        start = pl.multiple_of(block * ATT_TK, ATT_TK)
        i = chains[c][0]
        vt = jnp.concatenate([vt_ref[i * hd:(i + 1) * hd, pl.ds(start, ATT_TK)], ones_rows], axis=0)
        acc_ref[c] = (jnp.exp2(m - m_new) * acc_ref[c]
                      + jnp.dot(vt, pt, preferred_element_type=F32))
        m_ref[c] = m_new

    def stash(slot, block):
        for i in range(HEADS_PER_BLOCK):
            st = head_scores(i, block)
            for s in range(n_sub):
                part = st[:, s * ATT_SUB:(s + 1) * ATT_SUB]
                st_ref[slot, i * n_sub + s] = part
                mx_ref[slot, i * n_sub + s] = jnp.max(part, axis=0, keepdims=True)

    every = list(range(len(chains)))
    for c in every:
        m_ref[c] = jnp.full((1, ATT_SUB), -1e30, F32)
        acc_ref[c] = jnp.zeros((ATT_VT_ROWS, ATT_SUB), F32)

    first_diag = qi * n_sub
    diag = []
    for e in range(n_sub):
        for i in range(HEADS_PER_BLOCK):
            st = head_scores(i, first_diag + e, s_from=e)
            for s in range(e, n_sub):
                part = st[:, (s - e) * ATT_SUB:(s - e + 1) * ATT_SUB]
                if s == e:
                    part = jnp.where(causal, part, -1e30)
                diag.append((i * n_sub + s, first_diag + e, part))
    stash(0, 0)
    for c, block, part in diag:
        update(c, block, part, jnp.max(part, axis=0, keepdims=True))

    def half_step(block, slot, prefetch=True):
        if prefetch:
            stash(1 - slot, block + 1)
        for c in every:
            update(c, block, st_ref[slot, c], mx_ref[slot, c])

    def body(j, carry):
        for t in range(ATT_UNROLL):
            half_step(ATT_UNROLL * j + t, t % 2)
        return carry

    lax.fori_loop(0, qi * (n_sub // ATT_UNROLL) - 1, body, 0)

    @pl.when(qi > 0)
    def _():
        for t in range(ATT_UNROLL):
            half_step(first_diag - ATT_UNROLL + t, t % 2, prefetch=t < ATT_UNROLL - 1)

    heads = [[None] * n_sub for _ in range(HEADS_PER_BLOCK)]
    for c, (i, s) in enumerate(chains):
        acc = acc_ref[c]
        heads[i][s] = acc[0:hd, :] * (1.0 / acc[hd:hd + 1, :])
    o_t = jnp.concatenate([jnp.concatenate(h, axis=1) for h in heads], axis=0)
    o_ref[...] = o_t.astype(BF16)


def _attn(proj, v_t, feat_q, feat_k, batch, seq, q_col, k_col, tq=1024):
    n_tok = proj.shape[0]
    nq = seq // tq
    n_pairs = ATT_HEADS // HEADS_PER_BLOCK
    n_chains = HEADS_PER_BLOCK * (tq // ATT_SUB)
    return pl.pallas_call(
        functools.partial(_attn_kernel, tq=tq),
        grid=(batch, n_pairs, nq),
        in_specs=[pl.BlockSpec((tq, LANES), lambda b, p, i: (b * nq + i, q_col + p)),
                  pl.BlockSpec((seq, LANES), lambda b, p, i: (b, k_col + p)),
                  pl.BlockSpec((None, LANES, seq), lambda b, p, i: (b, p, 0)),
                  pl.BlockSpec((tq, LANES), lambda b, p, i: (b * nq + i, 0)),
                  pl.BlockSpec((seq, LANES), lambda b, p, i: (b, 0))],
        out_specs=pl.BlockSpec((None, LANES, tq), lambda b, p, i: (b, p, i)),
        out_shape=jax.ShapeDtypeStruct((batch, ATT_HEADS * ATT_HEAD_DIM, seq), BF16),
        scratch_shapes=[pltpu.VMEM((2, n_chains, ATT_TK, ATT_SUB), F32),
                        pltpu.VMEM((2, n_chains, 1, ATT_SUB), F32),
                        pltpu.VMEM((n_chains, ATT_VT_ROWS, ATT_SUB), F32),
                        pltpu.VMEM((n_chains, 1, ATT_SUB), F32)],
        compiler_params=pltpu.CompilerParams(
            dimension_semantics=("arbitrary", "arbitrary", "arbitrary"),
            vmem_limit_bytes=VMEM_LIMIT_BYTES),
        name="attn",
    )(proj, proj, v_t, feat_q, feat_k)


def _mix_tail_kernel(x_ref, ya_ref, mod_ref, wo_ref, anw_ref, g1_ref, b1_ref, w1_ref, w2_ref, g2_ref,
                     b2_ref, gate_ref, xs_ref, bc_ref, sm_ref, xh_ref, bch_ref, cw_ref, cb_ref,
                     brow_ref, alog_ref, dsk_ref, nw_ref, e_ref, o_ref, ys_ref, st_ref, xf_ref, *,
                     alpha, ff_chunk, tiles_per_seq, n_tiles):
    i = pl.program_id(0)
    tm = x_ref.shape[0]
    slot = i % 2

    @pl.when(i == 0)
    def _():
        ys_ref[...] = jnp.zeros(ys_ref.shape, BF16)
        st_ref[...] = jnp.zeros(st_ref.shape, F32)

    ssm_w = ys_ref.shape[2]
    ya = ya_ref[...].astype(F32).T
    ya = ya * lax.rsqrt(jnp.mean(ya * ya, axis=-1, keepdims=True) + RMS_EPS) * anw_ref[...]
    y = (jnp.dot(ys_ref[1 - slot], wo_ref[0:ssm_w, :], preferred_element_type=F32)
         + jnp.dot(ya.astype(BF16), wo_ref[ssm_w:, :], preferred_element_type=F32))
    x1 = _layer_norm(alpha * x_ref[...] + (1.0 + mod_ref[2]) * y, g1_ref[...], b1_ref[...])
    h = (x1 * (1.0 + mod_ref[4]) + mod_ref[3]).astype(BF16)

    seq_start = jnp.minimum(i, n_tiles - 1) % tiles_per_seq == 0
    n_chunks = tm // CHUNK
    n_ff = w1_ref.shape[1] // ff_chunk
    assert n_chunks == n_ff
    width = xs_ref.shape[1]
    halo = xh_ref.shape[0]
    before = jnp.concatenate([xh_ref[...], bch_ref[...]], axis=1).astype(F32)
    xf_ref[0:halo, :] = jnp.where(seq_start, 0.0, before)
    xf_ref[halo:, 0:width] = xs_ref[...].astype(F32)
    xf_ref[halo:, width:] = bc_ref[...].astype(F32)
    ff = jnp.zeros(x1.shape, F32)
    for c in range(n_chunks):
        rows = slice(c * CHUNK, (c + 1) * CHUNK)
        conv = cb_ref[...]
        for k in range(CONV_WIDTH):
            first = halo + c * CHUNK - k
            conv = conv + (cw_ref[CONV_WIDTH - 1 - k:CONV_WIDTH - k, :]
                           * xf_ref[first:first + CHUNK, :])
        ssd = _ssd_rows(gate_ref[rows, :], conv, sm_ref[rows, :], seq_start if c == 0 else None,
                        brow_ref, alog_ref, dsk_ref, nw_ref, e_ref, st_ref)
        next(ssd)
        a = jnp.dot(h, w1_ref[:, c * ff_chunk:(c + 1) * ff_chunk], preferred_element_type=F32)
        a = jnp.maximum(a, 0.0)
        next(ssd)
        ff = ff + jnp.dot((a * a).astype(BF16), w2_ref[c * ff_chunk:(c + 1) * ff_chunk, :],
                          preferred_element_type=F32)
        ys_ref[slot, rows, :] = next(ssd)
    o_ref[...] = _layer_norm(alpha * x1 + (1.0 + mod_ref[5]) * ff, g2_ref[...], b2_ref[...])


def _mix_tail(x2, proj, small, y_att_t, mod4, w_out, anw, g1, b1, w1, w2, g2, b2, conv_w, conv_b,
              brow, alog_row, dsk_full, ssm_norm_w, seq, alpha, tm=512, ff_chunk=1024):
    n_tok, d = x2.shape
    mix = w_out.shape[0]
    d_ff = w1.shape[1]
    width = SSM_HEADS * SSM_HEAD_DIM
    bc_w = 2 * SSM_GROUPS * SSM_STATE
    x_col, bc_col = 1, 2 * width // bc_w
    tiles_per_seq = seq // tm
    n_tiles = n_tok // tm
    expand = _head_expansion()
    const = lambda i: (0, 0)
    resident = functools.partial(pl.BlockSpec, index_map=const, pipeline_mode=pl.Buffered(1))
    mlp_tile = lambda i: jnp.maximum(i - 1, 0)
    ssd_tile = lambda i: jnp.minimum(i, n_tiles - 1)
    halo = lambda i: jnp.maximum(ssd_tile(i) * (tm // BF16_SUBLANES) - 1, 0)
    return pl.pallas_call(
        functools.partial(_mix_tail_kernel, alpha=alpha, ff_chunk=ff_chunk,
                          tiles_per_seq=tiles_per_seq, n_tiles=n_tiles),
        grid=(n_tiles + 1,),
        in_specs=[pl.BlockSpec((tm, d), lambda i: (mlp_tile(i), 0)),
                  pl.BlockSpec((None, y_att_t.shape[1], tm),
                               lambda i: (mlp_tile(i) // tiles_per_seq, 0,
                                          mlp_tile(i) % tiles_per_seq)),
                  pl.BlockSpec((None, 6, 1, d), lambda i: (mlp_tile(i) // tiles_per_seq, 0, 0, 0)),
                  resident((mix, d)),
                  pl.BlockSpec((1, y_att_t.shape[1]), const),
                  pl.BlockSpec((1, d), const),
                  pl.BlockSpec((1, d), const),
                  resident((d, d_ff)),
                  resident((d_ff, d)),
                  pl.BlockSpec((1, d), const),
                  pl.BlockSpec((1, d), const),
                  pl.BlockSpec((tm, width), lambda i: (ssd_tile(i), 0)),
                  pl.BlockSpec((tm, width), lambda i: (ssd_tile(i), x_col)),
                  pl.BlockSpec((tm, bc_w), lambda i: (ssd_tile(i), bc_col)),
                  pl.BlockSpec((tm, LANES), lambda i: (ssd_tile(i), 0)),
                  pl.BlockSpec((BF16_SUBLANES, width), lambda i: (halo(i), x_col)),
                  pl.BlockSpec((BF16_SUBLANES, bc_w), lambda i: (halo(i), bc_col)),
                  pl.BlockSpec(conv_w.shape, const),
                  pl.BlockSpec(conv_b.shape, const),
                  pl.BlockSpec((1, LANES), const),
                  pl.BlockSpec((1, LANES), const),
                  pl.BlockSpec((1, width), const),
                  pl.BlockSpec((1, width), const),
                  pl.BlockSpec(expand.shape, const)],
        out_specs=pl.BlockSpec((tm, d), lambda i: (mlp_tile(i), 0)),
        out_shape=jax.ShapeDtypeStruct((n_tok, d), F32),
        scratch_shapes=[pltpu.VMEM((2, tm, width), BF16),
                        pltpu.VMEM((SSM_GROUPS, SSM_STATE, HEADS_PER_GROUP * SSM_HEAD_DIM), F32),
                        pltpu.VMEM((BF16_SUBLANES + tm, width + bc_w), F32)],
        compiler_params=pltpu.CompilerParams(
            dimension_semantics=("arbitrary",), vmem_limit_bytes=VMEM_LIMIT_BYTES),
        name="mix_tail",
    )(x2, y_att_t, mod4, w_out, anw, g1, b1, w1, w2, g2, b2, proj, proj, proj, small, proj, proj,
      conv_w, conv_b, brow, alog_row, dsk_full, ssm_norm_w, expand)


def _pad_lanes(v):
    return jnp.pad(v, ((0, 0), (0, LANES - v.shape[1])))


def kernel(x, c, w_ada, b_ada, w_in, conv_w, conv_b, dt_bias, a_log, d_skip, ssm_norm_w, f_bias,
           attn_norm_w, w_out, ln1_g, ln1_b, w_ff_in, w_ff_out, ln2_g, ln2_b):
    batch, seq, d = x.shape
    depth = w_ada.shape[0]
    alpha = (2.0 * depth) ** 0.25
    ssm_w = SSM_HEADS * SSM_HEAD_DIM
    att_w = ATT_HEADS * ATT_HEAD_DIM
    conv_dim = ssm_w + 2 * SSM_GROUPS * SSM_STATE
    o_xbc = ssm_w
    o_dt = o_xbc + conv_dim
    o_q = o_dt + SSM_HEADS
    o_k = o_q + att_w
    o_v = o_k + att_w
    o_f = o_v + att_w
    q_col = (ssm_w + conv_dim) // LANES
    k_col = q_col + att_w // LANES
    v_col = k_col + att_w // LANES

    x2 = x.reshape(batch * seq, d)
    c_pad = jnp.pad(c, ((0, SUBLANES - batch % SUBLANES if batch % SUBLANES else 0), (0, 0)))
    for l in range(depth):
        mod = _ada(c_pad, w_ada[l], b_ada[l][None, :])
        mod4 = mod[:batch].reshape(batch, 6, 1, d)
        wt = jnp.swapaxes(w_in[l], 0, 1).astype(BF16)
        wt_main = jnp.concatenate([wt[:o_dt], wt[o_q:o_f]], axis=0)
        wt_small = jnp.concatenate([wt[o_dt:o_q], wt[o_f:]], axis=0)
        wt_small = jnp.pad(wt_small, ((0, LANES - wt_small.shape[0]), (0, 0)))
        brow = _pad_lanes(jnp.concatenate([dt_bias[l], f_bias[l]])[None, :])
        proj, small, v_t, feat_q, feat_k = _inproj(
            x2, mod4, wt_main, wt_small, brow, batch, seq, ssm_w, q_col * LANES, k_col * LANES,
            v_col * LANES, ATT_HEAD_DIM ** -0.5 * LOG2E)
        y_att_t = _attn(proj, v_t, feat_q, feat_k, batch, seq, q_col, k_col)

        alog_row = _pad_lanes(a_log[l][None, :])
        dsk_full = jnp.repeat(d_skip[l], SSM_HEAD_DIM)[None, :]
        x2 = _mix_tail(x2, proj, small, y_att_t, mod4, w_out[l].astype(BF16),
                       attn_norm_w[l][None, :], ln1_g[l][None, :], ln1_b[l][None, :],
                       w_ff_in[l].astype(BF16), w_ff_out[l].astype(BF16), ln2_g[l][None, :],
                       ln2_b[l][None, :], conv_w[l], conv_b[l][None, :], brow, alog_row, dsk_full,
                       ssm_norm_w[l][None, :], seq, alpha)
    return x2.reshape(batch, seq, d)
```

```python
import functools

import jax
import jax.numpy as jnp
import numpy as np
from jax import lax
from jax.experimental import pallas as pl
from jax.experimental.pallas import tpu as pltpu

F32 = jnp.float32
BF16 = jnp.bfloat16
HIGHEST = lax.Precision.HIGHEST

LANES = 128
SUBLANES = 8
BF16_SUBLANES = 16
VMEM_LIMIT_BYTES = 56 * 1024 * 1024

SSM_HEADS = 16
SSM_HEAD_DIM = 64
SSM_GROUPS = 2
SSM_STATE = 128
CONV_WIDTH = 4
CHUNK = 128
ATT_HEADS = 16
ATT_HEAD_DIM = 64
LN_EPS = 1e-5
RMS_EPS = 1e-5

HEADS_PER_GROUP = SSM_HEADS // SSM_GROUPS
HEADS_PER_BLOCK = LANES // ATT_HEAD_DIM
SMALL_DT = 0
SMALL_F = SSM_HEADS
LOG2E = 1.4426950408889634
BIAS_PIECES = 3
BIAS_LANES_PER_HEAD = 2 * BIAS_PIECES


def _silu(v):
    return v * (1.0 / (1.0 + jnp.exp(-v)))


def _softplus(v):
    return jnp.maximum(v, 0.0) + jnp.log(1.0 + jnp.exp(-jnp.abs(v)))


def _split3(c):
    hi = c.astype(BF16).astype(F32)
    r = c - hi
    mid = r.astype(BF16).astype(F32)
    return hi, mid, r - mid


def _pieces(c):
    return jnp.concatenate(_split3(c), axis=1).astype(BF16)


def _layer_norm(v, g, b):
    mu = jnp.mean(v, axis=-1, keepdims=True)
    d = v - mu
    var = jnp.mean(d * d, axis=-1, keepdims=True)
    return d * lax.rsqrt(var + LN_EPS) * g + b


def _ada_kernel(c_ref, w_ref, b_ref, o_ref):
    ca = _silu(c_ref[...]).astype(BF16)
    o_ref[...] = jnp.dot(ca, w_ref[...].astype(BF16), preferred_element_type=F32) + b_ref[...]


def _ada(c_pad, w, b, tn=512):
    rows, d = c_pad.shape
    n = w.shape[1]
    return pl.pallas_call(
        _ada_kernel,
        grid=(n // tn,),
        in_specs=[pl.BlockSpec((rows, d), lambda j: (0, 0)),
                  pl.BlockSpec((d, tn), lambda j: (0, j)),
                  pl.BlockSpec((1, tn), lambda j: (0, j))],
        out_specs=pl.BlockSpec((rows, tn), lambda j: (0, j)),
        out_shape=jax.ShapeDtypeStruct((rows, n), F32),
        name="ada",
    )(c_pad, w, b)


def _inproj_kernel(x_ref, mod_ref, wt_ref, wst_ref, brow_ref, place_ref, ones_ref, o_ref, os_ref,
                   ovt_ref, fq_ref, fk_ref, car_ref, *, tn, tiles_per_seq, z_hi, q_lo, q_hi, q_scale):
    nt = (((1,), (1,)), ((), ()))
    tm = x_ref.shape[0]
    v_lo = o_ref.shape[1]

    @pl.when(pl.program_id(0) == 0)
    def _():
        car_ref[...] = jnp.zeros(car_ref.shape, F32)

    h = (x_ref[...] * (1.0 + mod_ref[1]) + mod_ref[0]).astype(BF16)
    small = lax.dot_general(h, wst_ref[...], nt, preferred_element_type=F32)
    os_ref[...] = small

    lane = lax.broadcasted_iota(jnp.int32, (tm, LANES), 1)
    is_f = jnp.logical_and(lane >= SMALL_F, lane < SMALL_F + ATT_HEADS)
    log_f = jnp.where(is_f, -_softplus(-(small + brow_ref[...])), 0.0)
    tril = (lax.broadcasted_iota(jnp.int32, (tm, tm), 0)
            >= lax.broadcasted_iota(jnp.int32, (tm, tm), 1)).astype(BF16)
    cs3 = jnp.dot(tril, _pieces(log_f), preferred_element_type=F32)
    seq_start = pl.program_id(0) % tiles_per_seq == 0
    cum = (cs3[:, 0:LANES] + cs3[:, LANES:2 * LANES] + cs3[:, 2 * LANES:]
           + jnp.where(seq_start, 0.0, car_ref[...]))
    feats = jnp.dot(_pieces(cum * LOG2E), place_ref[...], preferred_element_type=F32) + ones_ref[...]
    fq_ref[...] = feats[:, 0:LANES].astype(BF16)
    fk_ref[...] = feats[:, LANES:].astype(BF16)
    car_ref[...] = cum[tm - 1:tm, :]

    for c0 in range(0, wt_ref.shape[0], tn):
        acc = lax.dot_general(h, wt_ref[c0:c0 + tn, :], nt, preferred_element_type=F32)
        if c0 >= v_lo:
            ovt_ref[c0 - v_lo:c0 - v_lo + tn, :] = acc.T.astype(BF16)
            continue
        if c0 < z_hi:
            acc = _silu(acc)
        elif q_lo <= c0 < q_hi:
            acc = acc * q_scale
        o_ref[:, c0:c0 + tn] = acc.astype(BF16)


def _inproj(x2, mod4, wt_main, wt_small, brow, batch, seq, z_hi, q_lo, q_hi, v_lo, q_scale, tm=512,
            tn=512):
    n_tok, d = x2.shape
    n_main = wt_main.shape[0]
    assert all(c % tn == 0 for c in (z_hi, q_lo, q_hi, v_lo, n_main))
    tiles_per_seq = seq // tm
    place, ones_row = _bias_routing()
    const = lambda i: (0, 0)
    tok = lambda i: (i, 0)
    resident = functools.partial(pl.BlockSpec, index_map=const, pipeline_mode=pl.Buffered(1))
    return pl.pallas_call(
        functools.partial(_inproj_kernel, tn=tn, tiles_per_seq=tiles_per_seq, z_hi=z_hi, q_lo=q_lo,
                          q_hi=q_hi, q_scale=q_scale),
        grid=(n_tok // tm,),
        in_specs=[pl.BlockSpec((tm, d), tok),
                  pl.BlockSpec((None, 6, 1, d), lambda i: (i // tiles_per_seq, 0, 0, 0)),
                  resident((n_main, d)),
                  resident((LANES, d)),
                  pl.BlockSpec((1, LANES), const),
                  pl.BlockSpec(place.shape, const),
                  pl.BlockSpec(ones_row.shape, const)],
        out_specs=[pl.BlockSpec((tm, v_lo), tok),
                   pl.BlockSpec((tm, LANES), tok),
                   pl.BlockSpec((None, n_main - v_lo, tm),
                                lambda i: (i // tiles_per_seq, 0, i % tiles_per_seq)),
                   pl.BlockSpec((tm, LANES), tok),
                   pl.BlockSpec((tm, LANES), tok)],
        out_shape=[jax.ShapeDtypeStruct((n_tok, v_lo), BF16),
                   jax.ShapeDtypeStruct((n_tok, LANES), F32),
                   jax.ShapeDtypeStruct((batch, n_main - v_lo, seq), BF16),
                   jax.ShapeDtypeStruct((n_tok, LANES), BF16),
                   jax.ShapeDtypeStruct((n_tok, LANES), BF16)],
        scratch_shapes=[pltpu.VMEM((1, LANES), F32)],
        compiler_params=pltpu.CompilerParams(
            dimension_semantics=("arbitrary",), vmem_limit_bytes=VMEM_LIMIT_BYTES),
        name="inproj",
    )(x2, mod4, wt_main, wt_small, brow, place, ones_row)


def _ssd_rows(gate, conv_silu, small, seq_start, brow_ref, alog_ref, dsk_ref, nw_ref, st_ref):
    width = SSM_HEADS * SSM_HEAD_DIM
    gw = HEADS_PER_GROUP * SSM_HEAD_DIM
    u = conv_silu()
    xs = u[:, 0:width]
    b_all = u[:, width:width + SSM_GROUPS * SSM_STATE]
    c_all = u[:, width + SSM_GROUPS * SSM_STATE:]

    lane = lax.broadcasted_iota(jnp.int32, (CHUNK, LANES), 1)
    row = lax.broadcasted_iota(jnp.int32, (CHUNK, LANES), 0)
    is_dt = lane < SMALL_F
    sp = _softplus(small + brow_ref[...])
    a_row = -jnp.exp(alog_ref[...])
    dt_tile = jnp.where(is_dt, sp, 0.0)
    val = jnp.where(is_dt, sp * a_row, 0.0)
    tril = row >= lane
    cs3 = jnp.dot(tril.astype(BF16), _pieces(val), preferred_element_type=F32)
    yield
    cs = cs3[:, 0:LANES] + cs3[:, LANES:2 * LANES] + cs3[:, 2 * LANES:]

    cs_t = cs.T
    dt_t = dt_tile.T
    lane_lo = lane < SSM_HEAD_DIM

    def per_head_lanes(tile):
        cols = [jnp.broadcast_to(tile[:, h:h + 1], (CHUNK, LANES)) for h in range(SSM_HEADS)]
        blocks = [jnp.where(lane_lo, cols[h], cols[h + 1]) for h in range(0, SSM_HEADS, 2)]
        return cols, jnp.concatenate(blocks, axis=1)

    cs_cols, acs_full = per_head_lanes(cs)
    _, dt_full = per_head_lanes(dt_tile)
    atot_full = acs_full[CHUNK - 1:CHUNK, :]
    dec_out = jnp.exp(acs_full)
    dec_end = jnp.exp(atot_full - acs_full)
    chunk_dec = jnp.exp(atot_full)
    xs_b = xs.astype(BF16)
    xw = (xs * (dec_end * dt_full)).astype(BF16)

    bgs = [b_all[:, g * SSM_STATE:(g + 1) * SSM_STATE] for g in range(SSM_GROUPS)]
    cgs = [c_all[:, g * SSM_STATE:(g + 1) * SSM_STATE].astype(BF16) for g in range(SSM_GROUPS)]
    cbms = [lax.dot_general(cg, bg.astype(BF16), (((1,), (1,)), ((), ())),
                            preferred_element_type=F32) for cg, bg in zip(cgs, bgs)]
    yield
    ydiag_blocks = []
    yoff_blocks = []
    for g in range(SSM_GROUPS):
        bg, cg, cbm = bgs[g], cgs[g], cbms[g]
        for j in range(HEADS_PER_GROUP // HEADS_PER_BLOCK):
            h0 = g * HEADS_PER_GROUP + HEADS_PER_BLOCK * j
            xp = xs_b[:, h0 * SSM_HEAD_DIM:h0 * SSM_HEAD_DIM + LANES]
            scs = []
            for h in range(h0, h0 + HEADS_PER_BLOCK):
                seg = cs_cols[h] - cs_t[h:h + 1, :]
                dec = jnp.exp(jnp.where(tril, seg, -1e30))
                scs.append((cbm * dec * dt_t[h:h + 1, :]).astype(BF16))
            zero = jnp.zeros_like(xp)
            xp2 = jnp.concatenate([jnp.where(lane_lo, xp, zero), jnp.where(lane_lo, zero, xp)], axis=0)
            ydiag_blocks.append(jnp.dot(jnp.concatenate(scs, axis=1), xp2,
                                        preferred_element_type=F32))
        st = st_ref[g]
        if seq_start is not None:
            st = jnp.where(seq_start, 0.0, st)
        yoff_blocks.append(jnp.dot(cg, st.astype(BF16), preferred_element_type=F32)
                           * dec_out[:, g * gw:(g + 1) * gw])
        st_ref[g] = st * chunk_dec[:, g * gw:(g + 1) * gw] + jnp.dot(
            bg.T.astype(BF16), xw[:, g * gw:(g + 1) * gw], preferred_element_type=F32)
    y = (jnp.concatenate(ydiag_blocks, axis=1) + jnp.concatenate(yoff_blocks, axis=1)
         + dsk_ref[...] * xs)
    gated = y * gate.astype(F32)
    ms = jnp.mean(gated * gated, axis=-1, keepdims=True)
    yield (gated * lax.rsqrt(ms + RMS_EPS) * nw_ref[...]).astype(BF16)


def _bias_routing():
    place = np.zeros((BIAS_PIECES * LANES, 2 * LANES), np.float32)
    ones_row = np.zeros((1, 2 * LANES), np.float32)
    for h in range(ATT_HEADS):
        base = BIAS_LANES_PER_HEAD * h
        for t in range(BIAS_PIECES):
            place[t * LANES + SMALL_F + h, base + t] = 1.0
            place[t * LANES + SMALL_F + h, LANES + base + BIAS_PIECES + t] = -1.0
            ones_row[0, base + BIAS_PIECES + t] = 1.0
            ones_row[0, LANES + base + t] = 1.0
    return jnp.asarray(place, BF16), jnp.asarray(ones_row, F32)


ATT_TK = 256
ATT_SUB = 256
ATT_UNROLL = 4
ATT_VT_ROWS = ATT_HEAD_DIM + BF16_SUBLANES


def _attn_kernel(q_ref, k_ref, vt_ref, fq_ref, fk_ref, o_ref, st_ref, mx_ref, acc_ref, m_ref, *, tq):
    pair = pl.program_id(1)
    qi = pl.program_id(2)
    n_sub = tq // ATT_SUB
    assert n_sub % ATT_UNROLL == 0
    hd = ATT_HEAD_DIM
    ones_rows = jnp.ones((ATT_VT_ROWS - hd, ATT_TK), BF16)

    lane = lax.broadcasted_iota(jnp.int32, (tq, LANES), 1)
    q = q_ref[...]
    fq = fq_ref[...]
    q_aug = []
    for i in range(HEADS_PER_BLOCK):
        first = BIAS_LANES_PER_HEAD * (HEADS_PER_BLOCK * pair + i)
        own_q = (lane < hd) if i == 0 else (lane >= hd)
        own_f = jnp.logical_and(lane >= first, lane < first + BIAS_LANES_PER_HEAD)
        q_aug.append(jnp.concatenate([jnp.where(own_q, q, jnp.zeros_like(q)),
                                      jnp.where(own_f, fq, jnp.zeros_like(fq))], axis=1))
    causal = (lax.broadcasted_iota(jnp.int32, (ATT_TK, ATT_SUB), 0)
              <= lax.broadcasted_iota(jnp.int32, (ATT_TK, ATT_SUB), 1))
    chains = [(i, s) for i in range(HEADS_PER_BLOCK) for s in range(n_sub)]

    def head_scores(i, block, s_from=0):
        rows = pl.ds(pl.multiple_of(block * ATT_TK, ATT_TK), ATT_TK)
        kb = jnp.concatenate([k_ref[rows, :], fk_ref[rows, :]], axis=1)
        qa = q_aug[i][s_from * ATT_SUB:, :]
        return lax.dot_general(kb, qa, (((1,), (1,)), ((), ())), preferred_element_type=F32)

    def update(c, block, st, st_max):
        m = m_ref[c]
        m_new = jnp.maximum(m, st_max)
        pt = jnp.exp2(st - m_new).astype(BF16)
        start = pl.multiple_of(block * ATT_TK, ATT_TK)
        i = chains[c][0]
        vt = jnp.concatenate([vt_ref[i * hd:(i + 1) * hd, pl.ds(start, ATT_TK)], ones_rows], axis=0)
        acc_ref[c] = (jnp.exp2(m - m_new) * acc_ref[c]
                      + jnp.dot(vt, pt, preferred_element_type=F32))
        m_ref[c] = m_new

    def stash(slot, block):
        for i in range(HEADS_PER_BLOCK):
            st = head_scores(i, block)
            for s in range(n_sub):
                part = st[:, s * ATT_SUB:(s + 1) * ATT_SUB]
                st_ref[slot, i * n_sub + s] = part
                mx_ref[slot, i * n_sub + s] = jnp.max(part, axis=0, keepdims=True)

    every = list(range(len(chains)))
    for c in every:
        m_ref[c] = jnp.full((1, ATT_SUB), -1e30, F32)
        acc_ref[c] = jnp.zeros((ATT_VT_ROWS, ATT_SUB), F32)

    first_diag = qi * n_sub
    diag = []
    for e in range(n_sub):
        for i in range(HEADS_PER_BLOCK):
            st = head_scores(i, first_diag + e, s_from=e)
            for s in range(e, n_sub):
                part = st[:, (s - e) * ATT_SUB:(s - e + 1) * ATT_SUB]
                if s == e:
                    part = jnp.where(causal, part, -1e30)
                diag.append((i * n_sub + s, first_diag + e, part))
    stash(0, 0)
    for c, block, part in diag:
        update(c, block, part, jnp.max(part, axis=0, keepdims=True))

    def half_step(block, slot, prefetch=True):
        if prefetch:
            stash(1 - slot, block + 1)
        for c in every:
            update(c, block, st_ref[slot, c], mx_ref[slot, c])

    def body(j, carry):
        for t in range(ATT_UNROLL):
            half_step(ATT_UNROLL * j + t, t % 2)
        return carry

    lax.fori_loop(0, qi * (n_sub // ATT_UNROLL) - 1, body, 0)

    @pl.when(qi > 0)
    def _():
        for t in range(ATT_UNROLL):
            half_step(first_diag - ATT_UNROLL + t, t % 2, prefetch=t < ATT_UNROLL - 1)

    heads = [[None] * n_sub for _ in range(HEADS_PER_BLOCK)]
    for c, (i, s) in enumerate(chains):
        acc = acc_ref[c]
        heads[i][s] = acc[0:hd, :] * (1.0 / acc[hd:hd + 1, :])
    o_t = jnp.concatenate([jnp.concatenate(h, axis=1) for h in heads], axis=0)
    o_ref[...] = o_t.astype(BF16)


def _attn(proj, v_t, feat_q, feat_k, batch, seq, q_col, k_col, tq=1024):
    n_tok = proj.shape[0]
    nq = seq // tq
    n_pairs = ATT_HEADS // HEADS_PER_BLOCK
    n_chains = HEADS_PER_BLOCK * (tq // ATT_SUB)
    return pl.pallas_call(
        functools.partial(_attn_kernel, tq=tq),
        grid=(batch, n_pairs, nq),
        in_specs=[pl.BlockSpec((tq, LANES), lambda b, p, i: (b * nq + i, q_col + p)),
                  pl.BlockSpec((seq, LANES), lambda b, p, i: (b, k_col + p)),
                  pl.BlockSpec((None, LANES, seq), lambda b, p, i: (b, p, 0)),
                  pl.BlockSpec((tq, LANES), lambda b, p, i: (b * nq + i, 0)),
                  pl.BlockSpec((seq, LANES), lambda b, p, i: (b, 0))],
        out_specs=pl.BlockSpec((None, LANES, tq), lambda b, p, i: (b, p, i)),
        out_shape=jax.ShapeDtypeStruct((batch, ATT_HEADS * ATT_HEAD_DIM, seq), BF16),
        scratch_shapes=[pltpu.VMEM((2, n_chains, ATT_TK, ATT_SUB), F32),
                        pltpu.VMEM((2, n_chains, 1, ATT_SUB), F32),
                        pltpu.VMEM((n_chains, ATT_VT_ROWS, ATT_SUB), F32),
                        pltpu.VMEM((n_chains, 1, ATT_SUB), F32)],
        compiler_params=pltpu.CompilerParams(
            dimension_semantics=("arbitrary", "arbitrary", "arbitrary"),
            vmem_limit_bytes=VMEM_LIMIT_BYTES),
        name="attn",
    )(proj, proj, v_t, feat_q, feat_k)


def _mix_tail_kernel(x_ref, ya_ref, mod_ref, wo_ref, anw_ref, g1_ref, b1_ref, w1_ref, w2_ref, g2_ref,
                     b2_ref, gate_ref, xs_ref, bc_ref, sm_ref, xh_ref, bch_ref, cw_ref, cb_ref,
                     brow_ref, alog_ref, dsk_ref, nw_ref, o_ref, ys_ref, st_ref, xf_ref, *,
                     alpha, ff_chunk, tiles_per_seq, n_tiles):
    i = pl.program_id(0)
    tm = x_ref.shape[0]
    slot = i % 2

    @pl.when(i == 0)
    def _():
        ys_ref[...] = jnp.zeros(ys_ref.shape, BF16)
        st_ref[...] = jnp.zeros(st_ref.shape, F32)

    seq_start = jnp.minimum(i, n_tiles - 1) % tiles_per_seq == 0
    width = xs_ref.shape[1]
    halo = xh_ref.shape[0]
    before = jnp.concatenate([xh_ref[...], bch_ref[...]], axis=1).astype(F32)
    xf_ref[0:halo, :] = jnp.where(seq_start, 0.0, before)
    xf_ref[halo:, 0:width] = xs_ref[...].astype(F32)
    xf_ref[halo:, width:] = bc_ref[...].astype(F32)

    def conv_silu(c):
        conv = cb_ref[...]
        for k in range(CONV_WIDTH):
            first = halo + c * CHUNK - k
            conv = conv + (cw_ref[CONV_WIDTH - 1 - k:CONV_WIDTH - k, :]
                           * xf_ref[first:first + CHUNK, :])
        return _silu(conv)

    n_chunks = tm // CHUNK
    chunk_rows = [slice(c * CHUNK, (c + 1) * CHUNK) for c in range(n_chunks)]
    ssd = [_ssd_rows(gate_ref[rows, :], functools.partial(conv_silu, c), sm_ref[rows, :],
                     seq_start if c == 0 else None, brow_ref, alog_ref, dsk_ref, nw_ref, st_ref)
           for c, rows in enumerate(chunk_rows)]

    def advance(c, last=False):
        if 0 <= c < n_chunks:
            out = next(ssd[c])
            if last:
                ys_ref[slot, chunk_rows[c], :] = out

    advance(0)

    ssm_w = ys_ref.shape[2]
    ya = ya_ref[...].astype(F32).T
    ya = ya * lax.rsqrt(jnp.mean(ya * ya, axis=-1, keepdims=True) + RMS_EPS) * anw_ref[...]
    y = (jnp.dot(ys_ref[1 - slot], wo_ref[0:ssm_w, :], preferred_element_type=F32)
         + jnp.dot(ya.astype(BF16), wo_ref[ssm_w:, :], preferred_element_type=F32))
    x1 = _layer_norm(alpha * x_ref[...] + (1.0 + mod_ref[2]) * y, g1_ref[...], b1_ref[...])
    h = (x1 * (1.0 + mod_ref[4]) + mod_ref[3]).astype(BF16)
    advance(0)

    n_ff = w1_ref.shape[1] // ff_chunk
    assert n_chunks == n_ff
    ff = jnp.zeros(x1.shape, F32)
    for c in range(n_ff):
        a = jnp.dot(h, w1_ref[:, c * ff_chunk:(c + 1) * ff_chunk], preferred_element_type=F32)
        a = jnp.maximum(a, 0.0)
        advance(c, last=True)
        advance(c + 1)
        ff = ff + jnp.dot((a * a).astype(BF16), w2_ref[c * ff_chunk:(c + 1) * ff_chunk, :],
                          preferred_element_type=F32)
        advance(c + 1)
    o_ref[...] = _layer_norm(alpha * x1 + (1.0 + mod_ref[5]) * ff, g2_ref[...], b2_ref[...])


def _mix_tail(x2, proj, small, y_att_t, mod4, w_out, anw, g1, b1, w1, w2, g2, b2, conv_w, conv_b,
              brow, alog_row, dsk_full, ssm_norm_w, seq, alpha, tm=512, ff_chunk=1024):
    n_tok, d = x2.shape
    mix = w_out.shape[0]
    d_ff = w1.shape[1]
    width = SSM_HEADS * SSM_HEAD_DIM
    bc_w = 2 * SSM_GROUPS * SSM_STATE
    x_col, bc_col = 1, 2 * width // bc_w
    tiles_per_seq = seq // tm
    n_tiles = n_tok // tm
    const = lambda i: (0, 0)
    resident = functools.partial(pl.BlockSpec, index_map=const, pipeline_mode=pl.Buffered(1))
    mlp_tile = lambda i: jnp.maximum(i - 1, 0)
    ssd_tile = lambda i: jnp.minimum(i, n_tiles - 1)
    halo = lambda i: jnp.maximum(ssd_tile(i) * (tm // BF16_SUBLANES) - 1, 0)
    return pl.pallas_call(
        functools.partial(_mix_tail_kernel, alpha=alpha, ff_chunk=ff_chunk,
                          tiles_per_seq=tiles_per_seq, n_tiles=n_tiles),
        grid=(n_tiles + 1,),
        in_specs=[pl.BlockSpec((tm, d), lambda i: (mlp_tile(i), 0)),
                  pl.BlockSpec((None, y_att_t.shape[1], tm),
                               lambda i: (mlp_tile(i) // tiles_per_seq, 0,
                                          mlp_tile(i) % tiles_per_seq)),
                  pl.BlockSpec((None, 6, 1, d), lambda i: (mlp_tile(i) // tiles_per_seq, 0, 0, 0)),
                  resident((mix, d)),
                  pl.BlockSpec((1, y_att_t.shape[1]), const),
                  pl.BlockSpec((1, d), const),
                  pl.BlockSpec((1, d), const),
                  resident((d, d_ff)),
                  resident((d_ff, d)),
                  pl.BlockSpec((1, d), const),
                  pl.BlockSpec((1, d), const),
                  pl.BlockSpec((tm, width), lambda i: (ssd_tile(i), 0)),
                  pl.BlockSpec((tm, width), lambda i: (ssd_tile(i), x_col)),
                  pl.BlockSpec((tm, bc_w), lambda i: (ssd_tile(i), bc_col)),
                  pl.BlockSpec((tm, LANES), lambda i: (ssd_tile(i), 0)),
                  pl.BlockSpec((BF16_SUBLANES, width), lambda i: (halo(i), x_col)),
                  pl.BlockSpec((BF16_SUBLANES, bc_w), lambda i: (halo(i), bc_col)),
                  pl.BlockSpec(conv_w.shape, const),
                  pl.BlockSpec(conv_b.shape, const),
                  pl.BlockSpec((1, LANES), const),
                  pl.BlockSpec((1, LANES), const),
                  pl.BlockSpec((1, width), const),
                  pl.BlockSpec((1, width), const)],
        out_specs=pl.BlockSpec((tm, d), lambda i: (mlp_tile(i), 0)),
        out_shape=jax.ShapeDtypeStruct((n_tok, d), F32),
        scratch_shapes=[pltpu.VMEM((2, tm, width), BF16),
                        pltpu.VMEM((SSM_GROUPS, SSM_STATE, HEADS_PER_GROUP * SSM_HEAD_DIM), F32),
                        pltpu.VMEM((BF16_SUBLANES + tm, width + bc_w), F32)],
        compiler_params=pltpu.CompilerParams(
            dimension_semantics=("arbitrary",), vmem_limit_bytes=VMEM_LIMIT_BYTES),
        name="mix_tail",
    )(x2, y_att_t, mod4, w_out, anw, g1, b1, w1, w2, g2, b2, proj, proj, proj, small, proj, proj,
      conv_w, conv_b, brow, alog_row, dsk_full, ssm_norm_w)


def _pad_lanes(v):
    return jnp.pad(v, ((0, 0), (0, LANES - v.shape[1])))


def kernel(x, c, w_ada, b_ada, w_in, conv_w, conv_b, dt_bias, a_log, d_skip, ssm_norm_w, f_bias,
           attn_norm_w, w_out, ln1_g, ln1_b, w_ff_in, w_ff_out, ln2_g, ln2_b):
    batch, seq, d = x.shape
    depth = w_ada.shape[0]
    alpha = (2.0 * depth) ** 0.25
    ssm_w = SSM_HEADS * SSM_HEAD_DIM
    att_w = ATT_HEADS * ATT_HEAD_DIM
    conv_dim = ssm_w + 2 * SSM_GROUPS * SSM_STATE
    o_xbc = ssm_w
    o_dt = o_xbc + conv_dim
    o_q = o_dt + SSM_HEADS
    o_k = o_q + att_w
    o_v = o_k + att_w
    o_f = o_v + att_w
    q_col = (ssm_w + conv_dim) // LANES
    k_col = q_col + att_w // LANES
    v_col = k_col + att_w // LANES

    x2 = x.reshape(batch * seq, d)
    c_pad = jnp.pad(c, ((0, SUBLANES - batch % SUBLANES if batch % SUBLANES else 0), (0, 0)))
    for l in range(depth):
        mod = _ada(c_pad, w_ada[l], b_ada[l][None, :])
        mod4 = mod[:batch].reshape(batch, 6, 1, d)
        wt = jnp.swapaxes(w_in[l], 0, 1).astype(BF16)
        wt_main = jnp.concatenate([wt[:o_dt], wt[o_q:o_f]], axis=0)
        wt_small = jnp.concatenate([wt[o_dt:o_q], wt[o_f:]], axis=0)
        wt_small = jnp.pad(wt_small, ((0, LANES - wt_small.shape[0]), (0, 0)))
        brow = _pad_lanes(jnp.concatenate([dt_bias[l], f_bias[l]])[None, :])
        proj, small, v_t, feat_q, feat_k = _inproj(
            x2, mod4, wt_main, wt_small, brow, batch, seq, ssm_w, q_col * LANES, k_col * LANES,
            v_col * LANES, ATT_HEAD_DIM ** -0.5 * LOG2E)
        y_att_t = _attn(proj, v_t, feat_q, feat_k, batch, seq, q_col, k_col)

        alog_row = _pad_lanes(a_log[l][None, :])
        dsk_full = jnp.repeat(d_skip[l], SSM_HEAD_DIM)[None, :]
        x2 = _mix_tail(x2, proj, small, y_att_t, mod4, w_out[l].astype(BF16),
                       attn_norm_w[l][None, :], ln1_g[l][None, :], ln1_b[l][None, :],
                       w_ff_in[l].astype(BF16), w_ff_out[l].astype(BF16), ln2_g[l][None, :],
                       ln2_b[l][None, :], conv_w[l], conv_b[l][None, :], brow, alog_row, dsk_full,
                       ssm_norm_w[l][None, :], seq, alpha)
    return x2.reshape(batch, seq, d)
```

```python
import functools

import jax
import jax.numpy as jnp
import numpy as np
from jax import lax
from jax.experimental import pallas as pl
from jax.experimental.pallas import tpu as pltpu

F32 = jnp.float32
BF16 = jnp.bfloat16
HIGHEST = lax.Precision.HIGHEST

LANES = 128
SUBLANES = 8
BF16_SUBLANES = 16
VMEM_LIMIT_BYTES = 56 * 1024 * 1024

SSM_HEADS = 16
SSM_HEAD_DIM = 64
SSM_GROUPS = 2
SSM_STATE = 128
CONV_WIDTH = 4
CHUNK = 128
ATT_HEADS = 16
ATT_HEAD_DIM = 64
LN_EPS = 1e-5
RMS_EPS = 1e-5

HEADS_PER_GROUP = SSM_HEADS // SSM_GROUPS
HEADS_PER_BLOCK = LANES // ATT_HEAD_DIM
SMALL_DT = 0
SMALL_F = SSM_HEADS
LOG2E = 1.4426950408889634
BIAS_PIECES = 3
BIAS_LANES_PER_HEAD = 2 * BIAS_PIECES


def _silu(v):
    return v * (1.0 / (1.0 + jnp.exp(-v)))


def _softplus(v):
    return jnp.maximum(v, 0.0) + jnp.log(1.0 + jnp.exp(-jnp.abs(v)))


def _split3(c):
    hi = c.astype(BF16).astype(F32)
    r = c - hi
    mid = r.astype(BF16).astype(F32)
    return hi, mid, r - mid


def _pieces(c):
    return jnp.concatenate(_split3(c), axis=1).astype(BF16)


def _layer_norm(v, g, b):
    mu = jnp.mean(v, axis=-1, keepdims=True)
    d = v - mu
    var = jnp.mean(d * d, axis=-1, keepdims=True)
    return d * lax.rsqrt(var + LN_EPS) * g + b


def _ada_kernel(c_ref, w_ref, b_ref, o_ref):
    ca = _silu(c_ref[...]).astype(BF16)
    o_ref[...] = jnp.dot(ca, w_ref[...].astype(BF16), preferred_element_type=F32) + b_ref[...]


def _ada(c_pad, w, b, tn=512):
    rows, d = c_pad.shape
    n = w.shape[1]
    return pl.pallas_call(
        _ada_kernel,
        grid=(n // tn,),
        in_specs=[pl.BlockSpec((rows, d), lambda j: (0, 0)),
                  pl.BlockSpec((d, tn), lambda j: (0, j)),
                  pl.BlockSpec((1, tn), lambda j: (0, j))],
        out_specs=pl.BlockSpec((rows, tn), lambda j: (0, j)),
        out_shape=jax.ShapeDtypeStruct((rows, n), F32),
        name="ada",
    )(c_pad, w, b)


def _inproj_kernel(x_ref, mod_ref, wt_ref, wst_ref, brow_ref, place_ref, ones_ref, o_ref, os_ref,
                   ovt_ref, fq_ref, fk_ref, car_ref, *, tn, tiles_per_seq, z_hi, q_lo, q_hi, q_scale):
    nt = (((1,), (1,)), ((), ()))
    tm = x_ref.shape[0]
    v_lo = o_ref.shape[1]

    @pl.when(pl.program_id(0) == 0)
    def _():
        car_ref[...] = jnp.zeros(car_ref.shape, F32)

    h = (x_ref[...] * (1.0 + mod_ref[1]) + mod_ref[0]).astype(BF16)
    small = lax.dot_general(h, wst_ref[...], nt, preferred_element_type=F32)
    os_ref[...] = small

    lane = lax.broadcasted_iota(jnp.int32, (tm, LANES), 1)
    is_f = jnp.logical_and(lane >= SMALL_F, lane < SMALL_F + ATT_HEADS)
    log_f = jnp.where(is_f, -_softplus(-(small + brow_ref[...])), 0.0)
    tril = (lax.broadcasted_iota(jnp.int32, (CHUNK, CHUNK), 0)
            >= lax.broadcasted_iota(jnp.int32, (CHUNK, CHUNK), 1)).astype(BF16)
    seq_start = pl.program_id(0) % tiles_per_seq == 0
    carry = jnp.where(seq_start, 0.0, car_ref[...])
    cums = []
    for r0 in range(0, tm, CHUNK):
        cs3 = jnp.dot(tril, _pieces(log_f[r0:r0 + CHUNK, :]), preferred_element_type=F32)
        cums.append(cs3[:, 0:LANES] + cs3[:, LANES:2 * LANES] + cs3[:, 2 * LANES:] + carry)
        carry = cums[-1][CHUNK - 1:CHUNK, :]
    car_ref[...] = carry
    cum = jnp.concatenate(cums, axis=0)
    feats = jnp.dot(_pieces(cum * LOG2E), place_ref[...], preferred_element_type=F32) + ones_ref[...]
    fq_ref[...] = feats[:, 0:LANES].astype(BF16)
    fk_ref[...] = feats[:, LANES:].astype(BF16)

    for c0 in range(0, wt_ref.shape[0], tn):
        acc = lax.dot_general(h, wt_ref[c0:c0 + tn, :], nt, preferred_element_type=F32)
        if c0 >= v_lo:
            ovt_ref[c0 - v_lo:c0 - v_lo + tn, :] = acc.T.astype(BF16)
            continue
        if c0 < z_hi:
            acc = _silu(acc)
        elif q_lo <= c0 < q_hi:
            acc = acc * q_scale
        o_ref[:, c0:c0 + tn] = acc.astype(BF16)


def _inproj(x2, mod4, wt_main, wt_small, brow, batch, seq, z_hi, q_lo, q_hi, v_lo, q_scale, tm=512,
            tn=512):
    n_tok, d = x2.shape
    n_main = wt_main.shape[0]
    assert all(c % tn == 0 for c in (z_hi, q_lo, q_hi, v_lo, n_main))
    tiles_per_seq = seq // tm
    place, ones_row = _bias_routing()
    const = lambda i: (0, 0)
    tok = lambda i: (i, 0)
    resident = functools.partial(pl.BlockSpec, index_map=const, pipeline_mode=pl.Buffered(1))
    return pl.pallas_call(
        functools.partial(_inproj_kernel, tn=tn, tiles_per_seq=tiles_per_seq, z_hi=z_hi, q_lo=q_lo,
                          q_hi=q_hi, q_scale=q_scale),
        grid=(n_tok // tm,),
        in_specs=[pl.BlockSpec((tm, d), tok),
                  pl.BlockSpec((None, 6, 1, d), lambda i: (i // tiles_per_seq, 0, 0, 0)),
                  resident((n_main, d)),
                  resident((LANES, d)),
                  pl.BlockSpec((1, LANES), const),
                  pl.BlockSpec(place.shape, const),
                  pl.BlockSpec(ones_row.shape, const)],
        out_specs=[pl.BlockSpec((tm, v_lo), tok),
                   pl.BlockSpec((tm, LANES), tok),
                   pl.BlockSpec((None, n_main - v_lo, tm),
                                lambda i: (i // tiles_per_seq, 0, i % tiles_per_seq)),
                   pl.BlockSpec((tm, LANES), tok),
                   pl.BlockSpec((tm, LANES), tok)],
        out_shape=[jax.ShapeDtypeStruct((n_tok, v_lo), BF16),
                   jax.ShapeDtypeStruct((n_tok, LANES), F32),
                   jax.ShapeDtypeStruct((batch, n_main - v_lo, seq), BF16),
                   jax.ShapeDtypeStruct((n_tok, LANES), BF16),
                   jax.ShapeDtypeStruct((n_tok, LANES), BF16)],
        scratch_shapes=[pltpu.VMEM((1, LANES), F32)],
        compiler_params=pltpu.CompilerParams(
            dimension_semantics=("arbitrary",), vmem_limit_bytes=VMEM_LIMIT_BYTES),
        name="inproj",
    )(x2, mod4, wt_main, wt_small, brow, place, ones_row)


def _ssd_rows(gate, conv_silu, small, seq_start, brow_ref, alog_ref, dsk_ref, nw_ref, st_ref):
    width = SSM_HEADS * SSM_HEAD_DIM
    gw = HEADS_PER_GROUP * SSM_HEAD_DIM
    u = conv_silu()
    xs = u[:, 0:width]
    b_all = u[:, width:width + SSM_GROUPS * SSM_STATE]
    c_all = u[:, width + SSM_GROUPS * SSM_STATE:]

    lane = lax.broadcasted_iota(jnp.int32, (CHUNK, LANES), 1)
    row = lax.broadcasted_iota(jnp.int32, (CHUNK, LANES), 0)
    is_dt = lane < SMALL_F
    sp = _softplus(small + brow_ref[...])
    a_row = -jnp.exp(alog_ref[...])
    dt_tile = jnp.where(is_dt, sp, 0.0)
    val = jnp.where(is_dt, sp * a_row, 0.0)
    tril = row >= lane
    cs3 = jnp.dot(tril.astype(BF16), _pieces(val), preferred_element_type=F32)
    yield
    cs = cs3[:, 0:LANES] + cs3[:, LANES:2 * LANES] + cs3[:, 2 * LANES:]

    cs_t = cs.T
    dt_t = dt_tile.T
    lane_lo = lane < SSM_HEAD_DIM

    def per_head_lanes(tile):
        cols = [jnp.broadcast_to(tile[:, h:h + 1], (CHUNK, LANES)) for h in range(SSM_HEADS)]
        blocks = [jnp.where(lane_lo, cols[h], cols[h + 1]) for h in range(0, SSM_HEADS, 2)]
        return cols, jnp.concatenate(blocks, axis=1)

    cs_cols, acs_full = per_head_lanes(cs)
    _, dt_full = per_head_lanes(dt_tile)
    atot_full = acs_full[CHUNK - 1:CHUNK, :]
    dec_out = jnp.exp(acs_full)
    dec_end = jnp.exp(atot_full - acs_full)
    chunk_dec = jnp.exp(atot_full)
    xs_b = xs.astype(BF16)
    xw = (xs * (dec_end * dt_full)).astype(BF16)

    bgs = [b_all[:, g * SSM_STATE:(g + 1) * SSM_STATE] for g in range(SSM_GROUPS)]
    cgs = [c_all[:, g * SSM_STATE:(g + 1) * SSM_STATE].astype(BF16) for g in range(SSM_GROUPS)]
    cbms = [lax.dot_general(cg, bg.astype(BF16), (((1,), (1,)), ((), ())),
                            preferred_element_type=F32) for cg, bg in zip(cgs, bgs)]
    yield
    ydiag_blocks = []
    yoff_blocks = []
    for g in range(SSM_GROUPS):
        bg, cg, cbm = bgs[g], cgs[g], cbms[g]
        for j in range(HEADS_PER_GROUP // HEADS_PER_BLOCK):
            h0 = g * HEADS_PER_GROUP + HEADS_PER_BLOCK * j
            xp = xs_b[:, h0 * SSM_HEAD_DIM:h0 * SSM_HEAD_DIM + LANES]
            scs = []
            for h in range(h0, h0 + HEADS_PER_BLOCK):
                seg = cs_cols[h] - cs_t[h:h + 1, :]
                dec = jnp.exp(jnp.where(tril, seg, -1e30))
                scs.append((cbm * dec * dt_t[h:h + 1, :]).astype(BF16))
            zero = jnp.zeros_like(xp)
            xp2 = jnp.concatenate([jnp.where(lane_lo, xp, zero), jnp.where(lane_lo, zero, xp)], axis=0)
            ydiag_blocks.append(jnp.dot(jnp.concatenate(scs, axis=1), xp2,
                                        preferred_element_type=F32))
        st = st_ref[g]
        if seq_start is not None:
            st = jnp.where(seq_start, 0.0, st)
        yoff_blocks.append(jnp.dot(cg, st.astype(BF16), preferred_element_type=F32)
                           * dec_out[:, g * gw:(g + 1) * gw])
        st_ref[g] = st * chunk_dec[:, g * gw:(g + 1) * gw] + jnp.dot(
            bg.T.astype(BF16), xw[:, g * gw:(g + 1) * gw], preferred_element_type=F32)
    y = (jnp.concatenate(ydiag_blocks, axis=1) + jnp.concatenate(yoff_blocks, axis=1)
         + dsk_ref[...] * xs)
    gated = y * gate.astype(F32)
    ms = jnp.mean(gated * gated, axis=-1, keepdims=True)
    yield (gated * lax.rsqrt(ms + RMS_EPS) * nw_ref[...]).astype(BF16)


def _bias_routing():
    place = np.zeros((BIAS_PIECES * LANES, 2 * LANES), np.float32)
    ones_row = np.zeros((1, 2 * LANES), np.float32)
    for h in range(ATT_HEADS):
        base = BIAS_LANES_PER_HEAD * h
        for t in range(BIAS_PIECES):
            place[t * LANES + SMALL_F + h, base + t] = 1.0
            place[t * LANES + SMALL_F + h, LANES + base + BIAS_PIECES + t] = -1.0
            ones_row[0, base + BIAS_PIECES + t] = 1.0
            ones_row[0, LANES + base + t] = 1.0
    return jnp.asarray(place, BF16), jnp.asarray(ones_row, F32)


ATT_TK = 256
ATT_SUB = 256
ATT_UNROLL = 4
ATT_VT_ROWS = ATT_HEAD_DIM + BF16_SUBLANES


def _attn_kernel(q_ref, k_ref, vt_ref, fq_ref, fk_ref, o_ref, st_ref, mx_ref, acc_ref, m_ref, *, tq):
    pair = pl.program_id(1)
    qi = pl.program_id(2)
    n_sub = tq // ATT_SUB
    assert n_sub % ATT_UNROLL == 0
    hd = ATT_HEAD_DIM
    ones_rows = jnp.ones((ATT_VT_ROWS - hd, ATT_TK), BF16)

    lane = lax.broadcasted_iota(jnp.int32, (tq, LANES), 1)
    q = q_ref[...]
    fq = fq_ref[...]
    q_aug = []
    for i in range(HEADS_PER_BLOCK):
        first = BIAS_LANES_PER_HEAD * (HEADS_PER_BLOCK * pair + i)
        own_q = (lane < hd) if i == 0 else (lane >= hd)
        own_f = jnp.logical_and(lane >= first, lane < first + BIAS_LANES_PER_HEAD)
        q_aug.append(jnp.concatenate([jnp.where(own_q, q, jnp.zeros_like(q)),
                                      jnp.where(own_f, fq, jnp.zeros_like(fq))], axis=1))
    causal = (lax.broadcasted_iota(jnp.int32, (ATT_TK, ATT_SUB), 0)
              <= lax.broadcasted_iota(jnp.int32, (ATT_TK, ATT_SUB), 1))
    chains = [(i, s) for i in range(HEADS_PER_BLOCK) for s in range(n_sub)]

    def head_scores(i, block, s_from=0):
        rows = pl.ds(pl.multiple_of(block * ATT_TK, ATT_TK), ATT_TK)
        kb = jnp.concatenate([k_ref[rows, :], fk_ref[rows, :]], axis=1)
        qa = q_aug[i][s_from * ATT_SUB:, :]
        return lax.dot_general(kb, qa, (((1,), (1,)), ((), ())), preferred_element_type=F32)

    def update(c, block, st, st_max):
        m = m_ref[c]
        m_new = jnp.maximum(m, st_max)
        pt = jnp.exp2(st - m_new).astype(BF16)
        start = pl.multiple_of(block * ATT_TK, ATT_TK)
        i = chains[c][0]
        vt = jnp.concatenate([vt_ref[i * hd:(i + 1) * hd, pl.ds(start, ATT_TK)], ones_rows], axis=0)
        acc_ref[c] = (jnp.exp2(m - m_new) * acc_ref[c]
                      + jnp.dot(vt, pt, preferred_element_type=F32))
        m_ref[c] = m_new

    def stash(slot, block):
        for i in range(HEADS_PER_BLOCK):
            st = head_scores(i, block)
            for s in range(n_sub):
                part = st[:, s * ATT_SUB:(s + 1) * ATT_SUB]
                st_ref[slot, i * n_sub + s] = part
                mx_ref[slot, i * n_sub + s] = jnp.max(part, axis=0, keepdims=True)

    every = list(range(len(chains)))
    for c in every:
        m_ref[c] = jnp.full((1, ATT_SUB), -1e30, F32)
        acc_ref[c] = jnp.zeros((ATT_VT_ROWS, ATT_SUB), F32)

    first_diag = qi * n_sub
    diag = []
    for e in range(n_sub):
        for i in range(HEADS_PER_BLOCK):
            st = head_scores(i, first_diag + e, s_from=e)
            for s in range(e, n_sub):
                part = st[:, (s - e) * ATT_SUB:(s - e + 1) * ATT_SUB]
                if s == e:
                    part = jnp.where(causal, part, -1e30)
                diag.append((i * n_sub + s, first_diag + e, part))
    stash(0, 0)
    for c, block, part in diag:
        update(c, block, part, jnp.max(part, axis=0, keepdims=True))

    def half_step(block, slot, prefetch=True):
        if prefetch:
            stash(1 - slot, block + 1)
        for c in every:
            update(c, block, st_ref[slot, c], mx_ref[slot, c])

    def body(j, carry):
        for t in range(ATT_UNROLL):
            half_step(ATT_UNROLL * j + t, t % 2)
        return carry

    lax.fori_loop(0, qi * (n_sub // ATT_UNROLL) - 1, body, 0)

    @pl.when(qi > 0)
    def _():
        for t in range(ATT_UNROLL):
            half_step(first_diag - ATT_UNROLL + t, t % 2, prefetch=t < ATT_UNROLL - 1)

    heads = [[None] * n_sub for _ in range(HEADS_PER_BLOCK)]
    for c, (i, s) in enumerate(chains):
        acc = acc_ref[c]
        heads[i][s] = acc[0:hd, :] * (1.0 / acc[hd:hd + 1, :])
    o_t = jnp.concatenate([jnp.concatenate(h, axis=1) for h in heads], axis=0)
    o_ref[...] = o_t.astype(BF16)


def _attn(proj, v_t, feat_q, feat_k, batch, seq, q_col, k_col, tq=2048):
    n_tok = proj.shape[0]
    nq = seq // tq
    n_pairs = ATT_HEADS // HEADS_PER_BLOCK
    n_chains = HEADS_PER_BLOCK * (tq // ATT_SUB)
    return pl.pallas_call(
        functools.partial(_attn_kernel, tq=tq),
        grid=(batch, n_pairs, nq),
        in_specs=[pl.BlockSpec((tq, LANES), lambda b, p, i: (b * nq + i, q_col + p)),
                  pl.BlockSpec((seq, LANES), lambda b, p, i: (b, k_col + p)),
                  pl.BlockSpec((None, LANES, seq), lambda b, p, i: (b, p, 0)),
                  pl.BlockSpec((tq, LANES), lambda b, p, i: (b * nq + i, 0)),
                  pl.BlockSpec((seq, LANES), lambda b, p, i: (b, 0))],
        out_specs=pl.BlockSpec((None, LANES, tq), lambda b, p, i: (b, p, i)),
        out_shape=jax.ShapeDtypeStruct((batch, ATT_HEADS * ATT_HEAD_DIM, seq), BF16),
        scratch_shapes=[pltpu.VMEM((2, n_chains, ATT_TK, ATT_SUB), F32),
                        pltpu.VMEM((2, n_chains, 1, ATT_SUB), F32),
                        pltpu.VMEM((n_chains, ATT_VT_ROWS, ATT_SUB), F32),
                        pltpu.VMEM((n_chains, 1, ATT_SUB), F32)],
        compiler_params=pltpu.CompilerParams(
            dimension_semantics=("arbitrary", "arbitrary", "arbitrary"),
            vmem_limit_bytes=VMEM_LIMIT_BYTES),
        name="attn",
    )(proj, proj, v_t, feat_q, feat_k)


def _mix_tail_kernel(x_ref, ya_ref, mod_ref, wo_ref, anw_ref, g1_ref, b1_ref, w1_ref, w2_ref, g2_ref,
                     b2_ref, gate_ref, xs_ref, bc_ref, sm_ref, xh_ref, bch_ref, cw_ref, cb_ref,
                     brow_ref, alog_ref, dsk_ref, nw_ref, o_ref, ys_ref, st_ref, xf_ref, *,
                     alpha, ff_chunk, tiles_per_seq, n_tiles):
    i = pl.program_id(0)
    tm = x_ref.shape[0]
    slot = i % 2

    @pl.when(i == 0)
    def _():
        ys_ref[...] = jnp.zeros(ys_ref.shape, BF16)
        st_ref[...] = jnp.zeros(st_ref.shape, F32)

    seq_start = jnp.minimum(i, n_tiles - 1) % tiles_per_seq == 0
    width = xs_ref.shape[1]
    halo = xh_ref.shape[0]
    before = jnp.concatenate([xh_ref[...], bch_ref[...]], axis=1).astype(F32)
    xf_ref[0:halo, :] = jnp.where(seq_start, 0.0, before)
    xf_ref[halo:, 0:width] = xs_ref[...].astype(F32)
    xf_ref[halo:, width:] = bc_ref[...].astype(F32)

    def conv_silu(c):
        conv = cb_ref[...]
        for k in range(CONV_WIDTH):
            first = halo + c * CHUNK - k
            conv = conv + (cw_ref[CONV_WIDTH - 1 - k:CONV_WIDTH - k, :]
                           * xf_ref[first:first + CHUNK, :])
        return _silu(conv)

    n_chunks = tm // CHUNK
    chunk_rows = [slice(c * CHUNK, (c + 1) * CHUNK) for c in range(n_chunks)]
    ssd = [_ssd_rows(gate_ref[rows, :], functools.partial(conv_silu, c), sm_ref[rows, :],
                     seq_start if c == 0 else None, brow_ref, alog_ref, dsk_ref, nw_ref, st_ref)
           for c, rows in enumerate(chunk_rows)]

    def advance(c, last=False):
        if 0 <= c < n_chunks:
            out = next(ssd[c])
            if last:
                ys_ref[slot, chunk_rows[c], :] = out

    advance(0)

    ssm_w = ys_ref.shape[2]
    ya = ya_ref[...].astype(F32).T
    ya = ya * lax.rsqrt(jnp.mean(ya * ya, axis=-1, keepdims=True) + RMS_EPS) * anw_ref[...]
    y = (jnp.dot(ys_ref[1 - slot], wo_ref[0:ssm_w, :], preferred_element_type=F32)
         + jnp.dot(ya.astype(BF16), wo_ref[ssm_w:, :], preferred_element_type=F32))
    x1 = _layer_norm(alpha * x_ref[...] + (1.0 + mod_ref[2]) * y, g1_ref[...], b1_ref[...])
    h = (x1 * (1.0 + mod_ref[4]) + mod_ref[3]).astype(BF16)
    advance(0)

    n_ff = w1_ref.shape[1] // ff_chunk
    assert n_chunks == n_ff
    ff = jnp.zeros(x1.shape, F32)
    for c in range(n_ff):
        a = jnp.dot(h, w1_ref[:, c * ff_chunk:(c + 1) * ff_chunk], preferred_element_type=F32)
        a = jnp.maximum(a, 0.0)
        advance(c, last=True)
        advance(c + 1)
        ff = ff + jnp.dot((a * a).astype(BF16), w2_ref[c * ff_chunk:(c + 1) * ff_chunk, :],
                          preferred_element_type=F32)
        advance(c + 1)
    o_ref[...] = _layer_norm(alpha * x1 + (1.0 + mod_ref[5]) * ff, g2_ref[...], b2_ref[...])


def _mix_tail(x2, proj, small, y_att_t, mod4, w_out, anw, g1, b1, w1, w2, g2, b2, conv_w, conv_b,
              brow, alog_row, dsk_full, ssm_norm_w, seq, alpha, tm=512, ff_chunk=1024):
    n_tok, d = x2.shape
    mix = w_out.shape[0]
    d_ff = w1.shape[1]
    width = SSM_HEADS * SSM_HEAD_DIM
    bc_w = 2 * SSM_GROUPS * SSM_STATE
    x_col, bc_col = 1, 2 * width // bc_w
    tiles_per_seq = seq // tm
    n_tiles = n_tok // tm
    const = lambda i: (0, 0)
    resident = functools.partial(pl.BlockSpec, index_map=const, pipeline_mode=pl.Buffered(1))
    mlp_tile = lambda i: jnp.maximum(i - 1, 0)
    ssd_tile = lambda i: jnp.minimum(i, n_tiles - 1)
    halo = lambda i: jnp.maximum(ssd_tile(i) * (tm // BF16_SUBLANES) - 1, 0)
    return pl.pallas_call(
        functools.partial(_mix_tail_kernel, alpha=alpha, ff_chunk=ff_chunk,
                          tiles_per_seq=tiles_per_seq, n_tiles=n_tiles),
        grid=(n_tiles + 1,),
        in_specs=[pl.BlockSpec((tm, d), lambda i: (mlp_tile(i), 0)),
                  pl.BlockSpec((None, y_att_t.shape[1], tm),
                               lambda i: (mlp_tile(i) // tiles_per_seq, 0,
                                          mlp_tile(i) % tiles_per_seq)),
                  pl.BlockSpec((None, 6, 1, d), lambda i: (mlp_tile(i) // tiles_per_seq, 0, 0, 0)),
                  resident((mix, d)),
                  pl.BlockSpec((1, y_att_t.shape[1]), const),
                  pl.BlockSpec((1, d), const),
                  pl.BlockSpec((1, d), const),
                  resident((d, d_ff)),
                  resident((d_ff, d)),
                  pl.BlockSpec((1, d), const),
                  pl.BlockSpec((1, d), const),
                  pl.BlockSpec((tm, width), lambda i: (ssd_tile(i), 0)),
                  pl.BlockSpec((tm, width), lambda i: (ssd_tile(i), x_col)),
                  pl.BlockSpec((tm, bc_w), lambda i: (ssd_tile(i), bc_col)),
                  pl.BlockSpec((tm, LANES), lambda i: (ssd_tile(i), 0)),
                  pl.BlockSpec((BF16_SUBLANES, width), lambda i: (halo(i), x_col)),
                  pl.BlockSpec((BF16_SUBLANES, bc_w), lambda i: (halo(i), bc_col)),
                  pl.BlockSpec(conv_w.shape, const),
                  pl.BlockSpec(conv_b.shape, const),
                  pl.BlockSpec((1, LANES), const),
                  pl.BlockSpec((1, LANES), const),
                  pl.BlockSpec((1, width), const),
                  pl.BlockSpec((1, width), const)],
        out_specs=pl.BlockSpec((tm, d), lambda i: (mlp_tile(i), 0)),
        out_shape=jax.ShapeDtypeStruct((n_tok, d), F32),
        scratch_shapes=[pltpu.VMEM((2, tm, width), BF16),
                        pltpu.VMEM((SSM_GROUPS, SSM_STATE, HEADS_PER_GROUP * SSM_HEAD_DIM), F32),
                        pltpu.VMEM((BF16_SUBLANES + tm, width + bc_w), F32)],
        compiler_params=pltpu.CompilerParams(
            dimension_semantics=("arbitrary",), vmem_limit_bytes=VMEM_LIMIT_BYTES),
        name="mix_tail",
    )(x2, y_att_t, mod4, w_out, anw, g1, b1, w1, w2, g2, b2, proj, proj, proj, small, proj, proj,
      conv_w, conv_b, brow, alog_row, dsk_full, ssm_norm_w)


def _pad_lanes(v):
    return jnp.pad(v, ((0, 0), (0, LANES - v.shape[1])))


def kernel(x, c, w_ada, b_ada, w_in, conv_w, conv_b, dt_bias, a_log, d_skip, ssm_norm_w, f_bias,
           attn_norm_w, w_out, ln1_g, ln1_b, w_ff_in, w_ff_out, ln2_g, ln2_b):
    batch, seq, d = x.shape
    depth = w_ada.shape[0]
    alpha = (2.0 * depth) ** 0.25
    ssm_w = SSM_HEADS * SSM_HEAD_DIM
    att_w = ATT_HEADS * ATT_HEAD_DIM
    conv_dim = ssm_w + 2 * SSM_GROUPS * SSM_STATE
    o_xbc = ssm_w
    o_dt = o_xbc + conv_dim
    o_q = o_dt + SSM_HEADS
    o_k = o_q + att_w
    o_v = o_k + att_w
    o_f = o_v + att_w
    q_col = (ssm_w + conv_dim) // LANES
    k_col = q_col + att_w // LANES
    v_col = k_col + att_w // LANES

    x2 = x.reshape(batch * seq, d)
    c_pad = jnp.pad(c, ((0, SUBLANES - batch % SUBLANES if batch % SUBLANES else 0), (0, 0)))
    for l in range(depth):
        mod = _ada(c_pad, w_ada[l], b_ada[l][None, :])
        mod4 = mod[:batch].reshape(batch, 6, 1, d)
        wt = jnp.swapaxes(w_in[l], 0, 1).astype(BF16)
        wt_main = jnp.concatenate([wt[:o_dt], wt[o_q:o_f]], axis=0)
        wt_small = jnp.concatenate([wt[o_dt:o_q], wt[o_f:]], axis=0)
        wt_small = jnp.pad(wt_small, ((0, LANES - wt_small.shape[0]), (0, 0)))
        brow = _pad_lanes(jnp.concatenate([dt_bias[l], f_bias[l]])[None, :])
        proj, small, v_t, feat_q, feat_k = _inproj(
            x2, mod4, wt_main, wt_small, brow, batch, seq, ssm_w, q_col * LANES, k_col * LANES,
            v_col * LANES, ATT_HEAD_DIM ** -0.5 * LOG2E)
        y_att_t = _attn(proj, v_t, feat_q, feat_k, batch, seq, q_col, k_col)

        alog_row = _pad_lanes(a_log[l][None, :])
        dsk_full = jnp.repeat(d_skip[l], SSM_HEAD_DIM)[None, :]
        x2 = _mix_tail(x2, proj, small, y_att_t, mod4, w_out[l].astype(BF16),
                       attn_norm_w[l][None, :], ln1_g[l][None, :], ln1_b[l][None, :],
                       w_ff_in[l].astype(BF16), w_ff_out[l].astype(BF16), ln2_g[l][None, :],
                       ln2_b[l][None, :], conv_w[l], conv_b[l][None, :], brow, alog_row, dsk_full,
                       ssm_norm_w[l][None, :], seq, alpha)
    return x2.reshape(batch, seq, d)
```

```python
import functools

import jax
import jax.numpy as jnp
import numpy as np
from jax import lax
from jax.experimental import pallas as pl
from jax.experimental.pallas import tpu as pltpu

F32 = jnp.float32
BF16 = jnp.bfloat16

LANES = 128
SUBLANES = 8
BF16_SUBLANES = 16
VMEM_LIMIT_BYTES = 56 * 1024 * 1024

SSM_HEADS = 16
SSM_HEAD_DIM = 64
SSM_GROUPS = 2
SSM_STATE = 128
CONV_WIDTH = 4
CHUNK = 128
ATT_HEADS = 16
ATT_HEAD_DIM = 64
LN_EPS = 1e-5
RMS_EPS = 1e-5

HEADS_PER_GROUP = SSM_HEADS // SSM_GROUPS
HEADS_PER_BLOCK = LANES // ATT_HEAD_DIM
SMALL_DT = 0
SMALL_F = SSM_HEADS
LOG2E = 1.4426950408889634
BIAS_PIECES = 3
BIAS_LANES_PER_HEAD = 2 * BIAS_PIECES


def _silu(v):
    return v * (1.0 / (1.0 + jnp.exp(-v)))


def _softplus(v):
    return jnp.maximum(v, 0.0) + jnp.log(1.0 + jnp.exp(-jnp.abs(v)))


def _split3(c):
    hi = c.astype(BF16).astype(F32)
    r = c - hi
    mid = r.astype(BF16).astype(F32)
    return hi, mid, r - mid


def _pieces(c):
    return jnp.concatenate(_split3(c), axis=1).astype(BF16)


def _layer_norm(v, g, b):
    mu = jnp.mean(v, axis=-1, keepdims=True)
    d = v - mu
    var = jnp.mean(d * d, axis=-1, keepdims=True)
    return d * lax.rsqrt(var + LN_EPS) * g + b


def _ada_kernel(c_ref, w_ref, b_ref, o_ref):
    ca = _silu(c_ref[...]).astype(BF16)
    o_ref[...] = jnp.dot(ca, w_ref[...].astype(BF16), preferred_element_type=F32) + b_ref[...]


def _ada(c_pad, w, b, tn=2048):
    rows, d = c_pad.shape
    n = w.shape[1]
    return pl.pallas_call(
        _ada_kernel,
        grid=(n // tn,),
        in_specs=[pl.BlockSpec((rows, d), lambda j: (0, 0)),
                  pl.BlockSpec((d, tn), lambda j: (0, j)),
                  pl.BlockSpec((1, tn), lambda j: (0, j))],
        out_specs=pl.BlockSpec((rows, tn), lambda j: (0, j)),
        out_shape=jax.ShapeDtypeStruct((rows, n), F32),
        compiler_params=pltpu.CompilerParams(
            dimension_semantics=("arbitrary",), vmem_limit_bytes=VMEM_LIMIT_BYTES),
        name="ada",
    )(c_pad, w, b)


def _inproj_kernel(x_ref, mod_ref, wt_ref, wst_ref, brow_ref, place_ref, ones_ref, o_ref, os_ref,
                   ovt_ref, fq_ref, fk_ref, car_ref, *, tn, tiles_per_seq, n_main, gap_at, gap, z_hi,
                   q_lo, q_hi, q_scale):
    nt = (((1,), (1,)), ((), ()))
    tm = x_ref.shape[0]
    v_lo = o_ref.shape[1]

    @pl.when(pl.program_id(0) == 0)
    def _():
        car_ref[...] = jnp.zeros(car_ref.shape, F32)

    h = (x_ref[...] * (1.0 + mod_ref[1]) + mod_ref[0]).astype(BF16)
    small = lax.dot_general(h, wst_ref[...], nt, preferred_element_type=F32)
    os_ref[...] = small

    lane = lax.broadcasted_iota(jnp.int32, (tm, LANES), 1)
    is_f = jnp.logical_and(lane >= SMALL_F, lane < SMALL_F + ATT_HEADS)
    log_f = jnp.where(is_f, -_softplus(-(small + brow_ref[...])), 0.0)
    tril = (lax.broadcasted_iota(jnp.int32, (CHUNK, CHUNK), 0)
            >= lax.broadcasted_iota(jnp.int32, (CHUNK, CHUNK), 1)).astype(BF16)
    seq_start = pl.program_id(0) % tiles_per_seq == 0
    carry = jnp.where(seq_start, 0.0, car_ref[...])
    cums = []
    for r0 in range(0, tm, CHUNK):
        cs3 = jnp.dot(tril, _pieces(log_f[r0:r0 + CHUNK, :]), preferred_element_type=F32)
        cums.append(cs3[:, 0:LANES] + cs3[:, LANES:2 * LANES] + cs3[:, 2 * LANES:] + carry)
        carry = cums[-1][CHUNK - 1:CHUNK, :]
    car_ref[...] = carry
    cum = jnp.concatenate(cums, axis=0)
    feats = jnp.dot(_pieces(cum * LOG2E), place_ref[...], preferred_element_type=F32) + ones_ref[...]
    fq_ref[...] = feats[:, 0:LANES].astype(BF16)
    fk_ref[...] = feats[:, LANES:].astype(BF16)

    for c0 in range(0, n_main, tn):
        r0 = c0 if c0 < gap_at else c0 + gap
        acc = lax.dot_general(h, wt_ref[r0:r0 + tn, :], nt, preferred_element_type=F32)
        if c0 >= v_lo:
            ovt_ref[c0 - v_lo:c0 - v_lo + tn, :] = acc.T.astype(BF16)
            continue
        if c0 < z_hi:
            acc = _silu(acc)
        elif q_lo <= c0 < q_hi:
            acc = acc * q_scale
        o_ref[:, c0:c0 + tn] = acc.astype(BF16)


def _inproj(x2, mod4, wt, wt_small, brow, batch, seq, n_main, gap_at, gap, z_hi, q_lo, q_hi, v_lo,
            q_scale, tm=512, tn=512):
    n_tok, d = x2.shape
    assert all(c % tn == 0 for c in (z_hi, q_lo, q_hi, v_lo, n_main, gap_at))
    assert gap % BF16_SUBLANES == 0
    tiles_per_seq = seq // tm
    place, ones_row = _bias_routing()
    const = lambda i: (0, 0)
    tok = lambda i: (i, 0)
    resident = functools.partial(pl.BlockSpec, index_map=const, pipeline_mode=pl.Buffered(1))
    return pl.pallas_call(
        functools.partial(_inproj_kernel, tn=tn, tiles_per_seq=tiles_per_seq, n_main=n_main,
                          gap_at=gap_at, gap=gap, z_hi=z_hi, q_lo=q_lo, q_hi=q_hi, q_scale=q_scale),
        grid=(n_tok // tm,),
        in_specs=[pl.BlockSpec((tm, d), tok),
                  pl.BlockSpec((None, 6, 1, d), lambda i: (i // tiles_per_seq, 0, 0, 0)),
                  resident(wt.shape),
                  resident((LANES, d)),
                  pl.BlockSpec((1, LANES), const),
                  pl.BlockSpec(place.shape, const),
                  pl.BlockSpec(ones_row.shape, const)],
        out_specs=[pl.BlockSpec((tm, v_lo), tok),
                   pl.BlockSpec((tm, LANES), tok),
                   pl.BlockSpec((None, n_main - v_lo, tm),
                                lambda i: (i // tiles_per_seq, 0, i % tiles_per_seq)),
                   pl.BlockSpec((tm, LANES), tok),
                   pl.BlockSpec((tm, LANES), tok)],
        out_shape=[jax.ShapeDtypeStruct((n_tok, v_lo), BF16),
                   jax.ShapeDtypeStruct((n_tok, LANES), F32),
                   jax.ShapeDtypeStruct((batch, n_main - v_lo, seq), BF16),
                   jax.ShapeDtypeStruct((n_tok, LANES), BF16),
                   jax.ShapeDtypeStruct((n_tok, LANES), BF16)],
        scratch_shapes=[pltpu.VMEM((1, LANES), F32)],
        compiler_params=pltpu.CompilerParams(
            dimension_semantics=("arbitrary",), vmem_limit_bytes=VMEM_LIMIT_BYTES),
        name="inproj",
    )(x2, mod4, wt, wt_small, brow, place, ones_row)


def _ssd_rows(gate, conv_silu, small, seq_start, brow_ref, alog_ref, dsk_ref, nw_ref, st_ref):
    width = SSM_HEADS * SSM_HEAD_DIM
    gw = HEADS_PER_GROUP * SSM_HEAD_DIM
    u = conv_silu()
    xs = u[:, 0:width]
    b_all = u[:, width:width + SSM_GROUPS * SSM_STATE]
    c_all = u[:, width + SSM_GROUPS * SSM_STATE:]

    lane = lax.broadcasted_iota(jnp.int32, (CHUNK, LANES), 1)
    row = lax.broadcasted_iota(jnp.int32, (CHUNK, LANES), 0)
    is_dt = lane < SMALL_F
    sp = _softplus(small + brow_ref[...])
    a_row = -jnp.exp(alog_ref[...])
    dt_tile = jnp.where(is_dt, sp, 0.0)
    val = jnp.where(is_dt, sp * a_row, 0.0)
    tril = row >= lane
    cs3 = jnp.dot(tril.astype(BF16), _pieces(val), preferred_element_type=F32)
    yield
    cs = cs3[:, 0:LANES] + cs3[:, LANES:2 * LANES] + cs3[:, 2 * LANES:]

    cs_t = cs.T
    dt_t = dt_tile.T
    lane_lo = lane < SSM_HEAD_DIM

    def per_head_lanes(tile):
        cols = [jnp.broadcast_to(tile[:, SMALL_DT + h:SMALL_DT + h + 1], (CHUNK, LANES))
                for h in range(SSM_HEADS)]
        blocks = [jnp.where(lane_lo, cols[h], cols[h + 1]) for h in range(0, SSM_HEADS, 2)]
        return cols, jnp.concatenate(blocks, axis=1)

    cs_cols, acs_full = per_head_lanes(cs)
    _, dt_full = per_head_lanes(dt_tile)
    atot_full = acs_full[CHUNK - 1:CHUNK, :]
    dec_out = jnp.exp(acs_full)
    dec_end = jnp.exp(atot_full - acs_full)
    chunk_dec = jnp.exp(atot_full)
    xs_b = xs.astype(BF16)
    xw = (xs * (dec_end * dt_full)).astype(BF16)

    bgs = [b_all[:, g * SSM_STATE:(g + 1) * SSM_STATE] for g in range(SSM_GROUPS)]
    cgs = [c_all[:, g * SSM_STATE:(g + 1) * SSM_STATE].astype(BF16) for g in range(SSM_GROUPS)]
    cbms = [lax.dot_general(cg, bg.astype(BF16), (((1,), (1,)), ((), ())),
                            preferred_element_type=F32) for cg, bg in zip(cgs, bgs)]
    yield
    ydiag_blocks = []
    yoff_blocks = []
    for g in range(SSM_GROUPS):
        bg, cg, cbm = bgs[g], cgs[g], cbms[g]
        for j in range(HEADS_PER_GROUP // HEADS_PER_BLOCK):
            h0 = g * HEADS_PER_GROUP + HEADS_PER_BLOCK * j
            xp = xs_b[:, h0 * SSM_HEAD_DIM:h0 * SSM_HEAD_DIM + LANES]
            scs = []
            for h in range(h0, h0 + HEADS_PER_BLOCK):
                seg = cs_cols[h] - cs_t[h:h + 1, :]
                dec = jnp.exp(jnp.where(tril, seg, -1e30))
                scs.append((cbm * dec * dt_t[h:h + 1, :]).astype(BF16))
            zero = jnp.zeros_like(xp)
            xp2 = jnp.concatenate([jnp.where(lane_lo, xp, zero), jnp.where(lane_lo, zero, xp)], axis=0)
            ydiag_blocks.append(jnp.dot(jnp.concatenate(scs, axis=1), xp2,
                                        preferred_element_type=F32))
        st = st_ref[g]
        if seq_start is not None:
            st = jnp.where(seq_start, 0.0, st)
        yoff_blocks.append(jnp.dot(cg, st.astype(BF16), preferred_element_type=F32)
                           * dec_out[:, g * gw:(g + 1) * gw])
        st_ref[g] = st * chunk_dec[:, g * gw:(g + 1) * gw] + jnp.dot(
            bg.T.astype(BF16), xw[:, g * gw:(g + 1) * gw], preferred_element_type=F32)
    y = (jnp.concatenate(ydiag_blocks, axis=1) + jnp.concatenate(yoff_blocks, axis=1)
         + dsk_ref[...] * xs)
    gated = y * gate.astype(F32)
    ms = jnp.mean(gated * gated, axis=-1, keepdims=True)
    yield (gated * lax.rsqrt(ms + RMS_EPS) * nw_ref[...]).astype(BF16)


def _bias_routing():
    place = np.zeros((BIAS_PIECES * LANES, 2 * LANES), np.float32)
    ones_row = np.zeros((1, 2 * LANES), np.float32)
    for h in range(ATT_HEADS):
        base = BIAS_LANES_PER_HEAD * h
        for t in range(BIAS_PIECES):
            place[t * LANES + SMALL_F + h, base + t] = 1.0
            place[t * LANES + SMALL_F + h, LANES + base + BIAS_PIECES + t] = -1.0
            ones_row[0, base + BIAS_PIECES + t] = 1.0
            ones_row[0, LANES + base + t] = 1.0
    return jnp.asarray(place, BF16), jnp.asarray(ones_row, F32)


ATT_TK = 256
ATT_SUB = 256
ATT_UNROLL = 4
ATT_VT_ROWS = ATT_HEAD_DIM + BF16_SUBLANES


def _attn_kernel(q_ref, k_ref, vt_ref, fq_ref, fk_ref, o_ref, st_ref, mx_ref, acc_ref, m_ref, *, tq):
    pair = pl.program_id(1)
    qi = pl.program_id(2)
    n_sub = tq // ATT_SUB
    assert n_sub % ATT_UNROLL == 0
    hd = ATT_HEAD_DIM
    ones_rows = jnp.ones((ATT_VT_ROWS - hd, ATT_TK), BF16)

    lane = lax.broadcasted_iota(jnp.int32, (tq, LANES), 1)
    q = q_ref[...]
    fq = fq_ref[...]
    q_aug = []
    for i in range(HEADS_PER_BLOCK):
        first = BIAS_LANES_PER_HEAD * (HEADS_PER_BLOCK * pair + i)
        own_q = (lane < hd) if i == 0 else (lane >= hd)
        own_f = jnp.logical_and(lane >= first, lane < first + BIAS_LANES_PER_HEAD)
        q_aug.append(jnp.concatenate([jnp.where(own_q, q, jnp.zeros_like(q)),
                                      jnp.where(own_f, fq, jnp.zeros_like(fq))], axis=1))
    causal = (lax.broadcasted_iota(jnp.int32, (ATT_TK, ATT_SUB), 0)
              <= lax.broadcasted_iota(jnp.int32, (ATT_TK, ATT_SUB), 1))
    chains = [(i, s) for i in range(HEADS_PER_BLOCK) for s in range(n_sub)]

    def head_scores(i, block, s_from=0):
        rows = pl.ds(pl.multiple_of(block * ATT_TK, ATT_TK), ATT_TK)
        kb = jnp.concatenate([k_ref[rows, :], fk_ref[rows, :]], axis=1)
        qa = q_aug[i][s_from * ATT_SUB:, :]
        return lax.dot_general(kb, qa, (((1,), (1,)), ((), ())), preferred_element_type=F32)

    def update(c, block, st, st_max):
        m = m_ref[c]
        m_new = jnp.maximum(m, st_max)
        pt = jnp.exp2(st - m_new).astype(BF16)
        start = pl.multiple_of(block * ATT_TK, ATT_TK)
        i = chains[c][0]
        vt = jnp.concatenate([vt_ref[i * hd:(i + 1) * hd, pl.ds(start, ATT_TK)], ones_rows], axis=0)
        acc_ref[c] = (jnp.exp2(m - m_new) * acc_ref[c]
                      + jnp.dot(vt, pt, preferred_element_type=F32))
        m_ref[c] = m_new

    def stash(slot, block):
        for i in range(HEADS_PER_BLOCK):
            st = head_scores(i, block)
            for s in range(n_sub):
                part = st[:, s * ATT_SUB:(s + 1) * ATT_SUB]
                st_ref[slot, i * n_sub + s] = part
                mx_ref[slot, i * n_sub + s] = jnp.max(part, axis=0, keepdims=True)

    every = list(range(len(chains)))
    for c in every:
        m_ref[c] = jnp.full((1, ATT_SUB), -1e30, F32)
        acc_ref[c] = jnp.zeros((ATT_VT_ROWS, ATT_SUB), F32)

    first_diag = qi * n_sub
    diag = []
    for e in range(n_sub):
        for i in range(HEADS_PER_BLOCK):
            st = head_scores(i, first_diag + e, s_from=e)
            for s in range(e, n_sub):
                part = st[:, (s - e) * ATT_SUB:(s - e + 1) * ATT_SUB]
                if s == e:
                    part = jnp.where(causal, part, -1e30)
                diag.append((i * n_sub + s, first_diag + e, part))
    stash(0, 0)
    for c, block, part in diag:
        update(c, block, part, jnp.max(part, axis=0, keepdims=True))

    def half_step(block, slot, prefetch=True):
        if prefetch:
            stash(1 - slot, block + 1)
        for c in every:
            update(c, block, st_ref[slot, c], mx_ref[slot, c])

    def body(j, carry):
        for t in range(ATT_UNROLL):
            half_step(ATT_UNROLL * j + t, t % 2)
        return carry

    lax.fori_loop(0, qi * (n_sub // ATT_UNROLL) - 1, body, 0)

    @pl.when(qi > 0)
    def _():
        for t in range(ATT_UNROLL):
            half_step(first_diag - ATT_UNROLL + t, t % 2, prefetch=t < ATT_UNROLL - 1)

    heads = [[None] * n_sub for _ in range(HEADS_PER_BLOCK)]
    for c, (i, s) in enumerate(chains):
        acc = acc_ref[c]
        heads[i][s] = acc[0:hd, :] * (1.0 / acc[hd:hd + 1, :])
    o_t = jnp.concatenate([jnp.concatenate(h, axis=1) for h in heads], axis=0)
    o_ref[...] = o_t.astype(BF16)


def _attn(proj, v_t, feat_q, feat_k, batch, seq, q_col, k_col, tq=2048):
    n_tok = proj.shape[0]
    nq = seq // tq
    n_pairs = ATT_HEADS // HEADS_PER_BLOCK
    n_chains = HEADS_PER_BLOCK * (tq // ATT_SUB)
    return pl.pallas_call(
        functools.partial(_attn_kernel, tq=tq),
        grid=(batch, n_pairs, nq),
        in_specs=[pl.BlockSpec((tq, LANES), lambda b, p, i: (b * nq + i, q_col + p)),
                  pl.BlockSpec((seq, LANES), lambda b, p, i: (b, k_col + p)),
                  pl.BlockSpec((None, LANES, seq), lambda b, p, i: (b, p, 0)),
                  pl.BlockSpec((tq, LANES), lambda b, p, i: (b * nq + i, 0)),
                  pl.BlockSpec((seq, LANES), lambda b, p, i: (b, 0))],
        out_specs=pl.BlockSpec((None, LANES, tq), lambda b, p, i: (b, p, i)),
        out_shape=jax.ShapeDtypeStruct((batch, ATT_HEADS * ATT_HEAD_DIM, seq), BF16),
        scratch_shapes=[pltpu.VMEM((2, n_chains, ATT_TK, ATT_SUB), F32),
                        pltpu.VMEM((2, n_chains, 1, ATT_SUB), F32),
                        pltpu.VMEM((n_chains, ATT_VT_ROWS, ATT_SUB), F32),
                        pltpu.VMEM((n_chains, 1, ATT_SUB), F32)],
        compiler_params=pltpu.CompilerParams(
            dimension_semantics=("arbitrary", "arbitrary", "arbitrary"),
            vmem_limit_bytes=VMEM_LIMIT_BYTES),
        name="attn",
    )(proj, proj, v_t, feat_q, feat_k)


def _mix_tail_kernel(x_ref, ya_ref, mod_ref, wo_ref, anw_ref, g1_ref, b1_ref, w1_ref, w2_ref, g2_ref,
                     b2_ref, gate_ref, xs_ref, bc_ref, sm_ref, xh_ref, bch_ref, cw_ref, cb_ref,
                     brow_ref, alog_ref, dsk_ref, nw_ref, o_ref, ys_ref, st_ref, xf_ref, *,
                     alpha, ff_chunk, tiles_per_seq, n_tiles):
    i = pl.program_id(0)
    tm = x_ref.shape[0]
    slot = i % 2

    @pl.when(i == 0)
    def _():
        ys_ref[...] = jnp.zeros(ys_ref.shape, BF16)
        st_ref[...] = jnp.zeros(st_ref.shape, F32)

    seq_start = jnp.minimum(i, n_tiles - 1) % tiles_per_seq == 0
    width = xs_ref.shape[1]
    halo = xh_ref.shape[0]
    before = jnp.concatenate([xh_ref[...], bch_ref[...]], axis=1).astype(F32)
    xf_ref[0:halo, :] = jnp.where(seq_start, 0.0, before)
    xf_ref[halo:, 0:width] = xs_ref[...].astype(F32)
    xf_ref[halo:, width:] = bc_ref[...].astype(F32)

    def conv_silu(c):
        conv = cb_ref[...]
        for k in range(CONV_WIDTH):
            first = halo + c * CHUNK - k
            conv = conv + (cw_ref[CONV_WIDTH - 1 - k:CONV_WIDTH - k, :]
                           * xf_ref[first:first + CHUNK, :])
        return _silu(conv)

    n_chunks = tm // CHUNK
    chunk_rows = [slice(c * CHUNK, (c + 1) * CHUNK) for c in range(n_chunks)]
    ssd = [_ssd_rows(gate_ref[rows, :], functools.partial(conv_silu, c), sm_ref[rows, :],
                     seq_start if c == 0 else None, brow_ref, alog_ref, dsk_ref, nw_ref, st_ref)
           for c, rows in enumerate(chunk_rows)]

    def advance(c, last=False):
        if 0 <= c < n_chunks:
            out = next(ssd[c])
            if last:
                ys_ref[slot, chunk_rows[c], :] = out

    advance(0)

    ssm_w = ys_ref.shape[2]
    ya = ya_ref[...].astype(F32).T
    ya = ya * lax.rsqrt(jnp.mean(ya * ya, axis=-1, keepdims=True) + RMS_EPS) * anw_ref[...]
    y = (jnp.dot(ys_ref[1 - slot], wo_ref[0:ssm_w, :], preferred_element_type=F32)
         + jnp.dot(ya.astype(BF16), wo_ref[ssm_w:, :], preferred_element_type=F32))
    x1 = _layer_norm(alpha * x_ref[...] + (1.0 + mod_ref[2]) * y, g1_ref[...], b1_ref[...])
    h = (x1 * (1.0 + mod_ref[4]) + mod_ref[3]).astype(BF16)
    advance(0)

    n_ff = w1_ref.shape[1] // ff_chunk
    assert n_chunks == n_ff
    ff = jnp.zeros(x1.shape, F32)
    for c in range(n_ff):
        a = jnp.dot(h, w1_ref[:, c * ff_chunk:(c + 1) * ff_chunk], preferred_element_type=F32)
        a = jnp.maximum(a, 0.0)
        advance(c, last=True)
        advance(c + 1)
        ff = ff + jnp.dot((a * a).astype(BF16), w2_ref[c * ff_chunk:(c + 1) * ff_chunk, :],
                          preferred_element_type=F32)
        advance(c + 1)
    o_ref[...] = _layer_norm(alpha * x1 + (1.0 + mod_ref[5]) * ff, g2_ref[...], b2_ref[...])


def _mix_tail(x2, proj, small, y_att_t, mod4, w_out, anw, g1, b1, w1, w2, g2, b2, conv_w, conv_b,
              brow, alog_row, dsk_full, ssm_norm_w, seq, alpha, tm=512, ff_chunk=1024):
    n_tok, d = x2.shape
    mix = w_out.shape[0]
    d_ff = w1.shape[1]
    width = SSM_HEADS * SSM_HEAD_DIM
    bc_w = 2 * SSM_GROUPS * SSM_STATE
    x_col, bc_col = 1, 2 * width // bc_w
    tiles_per_seq = seq // tm
    n_tiles = n_tok // tm
    const = lambda i: (0, 0)
    resident = functools.partial(pl.BlockSpec, index_map=const, pipeline_mode=pl.Buffered(1))
    mlp_tile = lambda i: jnp.maximum(i - 1, 0)
    ssd_tile = lambda i: jnp.minimum(i, n_tiles - 1)
    halo = lambda i: jnp.maximum(ssd_tile(i) * (tm // BF16_SUBLANES) - 1, 0)
    return pl.pallas_call(
        functools.partial(_mix_tail_kernel, alpha=alpha, ff_chunk=ff_chunk,
                          tiles_per_seq=tiles_per_seq, n_tiles=n_tiles),
        grid=(n_tiles + 1,),
        in_specs=[pl.BlockSpec((tm, d), lambda i: (mlp_tile(i), 0)),
                  pl.BlockSpec((None, y_att_t.shape[1], tm),
                               lambda i: (mlp_tile(i) // tiles_per_seq, 0,
                                          mlp_tile(i) % tiles_per_seq)),
                  pl.BlockSpec((None, 6, 1, d), lambda i: (mlp_tile(i) // tiles_per_seq, 0, 0, 0)),
                  resident((mix, d)),
                  pl.BlockSpec((1, y_att_t.shape[1]), const),
                  pl.BlockSpec((1, d), const),
                  pl.BlockSpec((1, d), const),
                  resident((d, d_ff)),
                  resident((d_ff, d)),
                  pl.BlockSpec((1, d), const),
                  pl.BlockSpec((1, d), const),
                  pl.BlockSpec((tm, width), lambda i: (ssd_tile(i), 0)),
                  pl.BlockSpec((tm, width), lambda i: (ssd_tile(i), x_col)),
                  pl.BlockSpec((tm, bc_w), lambda i: (ssd_tile(i), bc_col)),
                  pl.BlockSpec((tm, LANES), lambda i: (ssd_tile(i), 0)),
                  pl.BlockSpec((BF16_SUBLANES, width), lambda i: (halo(i), x_col)),
                  pl.BlockSpec((BF16_SUBLANES, bc_w), lambda i: (halo(i), bc_col)),
                  pl.BlockSpec(conv_w.shape, const),
                  pl.BlockSpec(conv_b.shape, const),
                  pl.BlockSpec((1, LANES), const),
                  pl.BlockSpec((1, LANES), const),
                  pl.BlockSpec((1, width), const),
                  pl.BlockSpec((1, width), const)],
        out_specs=pl.BlockSpec((tm, d), lambda i: (mlp_tile(i), 0)),
        out_shape=jax.ShapeDtypeStruct((n_tok, d), F32),
        scratch_shapes=[pltpu.VMEM((2, tm, width), BF16),
                        pltpu.VMEM((SSM_GROUPS, SSM_STATE, HEADS_PER_GROUP * SSM_HEAD_DIM), F32),
                        pltpu.VMEM((BF16_SUBLANES + tm, width + bc_w), F32)],
        compiler_params=pltpu.CompilerParams(
            dimension_semantics=("arbitrary",), vmem_limit_bytes=VMEM_LIMIT_BYTES),
        name="mix_tail",
    )(x2, y_att_t, mod4, w_out, anw, g1, b1, w1, w2, g2, b2, proj, proj, proj, small, proj, proj,
      conv_w, conv_b, brow, alog_row, dsk_full, ssm_norm_w)


def _pad_lanes(v):
    return jnp.pad(v, ((0, 0), (0, LANES - v.shape[1])))


def kernel(x, c, w_ada, b_ada, w_in, conv_w, conv_b, dt_bias, a_log, d_skip, ssm_norm_w, f_bias,
           attn_norm_w, w_out, ln1_g, ln1_b, w_ff_in, w_ff_out, ln2_g, ln2_b):
    batch, seq, d = x.shape
    depth = w_ada.shape[0]
    alpha = (2.0 * depth) ** 0.25
    ssm_w = SSM_HEADS * SSM_HEAD_DIM
    att_w = ATT_HEADS * ATT_HEAD_DIM
    conv_dim = ssm_w + 2 * SSM_GROUPS * SSM_STATE
    o_xbc = ssm_w
    o_dt = o_xbc + conv_dim
    o_q = o_dt + SSM_HEADS
    o_k = o_q + att_w
    o_v = o_k + att_w
    o_f = o_v + att_w
    q_col = (ssm_w + conv_dim) // LANES
    k_col = q_col + att_w // LANES
    v_col = k_col + att_w // LANES

    x2 = x.reshape(batch * seq, d)
    c_pad = jnp.pad(c, ((0, SUBLANES - batch % SUBLANES if batch % SUBLANES else 0), (0, 0)))
    for l in range(depth):
        mod = _ada(c_pad, w_ada[l], b_ada[l][None, :])
        mod4 = mod[:batch].reshape(batch, 6, 1, d)
        wt = jnp.swapaxes(w_in[l], 0, 1).astype(BF16)
        wt_small = jnp.concatenate([wt[o_dt:o_q], wt[o_f:]], axis=0)
        wt_small = jnp.pad(wt_small, ((0, LANES - wt_small.shape[0]), (0, 0)))
        brow = _pad_lanes(jnp.concatenate([dt_bias[l], f_bias[l]])[None, :])
        proj, small, v_t, feat_q, feat_k = _inproj(
            x2, mod4, wt, wt_small, brow, batch, seq, v_col * LANES + att_w, o_dt, o_q - o_dt,
            ssm_w, q_col * LANES, k_col * LANES, v_col * LANES, ATT_HEAD_DIM ** -0.5 * LOG2E)
        y_att_t = _attn(proj, v_t, feat_q, feat_k, batch, seq, q_col, k_col)

        alog_row = _pad_lanes(a_log[l][None, :])
        dsk_full = jnp.repeat(d_skip[l], SSM_HEAD_DIM)[None, :]
        x2 = _mix_tail(x2, proj, small, y_att_t, mod4, w_out[l].astype(BF16),
                       attn_norm_w[l][None, :], ln1_g[l][None, :], ln1_b[l][None, :],
                       w_ff_in[l].astype(BF16), w_ff_out[l].astype(BF16), ln2_g[l][None, :],
                       ln2_b[l][None, :], conv_w[l], conv_b[l][None, :], brow, alog_row, dsk_full,
                       ssm_norm_w[l][None, :], seq, alpha)
    return x2.reshape(batch, seq, d)
```

```python
import functools

import jax
import jax.numpy as jnp
import numpy as np
from jax import lax
from jax.experimental import pallas as pl
from jax.experimental.pallas import tpu as pltpu

F32 = jnp.float32
BF16 = jnp.bfloat16

LANES = 128
SUBLANES = 8
BF16_SUBLANES = 16
VMEM_LIMIT_BYTES = 56 * 1024 * 1024

SSM_HEADS = 16
SSM_HEAD_DIM = 64
SSM_GROUPS = 2
SSM_STATE = 128
CONV_WIDTH = 4
CHUNK = 128
ATT_HEADS = 16
ATT_HEAD_DIM = 64
LN_EPS = 1e-5
RMS_EPS = 1e-5

HEADS_PER_GROUP = SSM_HEADS // SSM_GROUPS
HEADS_PER_BLOCK = LANES // ATT_HEAD_DIM
SMALL_DT = 0
SMALL_F = SSM_HEADS
LOG2E = 1.4426950408889634
BIAS_PIECES = 3
BIAS_LANES_PER_HEAD = 2 * BIAS_PIECES


def _silu(v):
    return v * (1.0 / (1.0 + jnp.exp(-v)))


def _softplus(v):
    return jnp.maximum(v, 0.0) + jnp.log(1.0 + jnp.exp(-jnp.abs(v)))


def _split3(c):
    hi = c.astype(BF16).astype(F32)
    r = c - hi
    mid = r.astype(BF16).astype(F32)
    return hi, mid, r - mid


def _pieces(c):
    return jnp.concatenate(_split3(c), axis=1).astype(BF16)


def _layer_norm(v, g, b):
    mu = jnp.mean(v, axis=-1, keepdims=True)
    d = v - mu
    var = jnp.mean(d * d, axis=-1, keepdims=True)
    return d * lax.rsqrt(var + LN_EPS) * g + b


def _ada_kernel(c_ref, w_ref, b_ref, o_ref):
    ca = _silu(c_ref[...]).astype(BF16)
    o_ref[...] = jnp.dot(ca, w_ref[...].astype(BF16), preferred_element_type=F32) + b_ref[...]


def _ada(c_pad, w, b, tn=2048):
    rows, d = c_pad.shape
    n = w.shape[1]
    return pl.pallas_call(
        _ada_kernel,
        grid=(n // tn,),
        in_specs=[pl.BlockSpec((rows, d), lambda j: (0, 0)),
                  pl.BlockSpec((d, tn), lambda j: (0, j)),
                  pl.BlockSpec((1, tn), lambda j: (0, j))],
        out_specs=pl.BlockSpec((rows, tn), lambda j: (0, j)),
        out_shape=jax.ShapeDtypeStruct((rows, n), F32),
        compiler_params=pltpu.CompilerParams(
            dimension_semantics=("arbitrary",), vmem_limit_bytes=VMEM_LIMIT_BYTES),
        name="ada",
    )(c_pad, w, b)


def _inproj_kernel(x_ref, mod_ref, wt_ref, wst_ref, brow_ref, place_ref, ones_ref, o_ref, os_ref,
                   oqt_ref, ovt_ref, fq_ref, fk_ref, car_ref, *, tn, tiles_per_seq, n_main, gap_at,
                   gap, z_hi, q_lo, q_hi, v_lo, q_scale):
    nt = (((1,), (1,)), ((), ()))
    tm = x_ref.shape[0]

    @pl.when(pl.program_id(0) == 0)
    def _():
        car_ref[...] = jnp.zeros(car_ref.shape, F32)

    h = (x_ref[...] * (1.0 + mod_ref[1]) + mod_ref[0]).astype(BF16)
    small = lax.dot_general(h, wst_ref[...], nt, preferred_element_type=F32)
    os_ref[...] = small

    lane = lax.broadcasted_iota(jnp.int32, (tm, LANES), 1)
    is_f = jnp.logical_and(lane >= SMALL_F, lane < SMALL_F + ATT_HEADS)
    log_f = jnp.where(is_f, -_softplus(-(small + brow_ref[...])), 0.0)
    tril = (lax.broadcasted_iota(jnp.int32, (CHUNK, CHUNK), 0)
            >= lax.broadcasted_iota(jnp.int32, (CHUNK, CHUNK), 1)).astype(BF16)
    seq_start = pl.program_id(0) % tiles_per_seq == 0
    carry = jnp.where(seq_start, 0.0, car_ref[...])
    cums = []
    for r0 in range(0, tm, CHUNK):
        cs3 = jnp.dot(tril, _pieces(log_f[r0:r0 + CHUNK, :]), preferred_element_type=F32)
        cums.append(cs3[:, 0:LANES] + cs3[:, LANES:2 * LANES] + cs3[:, 2 * LANES:] + carry)
        carry = cums[-1][CHUNK - 1:CHUNK, :]
    car_ref[...] = carry
    cum = jnp.concatenate(cums, axis=0)
    feats = jnp.dot(_pieces(cum * LOG2E), place_ref[...], preferred_element_type=F32) + ones_ref[...]
    fq_ref[...] = feats[:, 0:LANES].T.astype(BF16)
    fk_ref[...] = feats[:, LANES:].astype(BF16)

    for c0 in range(0, n_main, tn):
        r0 = c0 if c0 < gap_at else c0 + gap
        acc = lax.dot_general(h, wt_ref[r0:r0 + tn, :], nt, preferred_element_type=F32)
        if c0 >= v_lo:
            ovt_ref[c0 - v_lo:c0 - v_lo + tn, :] = acc.T.astype(BF16)
        elif c0 >= q_hi:
            o_ref[:, c0 - (q_hi - q_lo):c0 - (q_hi - q_lo) + tn] = acc.astype(BF16)
        elif c0 >= q_lo:
            oqt_ref[c0 - q_lo:c0 - q_lo + tn, :] = (acc * q_scale).T.astype(BF16)
        else:
            o_ref[:, c0:c0 + tn] = (_silu(acc) if c0 < z_hi else acc).astype(BF16)


def _inproj(x2, mod4, wt, wt_small, brow, batch, seq, n_main, gap_at, gap, z_hi, q_lo, q_hi, v_lo,
            q_scale, tm=512, tn=512):
    n_tok, d = x2.shape
    assert all(c % tn == 0 for c in (z_hi, q_lo, q_hi, v_lo, n_main, gap_at))
    assert gap % BF16_SUBLANES == 0
    tiles_per_seq = seq // tm
    place, ones_row = _bias_routing()
    const = lambda i: (0, 0)
    tok = lambda i: (i, 0)
    resident = functools.partial(pl.BlockSpec, index_map=const, pipeline_mode=pl.Buffered(1))
    n_rowmajor = v_lo - (q_hi - q_lo)
    by_channel = lambda i: (i // tiles_per_seq, 0, i % tiles_per_seq)
    return pl.pallas_call(
        functools.partial(_inproj_kernel, tn=tn, tiles_per_seq=tiles_per_seq, n_main=n_main,
                          gap_at=gap_at, gap=gap, z_hi=z_hi, q_lo=q_lo, q_hi=q_hi, v_lo=v_lo,
                          q_scale=q_scale),
        grid=(n_tok // tm,),
        in_specs=[pl.BlockSpec((tm, d), tok),
                  pl.BlockSpec((None, 6, 1, d), lambda i: (i // tiles_per_seq, 0, 0, 0)),
                  resident(wt.shape),
                  resident((LANES, d)),
                  pl.BlockSpec((1, LANES), const),
                  pl.BlockSpec(place.shape, const),
                  pl.BlockSpec(ones_row.shape, const)],
        out_specs=[pl.BlockSpec((tm, n_rowmajor), tok),
                   pl.BlockSpec((tm, LANES), tok),
                   pl.BlockSpec((None, q_hi - q_lo, tm), by_channel),
                   pl.BlockSpec((None, n_main - v_lo, tm), by_channel),
                   pl.BlockSpec((None, LANES, tm), by_channel),
                   pl.BlockSpec((tm, LANES), tok)],
        out_shape=[jax.ShapeDtypeStruct((n_tok, n_rowmajor), BF16),
                   jax.ShapeDtypeStruct((n_tok, LANES), F32),
                   jax.ShapeDtypeStruct((batch, q_hi - q_lo, seq), BF16),
                   jax.ShapeDtypeStruct((batch, n_main - v_lo, seq), BF16),
                   jax.ShapeDtypeStruct((batch, LANES, seq), BF16),
                   jax.ShapeDtypeStruct((n_tok, LANES), BF16)],
        scratch_shapes=[pltpu.VMEM((1, LANES), F32)],
        compiler_params=pltpu.CompilerParams(
            dimension_semantics=("arbitrary",), vmem_limit_bytes=VMEM_LIMIT_BYTES),
        name="inproj",
    )(x2, mod4, wt, wt_small, brow, place, ones_row)


def _ssd_rows(gate, conv_silu, small, seq_start, brow_ref, alog_ref, dsk_ref, nw_ref, st_ref):
    width = SSM_HEADS * SSM_HEAD_DIM
    gw = HEADS_PER_GROUP * SSM_HEAD_DIM
    u = conv_silu()
    xs = u[:, 0:width]
    b_all = u[:, width:width + SSM_GROUPS * SSM_STATE]
    c_all = u[:, width + SSM_GROUPS * SSM_STATE:]

    lane = lax.broadcasted_iota(jnp.int32, (CHUNK, LANES), 1)
    row = lax.broadcasted_iota(jnp.int32, (CHUNK, LANES), 0)
    is_dt = lane < SMALL_F
    sp = _softplus(small + brow_ref[...])
    a_row = -jnp.exp(alog_ref[...])
    dt_tile = jnp.where(is_dt, sp, 0.0)
    val = jnp.where(is_dt, sp * a_row, 0.0)
    tril = row >= lane
    cs3 = jnp.dot(tril.astype(BF16), _pieces(val), preferred_element_type=F32)
    yield
    cs = cs3[:, 0:LANES] + cs3[:, LANES:2 * LANES] + cs3[:, 2 * LANES:]

    cs_t = cs.T
    dt_t = dt_tile.T
    lane_lo = lane < SSM_HEAD_DIM

    def per_head_lanes(tile):
        cols = [jnp.broadcast_to(tile[:, SMALL_DT + h:SMALL_DT + h + 1], (CHUNK, LANES))
                for h in range(SSM_HEADS)]
        blocks = [jnp.where(lane_lo, cols[h], cols[h + 1]) for h in range(0, SSM_HEADS, 2)]
        return cols, jnp.concatenate(blocks, axis=1)

    cs_cols, acs_full = per_head_lanes(cs)
    _, dt_full = per_head_lanes(dt_tile)
    atot_full = acs_full[CHUNK - 1:CHUNK, :]
    dec_out = jnp.exp(acs_full)
    dec_end = jnp.exp(atot_full - acs_full)
    chunk_dec = jnp.exp(atot_full)
    xs_b = xs.astype(BF16)
    xw = (xs * (dec_end * dt_full)).astype(BF16)

    bgs = [b_all[:, g * SSM_STATE:(g + 1) * SSM_STATE] for g in range(SSM_GROUPS)]
    cgs = [c_all[:, g * SSM_STATE:(g + 1) * SSM_STATE].astype(BF16) for g in range(SSM_GROUPS)]
    cbms = [lax.dot_general(cg, bg.astype(BF16), (((1,), (1,)), ((), ())),
                            preferred_element_type=F32) for cg, bg in zip(cgs, bgs)]
    yield
    ydiag_blocks = []
    yoff_blocks = []
    for g in range(SSM_GROUPS):
        bg, cg, cbm = bgs[g], cgs[g], cbms[g]
        for j in range(HEADS_PER_GROUP // HEADS_PER_BLOCK):
            h0 = g * HEADS_PER_GROUP + HEADS_PER_BLOCK * j
            xp = xs_b[:, h0 * SSM_HEAD_DIM:h0 * SSM_HEAD_DIM + LANES]
            scs = []
            for h in range(h0, h0 + HEADS_PER_BLOCK):
                seg = cs_cols[h] - cs_t[h:h + 1, :]
                dec = jnp.exp(jnp.where(tril, seg, -1e30))
                scs.append((cbm * dec * dt_t[h:h + 1, :]).astype(BF16))
            zero = jnp.zeros_like(xp)
            xp2 = jnp.concatenate([jnp.where(lane_lo, xp, zero), jnp.where(lane_lo, zero, xp)], axis=0)
            ydiag_blocks.append(jnp.dot(jnp.concatenate(scs, axis=1), xp2,
                                        preferred_element_type=F32))
        st = st_ref[g]
        if seq_start is not None:
            st = jnp.where(seq_start, 0.0, st)
        yoff_blocks.append(jnp.dot(cg, st.astype(BF16), preferred_element_type=F32)
                           * dec_out[:, g * gw:(g + 1) * gw])
        st_ref[g] = st * chunk_dec[:, g * gw:(g + 1) * gw] + jnp.dot(
            bg.T.astype(BF16), xw[:, g * gw:(g + 1) * gw], preferred_element_type=F32)
    y = (jnp.concatenate(ydiag_blocks, axis=1) + jnp.concatenate(yoff_blocks, axis=1)
         + dsk_ref[...] * xs)
    gated = y * gate.astype(F32)
    ms = jnp.mean(gated * gated, axis=-1, keepdims=True)
    yield (gated * lax.rsqrt(ms + RMS_EPS) * nw_ref[...]).astype(BF16)


def _bias_routing():
    place = np.zeros((BIAS_PIECES * LANES, 2 * LANES), np.float32)
    ones_row = np.zeros((1, 2 * LANES), np.float32)
    for h in range(ATT_HEADS):
        base = BIAS_LANES_PER_HEAD * h
        for t in range(BIAS_PIECES):
            place[t * LANES + SMALL_F + h, base + t] = 1.0
            place[t * LANES + SMALL_F + h, LANES + base + BIAS_PIECES + t] = -1.0
            ones_row[0, base + BIAS_PIECES + t] = 1.0
            ones_row[0, LANES + base + t] = 1.0
    return jnp.asarray(place, BF16), jnp.asarray(ones_row, F32)


ATT_TK = 256
ATT_SUB = 256
ATT_UNROLL = 4
ATT_VT_ROWS = ATT_HEAD_DIM + BF16_SUBLANES


def _attn_kernel(q_ref, k_ref, vt_ref, fq_ref, fk_ref, o_ref, st_ref, mx_ref, acc_ref, m_ref, *, tq):
    pair = pl.program_id(1)
    qi = pl.program_id(2)
    n_sub = tq // ATT_SUB
    assert n_sub % ATT_UNROLL == 0
    hd = ATT_HEAD_DIM
    ones_rows = jnp.ones((ATT_VT_ROWS - hd, ATT_TK), BF16)

    chan = lax.broadcasted_iota(jnp.int32, (LANES, tq), 0)
    q = q_ref[...]
    fq = fq_ref[...]
    q_aug = []
    for i in range(HEADS_PER_BLOCK):
        first = BIAS_LANES_PER_HEAD * (HEADS_PER_BLOCK * pair + i)
        own_q = (chan < hd) if i == 0 else (chan >= hd)
        own_f = jnp.logical_and(chan >= first, chan < first + BIAS_LANES_PER_HEAD)
        q_aug.append(jnp.concatenate([jnp.where(own_q, q, jnp.zeros_like(q)),
                                      jnp.where(own_f, fq, jnp.zeros_like(fq))], axis=0))
    causal = (lax.broadcasted_iota(jnp.int32, (ATT_TK, ATT_SUB), 0)
              <= lax.broadcasted_iota(jnp.int32, (ATT_TK, ATT_SUB), 1))
    chains = [(i, s) for i in range(HEADS_PER_BLOCK) for s in range(n_sub)]

    def head_scores(i, block, s_from=0):
        rows = pl.ds(pl.multiple_of(block * ATT_TK, ATT_TK), ATT_TK)
        kb = jnp.concatenate([k_ref[rows, :], fk_ref[rows, :]], axis=1)
        return jnp.dot(kb, q_aug[i][:, s_from * ATT_SUB:], preferred_element_type=F32)

    def update(c, block, st, st_max):
        m = m_ref[c]
        m_new = jnp.maximum(m, st_max)
        pt = jnp.exp2(st - m_new).astype(BF16)
        start = pl.multiple_of(block * ATT_TK, ATT_TK)
        i = chains[c][0]
        vt = jnp.concatenate([vt_ref[i * hd:(i + 1) * hd, pl.ds(start, ATT_TK)], ones_rows], axis=0)
        acc_ref[c] = (jnp.exp2(m - m_new) * acc_ref[c]
                      + jnp.dot(vt, pt, preferred_element_type=F32))
        m_ref[c] = m_new

    def stash(slot, block):
        for i in range(HEADS_PER_BLOCK):
            st = head_scores(i, block)
            for s in range(n_sub):
                part = st[:, s * ATT_SUB:(s + 1) * ATT_SUB]
                st_ref[slot, i * n_sub + s] = part
                mx_ref[slot, i * n_sub + s] = jnp.max(part, axis=0, keepdims=True)

    every = list(range(len(chains)))
    for c in every:
        m_ref[c] = jnp.full((1, ATT_SUB), -1e30, F32)
        acc_ref[c] = jnp.zeros((ATT_VT_ROWS, ATT_SUB), F32)

    first_diag = qi * n_sub
    diag = []
    for e in range(n_sub):
        for i in range(HEADS_PER_BLOCK):
            st = head_scores(i, first_diag + e, s_from=e)
            for s in range(e, n_sub):
                part = st[:, (s - e) * ATT_SUB:(s - e + 1) * ATT_SUB]
                if s == e:
                    part = jnp.where(causal, part, -1e30)
                diag.append((i * n_sub + s, first_diag + e, part))
    stash(0, 0)
    for c, block, part in diag:
        update(c, block, part, jnp.max(part, axis=0, keepdims=True))

    def half_step(block, slot, prefetch=True):
        if prefetch:
            stash(1 - slot, block + 1)
        for c in every:
            update(c, block, st_ref[slot, c], mx_ref[slot, c])

    def body(j, carry):
        for t in range(ATT_UNROLL):
            half_step(ATT_UNROLL * j + t, t % 2)
        return carry

    lax.fori_loop(0, qi * (n_sub // ATT_UNROLL) - 1, body, 0)

    @pl.when(qi > 0)
    def _():
        for t in range(ATT_UNROLL):
            half_step(first_diag - ATT_UNROLL + t, t % 2, prefetch=t < ATT_UNROLL - 1)

    heads = [[None] * n_sub for _ in range(HEADS_PER_BLOCK)]
    for c, (i, s) in enumerate(chains):
        acc = acc_ref[c]
        heads[i][s] = acc[0:hd, :] * (1.0 / acc[hd:hd + 1, :])
    o_t = jnp.concatenate([jnp.concatenate(h, axis=1) for h in heads], axis=0)
    o_ref[...] = o_t.astype(BF16)


def _attn(proj, q_t, v_t, feat_q_t, feat_k, batch, seq, k_col, tq=2048):
    nq = seq // tq
    n_pairs = ATT_HEADS // HEADS_PER_BLOCK
    n_chains = HEADS_PER_BLOCK * (tq // ATT_SUB)
    return pl.pallas_call(
        functools.partial(_attn_kernel, tq=tq),
        grid=(batch, n_pairs, nq),
        in_specs=[pl.BlockSpec((None, LANES, tq), lambda b, p, i: (b, p, i)),
                  pl.BlockSpec((seq, LANES), lambda b, p, i: (b, k_col + p)),
                  pl.BlockSpec((None, LANES, seq), lambda b, p, i: (b, p, 0)),
                  pl.BlockSpec((None, LANES, tq), lambda b, p, i: (b, 0, i)),
                  pl.BlockSpec((seq, LANES), lambda b, p, i: (b, 0))],
        out_specs=pl.BlockSpec((None, LANES, tq), lambda b, p, i: (b, p, i)),
        out_shape=jax.ShapeDtypeStruct((batch, ATT_HEADS * ATT_HEAD_DIM, seq), BF16),
        scratch_shapes=[pltpu.VMEM((2, n_chains, ATT_TK, ATT_SUB), F32),
                        pltpu.VMEM((2, n_chains, 1, ATT_SUB), F32),
                        pltpu.VMEM((n_chains, ATT_VT_ROWS, ATT_SUB), F32),
                        pltpu.VMEM((n_chains, 1, ATT_SUB), F32)],
        compiler_params=pltpu.CompilerParams(
            dimension_semantics=("arbitrary", "arbitrary", "arbitrary"),
            vmem_limit_bytes=VMEM_LIMIT_BYTES),
        name="attn",
    )(q_t, proj, v_t, feat_q_t, feat_k)


def _mix_tail_kernel(x_ref, ya_ref, mod_ref, wo_ref, anw_ref, g1_ref, b1_ref, w1_ref, w2_ref, g2_ref,
                     b2_ref, gate_ref, xs_ref, bc_ref, sm_ref, xh_ref, bch_ref, cw_ref, cb_ref,
                     brow_ref, alog_ref, dsk_ref, nw_ref, o_ref, ys_ref, st_ref, xf_ref, *,
                     alpha, ff_chunk, tiles_per_seq, n_tiles):
    i = pl.program_id(0)
    tm = x_ref.shape[0]
    slot = i % 2

    @pl.when(i == 0)
    def _():
        ys_ref[...] = jnp.zeros(ys_ref.shape, BF16)
        st_ref[...] = jnp.zeros(st_ref.shape, F32)

    seq_start = jnp.minimum(i, n_tiles - 1) % tiles_per_seq == 0
    width = xs_ref.shape[1]
    halo = xh_ref.shape[0]
    before = jnp.concatenate([xh_ref[...], bch_ref[...]], axis=1).astype(F32)
    xf_ref[0:halo, :] = jnp.where(seq_start, 0.0, before)
    xf_ref[halo:, 0:width] = xs_ref[...].astype(F32)
    xf_ref[halo:, width:] = bc_ref[...].astype(F32)

    def conv_silu(c):
        conv = cb_ref[...]
        for k in range(CONV_WIDTH):
            first = halo + c * CHUNK - k
            conv = conv + (cw_ref[CONV_WIDTH - 1 - k:CONV_WIDTH - k, :]
                           * xf_ref[first:first + CHUNK, :])
        return _silu(conv)

    n_chunks = tm // CHUNK
    chunk_rows = [slice(c * CHUNK, (c + 1) * CHUNK) for c in range(n_chunks)]
    ssd = [_ssd_rows(gate_ref[rows, :], functools.partial(conv_silu, c), sm_ref[rows, :],
                     seq_start if c == 0 else None, brow_ref, alog_ref, dsk_ref, nw_ref, st_ref)
           for c, rows in enumerate(chunk_rows)]

    def advance(c, last=False):
        if 0 <= c < n_chunks:
            out = next(ssd[c])
            if last:
                ys_ref[slot, chunk_rows[c], :] = out

    advance(0)

    ssm_w = ys_ref.shape[2]
    ya = ya_ref[...].astype(F32).T
    ya = ya * lax.rsqrt(jnp.mean(ya * ya, axis=-1, keepdims=True) + RMS_EPS) * anw_ref[...]
    y = (jnp.dot(ys_ref[1 - slot], wo_ref[0:ssm_w, :], preferred_element_type=F32)
         + jnp.dot(ya.astype(BF16), wo_ref[ssm_w:, :], preferred_element_type=F32))
    x1 = _layer_norm(alpha * x_ref[...] + (1.0 + mod_ref[2]) * y, g1_ref[...], b1_ref[...])
    h = (x1 * (1.0 + mod_ref[4]) + mod_ref[3]).astype(BF16)
    advance(0)

    n_ff = w1_ref.shape[1] // ff_chunk
    assert n_chunks == n_ff
    ff = jnp.zeros(x1.shape, F32)
    for c in range(n_ff):
        a = jnp.dot(h, w1_ref[:, c * ff_chunk:(c + 1) * ff_chunk], preferred_element_type=F32)
        a = jnp.maximum(a, 0.0)
        advance(c, last=True)
        advance(c + 1)
        ff = ff + jnp.dot((a * a).astype(BF16), w2_ref[c * ff_chunk:(c + 1) * ff_chunk, :],
                          preferred_element_type=F32)
        advance(c + 1)
    o_ref[...] = _layer_norm(alpha * x1 + (1.0 + mod_ref[5]) * ff, g2_ref[...], b2_ref[...])


def _mix_tail(x2, proj, small, y_att_t, mod4, w_out, anw, g1, b1, w1, w2, g2, b2, conv_w, conv_b,
              brow, alog_row, dsk_full, ssm_norm_w, seq, alpha, tm=512, ff_chunk=1024):
    n_tok, d = x2.shape
    mix = w_out.shape[0]
    d_ff = w1.shape[1]
    width = SSM_HEADS * SSM_HEAD_DIM
    bc_w = 2 * SSM_GROUPS * SSM_STATE
    x_col, bc_col = 1, 2 * width // bc_w
    tiles_per_seq = seq // tm
    n_tiles = n_tok // tm
    const = lambda i: (0, 0)
    resident = functools.partial(pl.BlockSpec, index_map=const, pipeline_mode=pl.Buffered(1))
    mlp_tile = lambda i: jnp.maximum(i - 1, 0)
    ssd_tile = lambda i: jnp.minimum(i, n_tiles - 1)
    halo = lambda i: jnp.maximum(ssd_tile(i) * (tm // BF16_SUBLANES) - 1, 0)
    return pl.pallas_call(
        functools.partial(_mix_tail_kernel, alpha=alpha, ff_chunk=ff_chunk,
                          tiles_per_seq=tiles_per_seq, n_tiles=n_tiles),
        grid=(n_tiles + 1,),
        in_specs=[pl.BlockSpec((tm, d), lambda i: (mlp_tile(i), 0)),
                  pl.BlockSpec((None, y_att_t.shape[1], tm),
                               lambda i: (mlp_tile(i) // tiles_per_seq, 0,
                                          mlp_tile(i) % tiles_per_seq)),
                  pl.BlockSpec((None, 6, 1, d), lambda i: (mlp_tile(i) // tiles_per_seq, 0, 0, 0)),
                  resident((mix, d)),
                  pl.BlockSpec((1, y_att_t.shape[1]), const),
                  pl.BlockSpec((1, d), const),
                  pl.BlockSpec((1, d), const),
                  resident((d, d_ff)),
                  resident((d_ff, d)),
                  pl.BlockSpec((1, d), const),
                  pl.BlockSpec((1, d), const),
                  pl.BlockSpec((tm, width), lambda i: (ssd_tile(i), 0)),
                  pl.BlockSpec((tm, width), lambda i: (ssd_tile(i), x_col)),
                  pl.BlockSpec((tm, bc_w), lambda i: (ssd_tile(i), bc_col)),
                  pl.BlockSpec((tm, LANES), lambda i: (ssd_tile(i), 0)),
                  pl.BlockSpec((BF16_SUBLANES, width), lambda i: (halo(i), x_col)),
                  pl.BlockSpec((BF16_SUBLANES, bc_w), lambda i: (halo(i), bc_col)),
                  pl.BlockSpec(conv_w.shape, const),
                  pl.BlockSpec(conv_b.shape, const),
                  pl.BlockSpec((1, LANES), const),
                  pl.BlockSpec((1, LANES), const),
                  pl.BlockSpec((1, width), const),
                  pl.BlockSpec((1, width), const)],
        out_specs=pl.BlockSpec((tm, d), lambda i: (mlp_tile(i), 0)),
        out_shape=jax.ShapeDtypeStruct((n_tok, d), F32),
        scratch_shapes=[pltpu.VMEM((2, tm, width), BF16),
                        pltpu.VMEM((SSM_GROUPS, SSM_STATE, HEADS_PER_GROUP * SSM_HEAD_DIM), F32),
                        pltpu.VMEM((BF16_SUBLANES + tm, width + bc_w), F32)],
        compiler_params=pltpu.CompilerParams(
            dimension_semantics=("arbitrary",), vmem_limit_bytes=VMEM_LIMIT_BYTES),
        name="mix_tail",
    )(x2, y_att_t, mod4, w_out, anw, g1, b1, w1, w2, g2, b2, proj, proj, proj, small, proj, proj,
      conv_w, conv_b, brow, alog_row, dsk_full, ssm_norm_w)


def _pad_lanes(v):
    return jnp.pad(v, ((0, 0), (0, LANES - v.shape[1])))


def kernel(x, c, w_ada, b_ada, w_in, conv_w, conv_b, dt_bias, a_log, d_skip, ssm_norm_w, f_bias,
           attn_norm_w, w_out, ln1_g, ln1_b, w_ff_in, w_ff_out, ln2_g, ln2_b):
    batch, seq, d = x.shape
    depth = w_ada.shape[0]
    alpha = (2.0 * depth) ** 0.25
    ssm_w = SSM_HEADS * SSM_HEAD_DIM
    att_w = ATT_HEADS * ATT_HEAD_DIM
    conv_dim = ssm_w + 2 * SSM_GROUPS * SSM_STATE
    o_xbc = ssm_w
    o_dt = o_xbc + conv_dim
    o_q = o_dt + SSM_HEADS
    o_k = o_q + att_w
    o_v = o_k + att_w
    o_f = o_v + att_w
    q_col = (ssm_w + conv_dim) // LANES
    k_col = q_col + att_w // LANES
    v_col = k_col + att_w // LANES

    x2 = x.reshape(batch * seq, d)
    c_pad = jnp.pad(c, ((0, SUBLANES - batch % SUBLANES if batch % SUBLANES else 0), (0, 0)))
    for l in range(depth):
        mod = _ada(c_pad, w_ada[l], b_ada[l][None, :])
        mod4 = mod[:batch].reshape(batch, 6, 1, d)
        wt = jnp.swapaxes(w_in[l], 0, 1).astype(BF16)
        wt_small = jnp.concatenate([wt[o_dt:o_q], wt[o_f:]], axis=0)
        wt_small = jnp.pad(wt_small, ((0, LANES - wt_small.shape[0]), (0, 0)))
        brow = _pad_lanes(jnp.concatenate([dt_bias[l], f_bias[l]])[None, :])
        proj, small, q_t, v_t, feat_q_t, feat_k = _inproj(
            x2, mod4, wt, wt_small, brow, batch, seq, v_col * LANES + att_w, o_dt, o_q - o_dt,
            ssm_w, q_col * LANES, k_col * LANES, v_col * LANES, ATT_HEAD_DIM ** -0.5 * LOG2E)
        y_att_t = _attn(proj, q_t, v_t, feat_q_t, feat_k, batch, seq, q_col)

        alog_row = _pad_lanes(a_log[l][None, :])
        dsk_full = jnp.repeat(d_skip[l], SSM_HEAD_DIM)[None, :]
        x2 = _mix_tail(x2, proj, small, y_att_t, mod4, w_out[l].astype(BF16),
                       attn_norm_w[l][None, :], ln1_g[l][None, :], ln1_b[l][None, :],
                       w_ff_in[l].astype(BF16), w_ff_out[l].astype(BF16), ln2_g[l][None, :],
                       ln2_b[l][None, :], conv_w[l], conv_b[l][None, :], brow, alog_row, dsk_full,
                       ssm_norm_w[l][None, :], seq, alpha)
    return x2.reshape(batch, seq, d)
```

```python
import functools

import jax
import jax.numpy as jnp
import numpy as np
from jax import lax
from jax.experimental import pallas as pl
from jax.experimental.pallas import tpu as pltpu

F32 = jnp.float32
BF16 = jnp.bfloat16

LANES = 128
SUBLANES = 8
BF16_SUBLANES = 16
VMEM_LIMIT_BYTES = 56 * 1024 * 1024

SSM_HEADS = 16
SSM_HEAD_DIM = 64
SSM_GROUPS = 2
SSM_STATE = 128
CONV_WIDTH = 4
CHUNK = 128
ATT_HEADS = 16
ATT_HEAD_DIM = 64
LN_EPS = 1e-5
RMS_EPS = 1e-5

HEADS_PER_GROUP = SSM_HEADS // SSM_GROUPS
HEADS_PER_BLOCK = LANES // ATT_HEAD_DIM
SMALL_DT = 0
SMALL_F = SSM_HEADS
LOG2E = 1.4426950408889634
BIAS_PIECES = 3
BIAS_LANES_PER_HEAD = 2 * BIAS_PIECES


def _silu(v):
    return v * (1.0 / (1.0 + jnp.exp(-v)))


def _softplus(v):
    return jnp.maximum(v, 0.0) + jnp.log(1.0 + jnp.exp(-jnp.abs(v)))


def _split3(c):
    hi = c.astype(BF16).astype(F32)
    r = c - hi
    mid = r.astype(BF16).astype(F32)
    return hi, mid, r - mid


def _pieces(c):
    return jnp.concatenate(_split3(c), axis=1).astype(BF16)


def _layer_norm(v, g, b):
    mu = jnp.mean(v, axis=-1, keepdims=True)
    d = v - mu
    var = jnp.mean(d * d, axis=-1, keepdims=True)
    return d * lax.rsqrt(var + LN_EPS) * g + b


def _ada_kernel(c_ref, w_ref, b_ref, o_ref):
    ca = _silu(c_ref[...]).astype(BF16)
    o_ref[...] = jnp.dot(ca, w_ref[...].astype(BF16), preferred_element_type=F32) + b_ref[...]


def _ada(c_pad, w, b, tn=2048):
    rows, d = c_pad.shape
    n = w.shape[1]
    return pl.pallas_call(
        _ada_kernel,
        grid=(n // tn,),
        in_specs=[pl.BlockSpec((rows, d), lambda j: (0, 0)),
                  pl.BlockSpec((d, tn), lambda j: (0, j)),
                  pl.BlockSpec((1, tn), lambda j: (0, j))],
        out_specs=pl.BlockSpec((rows, tn), lambda j: (0, j)),
        out_shape=jax.ShapeDtypeStruct((rows, n), F32),
        compiler_params=pltpu.CompilerParams(
            dimension_semantics=("arbitrary",), vmem_limit_bytes=VMEM_LIMIT_BYTES),
        name="ada",
    )(c_pad, w, b)


def _inproj_kernel(x_ref, mod_ref, wt_ref, wst_ref, brow_ref, place_ref, ones_ref, o_ref, os_ref,
                   oqt_ref, ovt_ref, fq_ref, fk_ref, car_ref, *, tn, tiles_per_seq, n_main, gap_at,
                   gap, z_hi, q_lo, q_hi, v_lo, q_scale):
    nt = (((1,), (1,)), ((), ()))
    tm = x_ref.shape[0]

    @pl.when(pl.program_id(0) == 0)
    def _():
        car_ref[...] = jnp.zeros(car_ref.shape, F32)

    h = (x_ref[...] * (1.0 + mod_ref[1]) + mod_ref[0]).astype(BF16)
    small = lax.dot_general(h, wst_ref[...], nt, preferred_element_type=F32)
    os_ref[...] = small

    lane = lax.broadcasted_iota(jnp.int32, (tm, LANES), 1)
    is_f = jnp.logical_and(lane >= SMALL_F, lane < SMALL_F + ATT_HEADS)
    log_f = jnp.where(is_f, -_softplus(-(small + brow_ref[...])), 0.0)
    tril = (lax.broadcasted_iota(jnp.int32, (CHUNK, CHUNK), 0)
            >= lax.broadcasted_iota(jnp.int32, (CHUNK, CHUNK), 1)).astype(BF16)
    seq_start = pl.program_id(0) % tiles_per_seq == 0
    carry = jnp.where(seq_start, 0.0, car_ref[...])
    cums = []
    for r0 in range(0, tm, CHUNK):
        cs3 = jnp.dot(tril, _pieces(log_f[r0:r0 + CHUNK, :]), preferred_element_type=F32)
        cums.append(cs3[:, 0:LANES] + cs3[:, LANES:2 * LANES] + cs3[:, 2 * LANES:] + carry)
        carry = cums[-1][CHUNK - 1:CHUNK, :]
    car_ref[...] = carry
    cum = jnp.concatenate(cums, axis=0)
    feats = jnp.dot(_pieces(cum * LOG2E), place_ref[...], preferred_element_type=F32) + ones_ref[...]
    fq_ref[...] = feats[:, 0:LANES].T.astype(BF16)
    fk_ref[...] = feats[:, LANES:].astype(BF16)

    for c0 in range(0, n_main, tn):
        r0 = c0 if c0 < gap_at else c0 + gap
        acc = lax.dot_general(h, wt_ref[r0:r0 + tn, :], nt, preferred_element_type=F32)
        if c0 >= v_lo:
            ovt_ref[c0 - v_lo:c0 - v_lo + tn, :] = acc.T.astype(BF16)
        elif c0 >= q_hi:
            o_ref[:, c0 - (q_hi - q_lo):c0 - (q_hi - q_lo) + tn] = acc.astype(BF16)
        elif c0 >= q_lo:
            oqt_ref[c0 - q_lo:c0 - q_lo + tn, :] = (acc * q_scale).T.astype(BF16)
        else:
            o_ref[:, c0:c0 + tn] = (_silu(acc) if c0 < z_hi else acc).astype(BF16)


def _inproj(x2, mod4, wt, wt_small, brow, batch, seq, n_main, gap_at, gap, z_hi, q_lo, q_hi, v_lo,
            q_scale, tm=512, tn=512):
    n_tok, d = x2.shape
    assert all(c % tn == 0 for c in (z_hi, q_lo, q_hi, v_lo, n_main, gap_at))
    assert gap % BF16_SUBLANES == 0
    tiles_per_seq = seq // tm
    place, ones_row = _bias_routing()
    const = lambda i: (0, 0)
    tok = lambda i: (i, 0)
    resident = functools.partial(pl.BlockSpec, index_map=const, pipeline_mode=pl.Buffered(1))
    n_rowmajor = v_lo - (q_hi - q_lo)
    by_channel = lambda i: (i // tiles_per_seq, 0, i % tiles_per_seq)
    return pl.pallas_call(
        functools.partial(_inproj_kernel, tn=tn, tiles_per_seq=tiles_per_seq, n_main=n_main,
                          gap_at=gap_at, gap=gap, z_hi=z_hi, q_lo=q_lo, q_hi=q_hi, v_lo=v_lo,
                          q_scale=q_scale),
        grid=(n_tok // tm,),
        in_specs=[pl.BlockSpec((tm, d), tok),
                  pl.BlockSpec((None, 6, 1, d), lambda i: (i // tiles_per_seq, 0, 0, 0)),
                  resident(wt.shape),
                  resident((LANES, d)),
                  pl.BlockSpec((1, LANES), const),
                  pl.BlockSpec(place.shape, const),
                  pl.BlockSpec(ones_row.shape, const)],
        out_specs=[pl.BlockSpec((tm, n_rowmajor), tok),
                   pl.BlockSpec((tm, LANES), tok),
                   pl.BlockSpec((None, q_hi - q_lo, tm), by_channel),
                   pl.BlockSpec((None, n_main - v_lo, tm), by_channel),
                   pl.BlockSpec((None, LANES, tm), by_channel),
                   pl.BlockSpec((tm, LANES), tok)],
        out_shape=[jax.ShapeDtypeStruct((n_tok, n_rowmajor), BF16),
                   jax.ShapeDtypeStruct((n_tok, LANES), F32),
                   jax.ShapeDtypeStruct((batch, q_hi - q_lo, seq), BF16),
                   jax.ShapeDtypeStruct((batch, n_main - v_lo, seq), BF16),
                   jax.ShapeDtypeStruct((batch, LANES, seq), BF16),
                   jax.ShapeDtypeStruct((n_tok, LANES), BF16)],
        scratch_shapes=[pltpu.VMEM((1, LANES), F32)],
        compiler_params=pltpu.CompilerParams(
            dimension_semantics=("arbitrary",), vmem_limit_bytes=VMEM_LIMIT_BYTES),
        name="inproj",
    )(x2, mod4, wt, wt_small, brow, place, ones_row)


def _ssd_rows(gate, conv_silu, small, seq_start, brow_ref, alog_ref, dsk_ref, nw_ref, st_ref):
    width = SSM_HEADS * SSM_HEAD_DIM
    gw = HEADS_PER_GROUP * SSM_HEAD_DIM
    u = conv_silu()
    xs = u[:, 0:width]
    b_all = u[:, width:width + SSM_GROUPS * SSM_STATE]
    c_all = u[:, width + SSM_GROUPS * SSM_STATE:]

    lane = lax.broadcasted_iota(jnp.int32, (CHUNK, LANES), 1)
    row = lax.broadcasted_iota(jnp.int32, (CHUNK, LANES), 0)
    is_dt = lane < SMALL_F
    sp = _softplus(small + brow_ref[...])
    a_row = -jnp.exp(alog_ref[...])
    dt_tile = jnp.where(is_dt, sp, 0.0)
    val = jnp.where(is_dt, sp * a_row, 0.0)
    tril = row >= lane
    cs3 = jnp.dot(tril.astype(BF16), _pieces(val), preferred_element_type=F32)
    yield
    cs = cs3[:, 0:LANES] + cs3[:, LANES:2 * LANES] + cs3[:, 2 * LANES:]

    cs_t = cs.T
    dt_t = dt_tile.T
    lane_lo = lane < SSM_HEAD_DIM

    def per_head_lanes(tile):
        cols = [jnp.broadcast_to(tile[:, SMALL_DT + h:SMALL_DT + h + 1], (CHUNK, LANES))
                for h in range(SSM_HEADS)]
        blocks = [jnp.where(lane_lo, cols[h], cols[h + 1]) for h in range(0, SSM_HEADS, 2)]
        return cols, jnp.concatenate(blocks, axis=1)

    cs_cols, acs_full = per_head_lanes(cs)
    _, dt_full = per_head_lanes(dt_tile)
    atot_full = acs_full[CHUNK - 1:CHUNK, :]
    dec_out = jnp.exp(acs_full)
    dec_end = jnp.exp(atot_full - acs_full)
    chunk_dec = jnp.exp(atot_full)
    xs_b = xs.astype(BF16)
    xw = (xs * (dec_end * dt_full)).astype(BF16)

    bgs = [b_all[:, g * SSM_STATE:(g + 1) * SSM_STATE] for g in range(SSM_GROUPS)]
    cgs = [c_all[:, g * SSM_STATE:(g + 1) * SSM_STATE].astype(BF16) for g in range(SSM_GROUPS)]
    cbms = [lax.dot_general(cg, bg.astype(BF16), (((1,), (1,)), ((), ())),
                            preferred_element_type=F32) for cg, bg in zip(cgs, bgs)]
    yield
    ydiag_blocks = []
    yoff_blocks = []
    for g in range(SSM_GROUPS):
        bg, cg, cbm = bgs[g], cgs[g], cbms[g]
        for j in range(HEADS_PER_GROUP // HEADS_PER_BLOCK):
            h0 = g * HEADS_PER_GROUP + HEADS_PER_BLOCK * j
            xp = xs_b[:, h0 * SSM_HEAD_DIM:h0 * SSM_HEAD_DIM + LANES]
            scs = []
            for h in range(h0, h0 + HEADS_PER_BLOCK):
                seg = cs_cols[h] - cs_t[h:h + 1, :]
                dec = jnp.exp(jnp.where(tril, seg, -1e30))
                scs.append((cbm * dec * dt_t[h:h + 1, :]).astype(BF16))
            zero = jnp.zeros_like(xp)
            xp2 = jnp.concatenate([jnp.where(lane_lo, xp, zero), jnp.where(lane_lo, zero, xp)], axis=0)
            ydiag_blocks.append(jnp.dot(jnp.concatenate(scs, axis=1), xp2,
                                        preferred_element_type=F32))
        st = st_ref[g]
        if seq_start is not None:
            st = jnp.where(seq_start, 0.0, st)
        yoff_blocks.append(jnp.dot(cg, st.astype(BF16), preferred_element_type=F32)
                           * dec_out[:, g * gw:(g + 1) * gw])
        st_ref[g] = st * chunk_dec[:, g * gw:(g + 1) * gw] + jnp.dot(
            bg.T.astype(BF16), xw[:, g * gw:(g + 1) * gw], preferred_element_type=F32)
    y = (jnp.concatenate(ydiag_blocks, axis=1) + jnp.concatenate(yoff_blocks, axis=1)
         + dsk_ref[...] * xs)
    gated = y * gate.astype(F32)
    ms = jnp.mean(gated * gated, axis=-1, keepdims=True)
    yield (gated * lax.rsqrt(ms + RMS_EPS) * nw_ref[...]).astype(BF16)


def _bias_routing():
    place = np.zeros((BIAS_PIECES * LANES, 2 * LANES), np.float32)
    ones_row = np.zeros((1, 2 * LANES), np.float32)
    for h in range(ATT_HEADS):
        base = BIAS_LANES_PER_HEAD * h
        for t in range(BIAS_PIECES):
            place[t * LANES + SMALL_F + h, base + t] = 1.0
            place[t * LANES + SMALL_F + h, LANES + base + BIAS_PIECES + t] = -1.0
            ones_row[0, base + BIAS_PIECES + t] = 1.0
            ones_row[0, LANES + base + t] = 1.0
    return jnp.asarray(place, BF16), jnp.asarray(ones_row, F32)


ATT_TK = 256
ATT_SUB = 256
ATT_UNROLL = 4
ATT_VT_ROWS = ATT_HEAD_DIM + BF16_SUBLANES


def _attn_kernel(q_ref, k_ref, vt_ref, fq_ref, fk_ref, o_ref, st_ref, mx_ref, acc_ref, m_ref, *, tq):
    pair = pl.program_id(1)
    qi = pl.program_id(2)
    n_sub = tq // ATT_SUB
    assert n_sub % ATT_UNROLL == 0
    hd = ATT_HEAD_DIM
    ones_rows = jnp.ones((ATT_VT_ROWS - hd, ATT_TK), BF16)

    chan = lax.broadcasted_iota(jnp.int32, (LANES, tq), 0)
    q = q_ref[...]
    fq = fq_ref[...]
    q_aug = []
    for i in range(HEADS_PER_BLOCK):
        first = BIAS_LANES_PER_HEAD * (HEADS_PER_BLOCK * pair + i)
        own_q = (chan < hd) if i == 0 else (chan >= hd)
        own_f = jnp.logical_and(chan >= first, chan < first + BIAS_LANES_PER_HEAD)
        q_aug.append(jnp.concatenate([jnp.where(own_q, q, jnp.zeros_like(q)),
                                      jnp.where(own_f, fq, jnp.zeros_like(fq))], axis=0))
    causal = (lax.broadcasted_iota(jnp.int32, (ATT_TK, ATT_SUB), 0)
              <= lax.broadcasted_iota(jnp.int32, (ATT_TK, ATT_SUB), 1))
    chains = [(i, s) for i in range(HEADS_PER_BLOCK) for s in range(n_sub)]

    def head_scores(i, block, s_from=0):
        rows = pl.ds(pl.multiple_of(block * ATT_TK, ATT_TK), ATT_TK)
        kb = jnp.concatenate([k_ref[rows, :], fk_ref[rows, :]], axis=1)
        return jnp.dot(kb, q_aug[i][:, s_from * ATT_SUB:], preferred_element_type=F32)

    def update(c, block, st, st_max):
        m = m_ref[c]
        m_new = jnp.maximum(m, st_max)
        pt = jnp.exp2(st - m_new).astype(BF16)
        start = pl.multiple_of(block * ATT_TK, ATT_TK)
        i = chains[c][0]
        vt = jnp.concatenate([vt_ref[i * hd:(i + 1) * hd, pl.ds(start, ATT_TK)], ones_rows], axis=0)
        acc_ref[c] = (jnp.exp2(m - m_new) * acc_ref[c]
                      + jnp.dot(vt, pt, preferred_element_type=F32))
        m_ref[c] = m_new

    def stash(slot, block):
        for i in range(HEADS_PER_BLOCK):
            st = head_scores(i, block)
            for s in range(n_sub):
                part = st[:, s * ATT_SUB:(s + 1) * ATT_SUB]
                st_ref[slot, i * n_sub + s] = part
                mx_ref[slot, i * n_sub + s] = jnp.max(part, axis=0, keepdims=True)

    every = list(range(len(chains)))
    for c in every:
        m_ref[c] = jnp.full((1, ATT_SUB), -1e30, F32)
        acc_ref[c] = jnp.zeros((ATT_VT_ROWS, ATT_SUB), F32)

    first_diag = qi * n_sub
    diag = []
    for e in range(n_sub):
        for i in range(HEADS_PER_BLOCK):
            st = head_scores(i, first_diag + e, s_from=e)
            for s in range(e, n_sub):
                part = st[:, (s - e) * ATT_SUB:(s - e + 1) * ATT_SUB]
                if s == e:
                    part = jnp.where(causal, part, -1e30)
                diag.append((i * n_sub + s, first_diag + e, part))
    stash(0, 0)
    for c, block, part in diag:
        update(c, block, part, jnp.max(part, axis=0, keepdims=True))

    def half_step(block, slot, prefetch=True):
        if prefetch:
            stash(1 - slot, block + 1)
        for c in every:
            update(c, block, st_ref[slot, c], mx_ref[slot, c])

    def body(j, carry):
        for t in range(ATT_UNROLL):
            half_step(ATT_UNROLL * j + t, t % 2)
        return carry

    lax.fori_loop(0, qi * (n_sub // ATT_UNROLL) - 1, body, 0)

    @pl.when(qi > 0)
    def _():
        for t in range(ATT_UNROLL):
            half_step(first_diag - ATT_UNROLL + t, t % 2, prefetch=t < ATT_UNROLL - 1)

    heads = [[None] * n_sub for _ in range(HEADS_PER_BLOCK)]
    for c, (i, s) in enumerate(chains):
        acc = acc_ref[c]
        heads[i][s] = acc[0:hd, :] * (1.0 / acc[hd:hd + 1, :])
    o_t = jnp.concatenate([jnp.concatenate(h, axis=1) for h in heads], axis=0)
    o_ref[...] = o_t.astype(BF16)


def _attn(proj, q_t, v_t, feat_q_t, feat_k, batch, seq, k_col, tq=2048):
    nq = seq // tq
    n_pairs = ATT_HEADS // HEADS_PER_BLOCK
    n_chains = HEADS_PER_BLOCK * (tq // ATT_SUB)
    return pl.pallas_call(
        functools.partial(_attn_kernel, tq=tq),
        grid=(batch, n_pairs, nq),
        in_specs=[pl.BlockSpec((None, LANES, tq), lambda b, p, i: (b, p, i)),
                  pl.BlockSpec((seq, LANES), lambda b, p, i: (b, k_col + p)),
                  pl.BlockSpec((None, LANES, seq), lambda b, p, i: (b, p, 0)),
                  pl.BlockSpec((None, LANES, tq), lambda b, p, i: (b, 0, i)),
                  pl.BlockSpec((seq, LANES), lambda b, p, i: (b, 0))],
        out_specs=pl.BlockSpec((None, LANES, tq), lambda b, p, i: (b, p, i)),
        out_shape=jax.ShapeDtypeStruct((batch, ATT_HEADS * ATT_HEAD_DIM, seq), BF16),
        scratch_shapes=[pltpu.VMEM((2, n_chains, ATT_TK, ATT_SUB), F32),
                        pltpu.VMEM((2, n_chains, 1, ATT_SUB), F32),
                        pltpu.VMEM((n_chains, ATT_VT_ROWS, ATT_SUB), F32),
                        pltpu.VMEM((n_chains, 1, ATT_SUB), F32)],
        compiler_params=pltpu.CompilerParams(
            dimension_semantics=("arbitrary", "arbitrary", "arbitrary"),
            vmem_limit_bytes=VMEM_LIMIT_BYTES),
        name="attn",
    )(q_t, proj, v_t, feat_q_t, feat_k)


def _mix_tail_kernel(x_ref, ya_ref, mod_ref, wo_ref, anw_ref, g1_ref, b1_ref, w1_ref, w2_ref, g2_ref,
                     b2_ref, gate_ref, xs_ref, bc_ref, sm_ref, xh_ref, bch_ref, cw_ref, cb_ref,
                     brow_ref, alog_ref, dsk_ref, nw_ref, o_ref, ys_ref, st_ref, xf_ref, *,
                     alpha, ff_chunk, tiles_per_seq, n_tiles):
    i = pl.program_id(0)
    tm = x_ref.shape[0]
    slot = i % 2
    n_chunks = tm // CHUNK
    n_ff = w1_ref.shape[1] // ff_chunk
    assert n_chunks == n_ff
    chunk_rows = [slice(c * CHUNK, (c + 1) * CHUNK) for c in range(n_chunks)]
    width = xs_ref.shape[1]
    halo = xh_ref.shape[0]

    def step(with_ssd, with_mlp):
        ssd = []
        if with_ssd:
            seq_start = i % tiles_per_seq == 0
            before = jnp.concatenate([xh_ref[...], bch_ref[...]], axis=1).astype(F32)
            xf_ref[0:halo, :] = jnp.where(seq_start, 0.0, before)
            xf_ref[halo:, 0:width] = xs_ref[...].astype(F32)
            xf_ref[halo:, width:] = bc_ref[...].astype(F32)

            def conv_silu(c):
                conv = cb_ref[...]
                for k in range(CONV_WIDTH):
                    first = halo + c * CHUNK - k
                    conv = conv + (cw_ref[CONV_WIDTH - 1 - k:CONV_WIDTH - k, :]
                                   * xf_ref[first:first + CHUNK, :])
                return _silu(conv)

            ssd = [_ssd_rows(gate_ref[rows, :], functools.partial(conv_silu, c), sm_ref[rows, :],
                             seq_start if c == 0 else None, brow_ref, alog_ref, dsk_ref, nw_ref,
                             st_ref) for c, rows in enumerate(chunk_rows)]

        def advance(c, last=False):
            if 0 <= c < len(ssd):
                out = next(ssd[c])
                if last:
                    ys_ref[slot, chunk_rows[c], :] = out

        advance(0)
        if with_mlp:
            ssm_w = ys_ref.shape[2]
            ya = ya_ref[...].astype(F32).T
            ya = ya * lax.rsqrt(jnp.mean(ya * ya, axis=-1, keepdims=True) + RMS_EPS) * anw_ref[...]
            y = (jnp.dot(ys_ref[1 - slot], wo_ref[0:ssm_w, :], preferred_element_type=F32)
                 + jnp.dot(ya.astype(BF16), wo_ref[ssm_w:, :], preferred_element_type=F32))
            x1 = _layer_norm(alpha * x_ref[...] + (1.0 + mod_ref[2]) * y, g1_ref[...], b1_ref[...])
            h = (x1 * (1.0 + mod_ref[4]) + mod_ref[3]).astype(BF16)
            ff = jnp.zeros(x1.shape, F32)
        advance(0)
        for c in range(n_ff):
            if with_mlp:
                a = jnp.dot(h, w1_ref[:, c * ff_chunk:(c + 1) * ff_chunk],
                            preferred_element_type=F32)
                a = jnp.maximum(a, 0.0)
            advance(c, last=True)
            advance(c + 1)
            if with_mlp:
                ff = ff + jnp.dot((a * a).astype(BF16), w2_ref[c * ff_chunk:(c + 1) * ff_chunk, :],
                                  preferred_element_type=F32)
            advance(c + 1)
        if with_mlp:
            o_ref[...] = _layer_norm(alpha * x1 + (1.0 + mod_ref[5]) * ff, g2_ref[...], b2_ref[...])

    @pl.when(i == 0)
    def _():
        st_ref[...] = jnp.zeros(st_ref.shape, F32)
        step(True, False)

    @pl.when(jnp.logical_and(i > 0, i < n_tiles))
    def _():
        step(True, True)

    @pl.when(i == n_tiles)
    def _():
        step(False, True)


def _mix_tail(x2, proj, small, y_att_t, mod4, w_out, anw, g1, b1, w1, w2, g2, b2, conv_w, conv_b,
              brow, alog_row, dsk_full, ssm_norm_w, seq, alpha, tm=512, ff_chunk=1024):
    n_tok, d = x2.shape
    mix = w_out.shape[0]
    d_ff = w1.shape[1]
    width = SSM_HEADS * SSM_HEAD_DIM
    bc_w = 2 * SSM_GROUPS * SSM_STATE
    x_col, bc_col = 1, 2 * width // bc_w
    tiles_per_seq = seq // tm
    n_tiles = n_tok // tm
    const = lambda i: (0, 0)
    resident = functools.partial(pl.BlockSpec, index_map=const, pipeline_mode=pl.Buffered(1))
    mlp_tile = lambda i: jnp.maximum(i - 1, 0)
    ssd_tile = lambda i: jnp.minimum(i, n_tiles - 1)
    halo = lambda i: jnp.maximum(ssd_tile(i) * (tm // BF16_SUBLANES) - 1, 0)
    return pl.pallas_call(
        functools.partial(_mix_tail_kernel, alpha=alpha, ff_chunk=ff_chunk,
                          tiles_per_seq=tiles_per_seq, n_tiles=n_tiles),
        grid=(n_tiles + 1,),
        in_specs=[pl.BlockSpec((tm, d), lambda i: (mlp_tile(i), 0)),
                  pl.BlockSpec((None, y_att_t.shape[1], tm),
                               lambda i: (mlp_tile(i) // tiles_per_seq, 0,
                                          mlp_tile(i) % tiles_per_seq)),
                  pl.BlockSpec((None, 6, 1, d), lambda i: (mlp_tile(i) // tiles_per_seq, 0, 0, 0)),
                  resident((mix, d)),
                  pl.BlockSpec((1, y_att_t.shape[1]), const),
                  pl.BlockSpec((1, d), const),
                  pl.BlockSpec((1, d), const),
                  resident((d, d_ff)),
                  resident((d_ff, d)),
                  pl.BlockSpec((1, d), const),
                  pl.BlockSpec((1, d), const),
                  pl.BlockSpec((tm, width), lambda i: (ssd_tile(i), 0)),
                  pl.BlockSpec((tm, width), lambda i: (ssd_tile(i), x_col)),
                  pl.BlockSpec((tm, bc_w), lambda i: (ssd_tile(i), bc_col)),
                  pl.BlockSpec((tm, LANES), lambda i: (ssd_tile(i), 0)),
                  pl.BlockSpec((BF16_SUBLANES, width), lambda i: (halo(i), x_col)),
                  pl.BlockSpec((BF16_SUBLANES, bc_w), lambda i: (halo(i), bc_col)),
                  pl.BlockSpec(conv_w.shape, const),
                  pl.BlockSpec(conv_b.shape, const),
                  pl.BlockSpec((1, LANES), const),
                  pl.BlockSpec((1, LANES), const),
                  pl.BlockSpec((1, width), const),
                  pl.BlockSpec((1, width), const)],
        out_specs=pl.BlockSpec((tm, d), lambda i: (mlp_tile(i), 0)),
        out_shape=jax.ShapeDtypeStruct((n_tok, d), F32),
        scratch_shapes=[pltpu.VMEM((2, tm, width), BF16),
                        pltpu.VMEM((SSM_GROUPS, SSM_STATE, HEADS_PER_GROUP * SSM_HEAD_DIM), F32),
                        pltpu.VMEM((BF16_SUBLANES + tm, width + bc_w), F32)],
        compiler_params=pltpu.CompilerParams(
            dimension_semantics=("arbitrary",), vmem_limit_bytes=VMEM_LIMIT_BYTES),
        name="mix_tail",
    )(x2, y_att_t, mod4, w_out, anw, g1, b1, w1, w2, g2, b2, proj, proj, proj, small, proj, proj,
      conv_w, conv_b, brow, alog_row, dsk_full, ssm_norm_w)


def _pad_lanes(v):
    return jnp.pad(v, ((0, 0), (0, LANES - v.shape[1])))


def kernel(x, c, w_ada, b_ada, w_in, conv_w, conv_b, dt_bias, a_log, d_skip, ssm_norm_w, f_bias,
           attn_norm_w, w_out, ln1_g, ln1_b, w_ff_in, w_ff_out, ln2_g, ln2_b):
    batch, seq, d = x.shape
    depth = w_ada.shape[0]
    alpha = (2.0 * depth) ** 0.25
    ssm_w = SSM_HEADS * SSM_HEAD_DIM
    att_w = ATT_HEADS * ATT_HEAD_DIM
    conv_dim = ssm_w + 2 * SSM_GROUPS * SSM_STATE
    o_xbc = ssm_w
    o_dt = o_xbc + conv_dim
    o_q = o_dt + SSM_HEADS
    o_k = o_q + att_w
    o_v = o_k + att_w
    o_f = o_v + att_w
    q_col = (ssm_w + conv_dim) // LANES
    k_col = q_col + att_w // LANES
    v_col = k_col + att_w // LANES

    x2 = x.reshape(batch * seq, d)
    c_pad = jnp.pad(c, ((0, SUBLANES - batch % SUBLANES if batch % SUBLANES else 0), (0, 0)))
    for l in range(depth):
        mod = _ada(c_pad, w_ada[l], b_ada[l][None, :])
        mod4 = mod[:batch].reshape(batch, 6, 1, d)
        wt = jnp.swapaxes(w_in[l], 0, 1).astype(BF16)
        wt_small = jnp.concatenate([wt[o_dt:o_q], wt[o_f:]], axis=0)
        wt_small = jnp.pad(wt_small, ((0, LANES - wt_small.shape[0]), (0, 0)))
        brow = _pad_lanes(jnp.concatenate([dt_bias[l], f_bias[l]])[None, :])
        proj, small, q_t, v_t, feat_q_t, feat_k = _inproj(
            x2, mod4, wt, wt_small, brow, batch, seq, v_col * LANES + att_w, o_dt, o_q - o_dt,
            ssm_w, q_col * LANES, k_col * LANES, v_col * LANES, ATT_HEAD_DIM ** -0.5 * LOG2E)
        y_att_t = _attn(proj, q_t, v_t, feat_q_t, feat_k, batch, seq, q_col)

        alog_row = _pad_lanes(a_log[l][None, :])
        dsk_full = jnp.repeat(d_skip[l], SSM_HEAD_DIM)[None, :]
        x2 = _mix_tail(x2, proj, small, y_att_t, mod4, w_out[l].astype(BF16),
                       attn_norm_w[l][None, :], ln1_g[l][None, :], ln1_b[l][None, :],
                       w_ff_in[l].astype(BF16), w_ff_out[l].astype(BF16), ln2_g[l][None, :],
                       ln2_b[l][None, :], conv_w[l], conv_b[l][None, :], brow, alog_row, dsk_full,
                       ssm_norm_w[l][None, :], seq, alpha)
    return x2.reshape(batch, seq, d)
```

```python
import functools

import jax
import jax.numpy as jnp
import numpy as np
from jax import lax
from jax.experimental import pallas as pl
from jax.experimental.pallas import tpu as pltpu

F32 = jnp.float32
BF16 = jnp.bfloat16

LANES = 128
SUBLANES = 8
BF16_SUBLANES = 16
VMEM_LIMIT_BYTES = 56 * 1024 * 1024

SSM_HEADS = 16
SSM_HEAD_DIM = 64
SSM_GROUPS = 2
SSM_STATE = 128
CONV_WIDTH = 4
CHUNK = 128
ATT_HEADS = 16
ATT_HEAD_DIM = 64
LN_EPS = 1e-5
RMS_EPS = 1e-5

HEADS_PER_GROUP = SSM_HEADS // SSM_GROUPS
HEADS_PER_BLOCK = LANES // ATT_HEAD_DIM
SMALL_DT = 0
SMALL_F = SSM_HEADS
LOG2E = 1.4426950408889634
MASKED = -1e30
BIAS_PIECES = 3
BIAS_LANES_PER_HEAD = 2 * BIAS_PIECES


def _silu(v):
    return v * (1.0 / (1.0 + jnp.exp(-v)))


def _softplus(v):
    return jnp.maximum(v, 0.0) + jnp.log(1.0 + jnp.exp(-jnp.abs(v)))


def _split3(c):
    hi = c.astype(BF16).astype(F32)
    r = c - hi
    mid = r.astype(BF16).astype(F32)
    return hi, mid, r - mid


def _pieces(c):
    return jnp.concatenate(_split3(c), axis=1).astype(BF16)


def _layer_norm(v, g, b):
    mu = jnp.mean(v, axis=-1, keepdims=True)
    d = v - mu
    var = jnp.mean(d * d, axis=-1, keepdims=True)
    return d * lax.rsqrt(var + LN_EPS) * g + b


def _ada_kernel(c_ref, w_ref, b_ref, o_ref):
    ca = _silu(c_ref[...]).astype(BF16)
    o_ref[...] = jnp.dot(ca, w_ref[...].astype(BF16), preferred_element_type=F32) + b_ref[...]


def _ada(c_pad, w, b, tn=2048):
    rows, d = c_pad.shape
    n = w.shape[1]
    return pl.pallas_call(
        _ada_kernel,
        grid=(n // tn,),
        in_specs=[pl.BlockSpec((rows, d), lambda j: (0, 0)),
                  pl.BlockSpec((d, tn), lambda j: (0, j)),
                  pl.BlockSpec((1, tn), lambda j: (0, j))],
        out_specs=pl.BlockSpec((rows, tn), lambda j: (0, j)),
        out_shape=jax.ShapeDtypeStruct((rows, n), F32),
        compiler_params=pltpu.CompilerParams(
            dimension_semantics=("arbitrary",), vmem_limit_bytes=VMEM_LIMIT_BYTES),
        name="ada",
    )(c_pad, w, b)


def _inproj_kernel(x_ref, mod_ref, wt_ref, wst_ref, brow_ref, place_ref, ones_ref, o_ref, os_ref,
                   oqt_ref, ovt_ref, fq_ref, fk_ref, car_ref, *, tn, tiles_per_seq, n_main, gap_at,
                   gap, z_hi, q_lo, q_hi, v_lo, q_scale):
    nt = (((1,), (1,)), ((), ()))
    tm = x_ref.shape[0]

    @pl.when(pl.program_id(0) == 0)
    def _():
        car_ref[...] = jnp.zeros(car_ref.shape, F32)

    h = (x_ref[...] * (1.0 + mod_ref[1]) + mod_ref[0]).astype(BF16)
    small = lax.dot_general(h, wst_ref[...], nt, preferred_element_type=F32)
    os_ref[...] = small

    lane = lax.broadcasted_iota(jnp.int32, (tm, LANES), 1)
    is_f = jnp.logical_and(lane >= SMALL_F, lane < SMALL_F + ATT_HEADS)
    log_f = jnp.where(is_f, -_softplus(-(small + brow_ref[...])), 0.0)
    tril = (lax.broadcasted_iota(jnp.int32, (CHUNK, CHUNK), 0)
            >= lax.broadcasted_iota(jnp.int32, (CHUNK, CHUNK), 1)).astype(BF16)
    seq_start = pl.program_id(0) % tiles_per_seq == 0
    carry = jnp.where(seq_start, 0.0, car_ref[...])
    cums = []
    for r0 in range(0, tm, CHUNK):
        cs3 = jnp.dot(tril, _pieces(log_f[r0:r0 + CHUNK, :]), preferred_element_type=F32)
        cums.append(cs3[:, 0:LANES] + cs3[:, LANES:2 * LANES] + cs3[:, 2 * LANES:] + carry)
        carry = cums[-1][CHUNK - 1:CHUNK, :]
    car_ref[...] = carry
    cum = jnp.concatenate(cums, axis=0)
    feats = jnp.dot(_pieces(cum * LOG2E), place_ref[...], preferred_element_type=F32) + ones_ref[...]
    fq_ref[...] = feats[:, 0:LANES].T.astype(BF16)
    fk_ref[...] = feats[:, LANES:].astype(BF16)

    for c0 in range(0, n_main, tn):
        r0 = c0 if c0 < gap_at else c0 + gap
        acc = lax.dot_general(h, wt_ref[r0:r0 + tn, :], nt, preferred_element_type=F32)
        if c0 >= v_lo:
            ovt_ref[c0 - v_lo:c0 - v_lo + tn, :] = acc.T.astype(BF16)
        elif c0 >= q_hi:
            o_ref[:, c0 - (q_hi - q_lo):c0 - (q_hi - q_lo) + tn] = acc.astype(BF16)
        elif c0 >= q_lo:
            oqt_ref[c0 - q_lo:c0 - q_lo + tn, :] = (acc * q_scale).T.astype(BF16)
        else:
            o_ref[:, c0:c0 + tn] = (_silu(acc) if c0 < z_hi else acc).astype(BF16)


def _inproj(x2, mod4, wt, wt_small, brow, batch, seq, n_main, gap_at, gap, z_hi, q_lo, q_hi, v_lo,
            q_scale, tm=512, tn=512):
    n_tok, d = x2.shape
    assert all(c % tn == 0 for c in (z_hi, q_lo, q_hi, v_lo, n_main, gap_at))
    assert gap % BF16_SUBLANES == 0
    tiles_per_seq = seq // tm
    place, ones_row = _bias_routing()
    const = lambda i: (0, 0)
    tok = lambda i: (i, 0)
    resident = functools.partial(pl.BlockSpec, index_map=const, pipeline_mode=pl.Buffered(1))
    n_rowmajor = v_lo - (q_hi - q_lo)
    by_channel = lambda i: (i // tiles_per_seq, 0, i % tiles_per_seq)
    return pl.pallas_call(
        functools.partial(_inproj_kernel, tn=tn, tiles_per_seq=tiles_per_seq, n_main=n_main,
                          gap_at=gap_at, gap=gap, z_hi=z_hi, q_lo=q_lo, q_hi=q_hi, v_lo=v_lo,
                          q_scale=q_scale),
        grid=(n_tok // tm,),
        in_specs=[pl.BlockSpec((tm, d), tok),
                  pl.BlockSpec((None, 6, 1, d), lambda i: (i // tiles_per_seq, 0, 0, 0)),
                  resident(wt.shape),
                  resident((LANES, d)),
                  pl.BlockSpec((1, LANES), const),
                  pl.BlockSpec(place.shape, const),
                  pl.BlockSpec(ones_row.shape, const)],
        out_specs=[pl.BlockSpec((tm, n_rowmajor), tok),
                   pl.BlockSpec((tm, LANES), tok),
                   pl.BlockSpec((None, q_hi - q_lo, tm), by_channel),
                   pl.BlockSpec((None, n_main - v_lo, tm), by_channel),
                   pl.BlockSpec((None, LANES, tm), by_channel),
                   pl.BlockSpec((tm, LANES), tok)],
        out_shape=[jax.ShapeDtypeStruct((n_tok, n_rowmajor), BF16),
                   jax.ShapeDtypeStruct((n_tok, LANES), F32),
                   jax.ShapeDtypeStruct((batch, q_hi - q_lo, seq), BF16),
                   jax.ShapeDtypeStruct((batch, n_main - v_lo, seq), BF16),
                   jax.ShapeDtypeStruct((batch, LANES, seq), BF16),
                   jax.ShapeDtypeStruct((n_tok, LANES), BF16)],
        scratch_shapes=[pltpu.VMEM((1, LANES), F32)],
        compiler_params=pltpu.CompilerParams(
            dimension_semantics=("arbitrary",), vmem_limit_bytes=VMEM_LIMIT_BYTES),
        name="inproj",
    )(x2, mod4, wt, wt_small, brow, place, ones_row)


def _ssd_rows(gate, conv_silu, small, seq_start, brow_ref, alog_ref, dsk_ref, nw_ref, st_ref):
    width = SSM_HEADS * SSM_HEAD_DIM
    gw = HEADS_PER_GROUP * SSM_HEAD_DIM
    u = conv_silu()
    xs = u[:, 0:width]
    b_all = u[:, width:width + SSM_GROUPS * SSM_STATE]
    c_all = u[:, width + SSM_GROUPS * SSM_STATE:]

    lane = lax.broadcasted_iota(jnp.int32, (CHUNK, LANES), 1)
    row = lax.broadcasted_iota(jnp.int32, (CHUNK, LANES), 0)
    is_dt = lane < SMALL_F
    sp = _softplus(small + brow_ref[...])
    a_row = -jnp.exp(alog_ref[...])
    dt_tile = jnp.where(is_dt, sp, 0.0)
    val = jnp.where(is_dt, sp * a_row, 0.0)
    tril = row >= lane
    cs3 = jnp.dot(tril.astype(BF16), _pieces(val), preferred_element_type=F32)
    yield
    cs = cs3[:, 0:LANES] + cs3[:, LANES:2 * LANES] + cs3[:, 2 * LANES:]

    cs_t = cs.T
    dt_t = dt_tile.T
    lane_lo = lane < SSM_HEAD_DIM

    def per_head_lanes(tile):
        cols = [jnp.broadcast_to(tile[:, SMALL_DT + h:SMALL_DT + h + 1], (CHUNK, LANES))
                for h in range(SSM_HEADS)]
        blocks = [jnp.where(lane_lo, cols[h], cols[h + 1]) for h in range(0, SSM_HEADS, 2)]
        return cols, jnp.concatenate(blocks, axis=1)

    cs_cols, acs_full = per_head_lanes(cs)
    _, dt_full = per_head_lanes(dt_tile)
    atot_full = acs_full[CHUNK - 1:CHUNK, :]
    dec_out = jnp.exp(acs_full)
    dec_end = jnp.exp(atot_full - acs_full)
    chunk_dec = jnp.exp(atot_full)
    xs_b = xs.astype(BF16)
    xw = (xs * (dec_end * dt_full)).astype(BF16)

    bgs = [b_all[:, g * SSM_STATE:(g + 1) * SSM_STATE] for g in range(SSM_GROUPS)]
    cgs = [c_all[:, g * SSM_STATE:(g + 1) * SSM_STATE].astype(BF16) for g in range(SSM_GROUPS)]
    cbms = [lax.dot_general(cg, bg.astype(BF16), (((1,), (1,)), ((), ())),
                            preferred_element_type=F32) for cg, bg in zip(cgs, bgs)]
    yield
    ydiag_blocks = []
    yoff_blocks = []
    for g in range(SSM_GROUPS):
        bg, cg, cbm = bgs[g], cgs[g], cbms[g]
        for j in range(HEADS_PER_GROUP // HEADS_PER_BLOCK):
            h0 = g * HEADS_PER_GROUP + HEADS_PER_BLOCK * j
            xp = xs_b[:, h0 * SSM_HEAD_DIM:h0 * SSM_HEAD_DIM + LANES]
            scs = []
            for h in range(h0, h0 + HEADS_PER_BLOCK):
                seg = cs_cols[h] - cs_t[h:h + 1, :]
                dec = jnp.exp(jnp.where(tril, seg, MASKED))
                scs.append((cbm * dec * dt_t[h:h + 1, :]).astype(BF16))
            zero = jnp.zeros_like(xp)
            xp2 = jnp.concatenate([jnp.where(lane_lo, xp, zero), jnp.where(lane_lo, zero, xp)], axis=0)
            ydiag_blocks.append(jnp.dot(jnp.concatenate(scs, axis=1), xp2,
                                        preferred_element_type=F32))
        st = st_ref[g]
        if seq_start is not None:
            st = jnp.where(seq_start, 0.0, st)
        yoff_blocks.append(jnp.dot(cg, st.astype(BF16), preferred_element_type=F32)
                           * dec_out[:, g * gw:(g + 1) * gw])
        st_ref[g] = st * chunk_dec[:, g * gw:(g + 1) * gw] + jnp.dot(
            bg.T.astype(BF16), xw[:, g * gw:(g + 1) * gw], preferred_element_type=F32)
    y = (jnp.concatenate(ydiag_blocks, axis=1) + jnp.concatenate(yoff_blocks, axis=1)
         + dsk_ref[...] * xs)
    gated = y * gate.astype(F32)
    ms = jnp.mean(gated * gated, axis=-1, keepdims=True)
    yield (gated * lax.rsqrt(ms + RMS_EPS) * nw_ref[...]).astype(BF16)


def _bias_routing():
    place = np.zeros((BIAS_PIECES * LANES, 2 * LANES), np.float32)
    ones_row = np.zeros((1, 2 * LANES), np.float32)
    for h in range(ATT_HEADS):
        base = BIAS_LANES_PER_HEAD * h
        for t in range(BIAS_PIECES):
            place[t * LANES + SMALL_F + h, base + t] = 1.0
            place[t * LANES + SMALL_F + h, LANES + base + BIAS_PIECES + t] = -1.0
            ones_row[0, base + BIAS_PIECES + t] = 1.0
            ones_row[0, LANES + base + t] = 1.0
    return jnp.asarray(place, BF16), jnp.asarray(ones_row, F32)


ATT_TK = 256
ATT_SUB = 256
ATT_UNROLL = 4
ATT_VT_ROWS = ATT_HEAD_DIM + BF16_SUBLANES


def _attn_kernel(q_ref, k_ref, vt_ref, fq_ref, fk_ref, o_ref, st_ref, mx_ref, acc_ref, m_ref, *, tq):
    pair = pl.program_id(1)
    qi = pl.program_id(2)
    n_sub = tq // ATT_SUB
    assert n_sub % ATT_UNROLL == 0
    hd = ATT_HEAD_DIM
    ones_rows = jnp.ones((ATT_VT_ROWS - hd, ATT_TK), BF16)

    chan = lax.broadcasted_iota(jnp.int32, (LANES, tq), 0)
    q = q_ref[...]
    fq = fq_ref[...]
    q_aug = []
    for i in range(HEADS_PER_BLOCK):
        first = BIAS_LANES_PER_HEAD * (HEADS_PER_BLOCK * pair + i)
        own_q = (chan < hd) if i == 0 else (chan >= hd)
        own_f = jnp.logical_and(chan >= first, chan < first + BIAS_LANES_PER_HEAD)
        q_aug.append(jnp.concatenate([jnp.where(own_q, q, jnp.zeros_like(q)),
                                      jnp.where(own_f, fq, jnp.zeros_like(fq))], axis=0))
    causal = (lax.broadcasted_iota(jnp.int32, (ATT_TK, ATT_SUB), 0)
              <= lax.broadcasted_iota(jnp.int32, (ATT_TK, ATT_SUB), 1))
    chains = [(i, s) for i in range(HEADS_PER_BLOCK) for s in range(n_sub)]

    def head_scores(i, block, s_from=0):
        rows = pl.ds(pl.multiple_of(block * ATT_TK, ATT_TK), ATT_TK)
        kb = jnp.concatenate([k_ref[rows, :], fk_ref[rows, :]], axis=1)
        return jnp.dot(kb, q_aug[i][:, s_from * ATT_SUB:], preferred_element_type=F32)

    def update(c, block, st, st_max):
        m = m_ref[c]
        m_new = jnp.maximum(m, st_max)
        pt = jnp.exp2(st - m_new).astype(BF16)
        start = pl.multiple_of(block * ATT_TK, ATT_TK)
        i = chains[c][0]
        vt = jnp.concatenate([vt_ref[i * hd:(i + 1) * hd, pl.ds(start, ATT_TK)], ones_rows], axis=0)
        acc_ref[c] = (jnp.exp2(m - m_new) * acc_ref[c]
                      + jnp.dot(vt, pt, preferred_element_type=F32))
        m_ref[c] = m_new

    def stash(slot, block):
        for i in range(HEADS_PER_BLOCK):
            st = head_scores(i, block)
            for s in range(n_sub):
                part = st[:, s * ATT_SUB:(s + 1) * ATT_SUB]
                st_ref[slot, i * n_sub + s] = part
                mx_ref[slot, i * n_sub + s] = jnp.max(part, axis=0, keepdims=True)

    every = list(range(len(chains)))
    for c in every:
        m_ref[c] = jnp.full((1, ATT_SUB), MASKED, F32)
        acc_ref[c] = jnp.zeros((ATT_VT_ROWS, ATT_SUB), F32)

    first_diag = qi * n_sub
    diag = []
    for e in range(n_sub):
        for i in range(HEADS_PER_BLOCK):
            st = head_scores(i, first_diag + e, s_from=e)
            for s in range(e, n_sub):
                part = st[:, (s - e) * ATT_SUB:(s - e + 1) * ATT_SUB]
                if s == e:
                    part = jnp.where(causal, part, MASKED)
                diag.append((i * n_sub + s, first_diag + e, part))
    stash(0, 0)
    for c, block, part in diag:
        update(c, block, part, jnp.max(part, axis=0, keepdims=True))

    def half_step(block, slot, prefetch=True):
        if prefetch:
            stash(1 - slot, block + 1)
        for c in every:
            update(c, block, st_ref[slot, c], mx_ref[slot, c])

    def body(j, carry):
        for t in range(ATT_UNROLL):
            half_step(ATT_UNROLL * j + t, t % 2)
        return carry

    lax.fori_loop(0, qi * (n_sub // ATT_UNROLL) - 1, body, 0)

    @pl.when(qi > 0)
    def _():
        for t in range(ATT_UNROLL):
            half_step(first_diag - ATT_UNROLL + t, t % 2, prefetch=t < ATT_UNROLL - 1)

    heads = [[None] * n_sub for _ in range(HEADS_PER_BLOCK)]
    for c, (i, s) in enumerate(chains):
        acc = acc_ref[c]
        heads[i][s] = acc[0:hd, :] * (1.0 / acc[hd:hd + 1, :])
    o_t = jnp.concatenate([jnp.concatenate(h, axis=1) for h in heads], axis=0)
    o_ref[...] = o_t.astype(BF16)


def _attn(proj, q_t, v_t, feat_q_t, feat_k, batch, seq, k_col, tq=2048):
    nq = seq // tq
    n_pairs = ATT_HEADS // HEADS_PER_BLOCK
    n_chains = HEADS_PER_BLOCK * (tq // ATT_SUB)
    return pl.pallas_call(
        functools.partial(_attn_kernel, tq=tq),
        grid=(batch, n_pairs, nq),
        in_specs=[pl.BlockSpec((None, LANES, tq), lambda b, p, i: (b, p, i)),
                  pl.BlockSpec((seq, LANES), lambda b, p, i: (b, k_col + p)),
                  pl.BlockSpec((None, LANES, seq), lambda b, p, i: (b, p, 0)),
                  pl.BlockSpec((None, LANES, tq), lambda b, p, i: (b, 0, i)),
                  pl.BlockSpec((seq, LANES), lambda b, p, i: (b, 0))],
        out_specs=pl.BlockSpec((None, LANES, tq), lambda b, p, i: (b, p, i)),
        out_shape=jax.ShapeDtypeStruct((batch, ATT_HEADS * ATT_HEAD_DIM, seq), BF16),
        scratch_shapes=[pltpu.VMEM((2, n_chains, ATT_TK, ATT_SUB), F32),
                        pltpu.VMEM((2, n_chains, 1, ATT_SUB), F32),
                        pltpu.VMEM((n_chains, ATT_VT_ROWS, ATT_SUB), F32),
                        pltpu.VMEM((n_chains, 1, ATT_SUB), F32)],
        compiler_params=pltpu.CompilerParams(
            dimension_semantics=("arbitrary", "arbitrary", "arbitrary"),
            vmem_limit_bytes=VMEM_LIMIT_BYTES),
        name="attn",
    )(q_t, proj, v_t, feat_q_t, feat_k)


def _mix_tail_kernel(x_ref, ya_ref, mod_ref, wo_ref, anw_ref, g1_ref, b1_ref, w1_ref, w2_ref, g2_ref,
                     b2_ref, gate_ref, xs_ref, bc_ref, sm_ref, xh_ref, bch_ref, cw_ref, cb_ref,
                     brow_ref, alog_ref, dsk_ref, nw_ref, o_ref, ys_ref, st_ref, xf_ref, *,
                     alpha, ff_chunk, tiles_per_seq, n_tiles):
    i = pl.program_id(0)
    tm = x_ref.shape[0]
    slot = i % 2
    n_chunks = tm // CHUNK
    n_ff = w1_ref.shape[1] // ff_chunk
    assert n_chunks == n_ff
    chunk_rows = [slice(c * CHUNK, (c + 1) * CHUNK) for c in range(n_chunks)]
    width = xs_ref.shape[1]
    halo = xh_ref.shape[0]

    def step(with_ssd, with_mlp):
        ssd = []
        if with_ssd:
            seq_start = i % tiles_per_seq == 0
            before = jnp.concatenate([xh_ref[...], bch_ref[...]], axis=1).astype(F32)
            xf_ref[0:halo, :] = jnp.where(seq_start, 0.0, before)
            xf_ref[halo:, 0:width] = xs_ref[...].astype(F32)
            xf_ref[halo:, width:] = bc_ref[...].astype(F32)

            def conv_silu(c):
                conv = cb_ref[...]
                for k in range(CONV_WIDTH):
                    first = halo + c * CHUNK - k
                    conv = conv + (cw_ref[CONV_WIDTH - 1 - k:CONV_WIDTH - k, :]
                                   * xf_ref[first:first + CHUNK, :])
                return _silu(conv)

            ssd = [_ssd_rows(gate_ref[rows, :], functools.partial(conv_silu, c), sm_ref[rows, :],
                             seq_start if c == 0 else None, brow_ref, alog_ref, dsk_ref, nw_ref,
                             st_ref) for c, rows in enumerate(chunk_rows)]

        def advance(c, last=False):
            if 0 <= c < len(ssd):
                out = next(ssd[c])
                if last:
                    ys_ref[slot, chunk_rows[c], :] = out

        advance(0)
        if with_mlp:
            ssm_w = ys_ref.shape[2]
            ya = ya_ref[...].astype(F32).T
            ya = ya * lax.rsqrt(jnp.mean(ya * ya, axis=-1, keepdims=True) + RMS_EPS) * anw_ref[...]
            y = (jnp.dot(ys_ref[1 - slot], wo_ref[0:ssm_w, :], preferred_element_type=F32)
                 + jnp.dot(ya.astype(BF16), wo_ref[ssm_w:, :], preferred_element_type=F32))
            x1 = _layer_norm(alpha * x_ref[...] + (1.0 + mod_ref[2]) * y, g1_ref[...], b1_ref[...])
            h = (x1 * (1.0 + mod_ref[4]) + mod_ref[3]).astype(BF16)
            ff = jnp.zeros(x1.shape, F32)
        advance(0)
        for c in range(n_ff):
            if with_mlp:
                a = jnp.dot(h, w1_ref[:, c * ff_chunk:(c + 1) * ff_chunk],
                            preferred_element_type=F32)
                a = jnp.maximum(a, 0.0)
            advance(c, last=True)
            advance(c + 1)
            if with_mlp:
                ff = ff + jnp.dot((a * a).astype(BF16), w2_ref[c * ff_chunk:(c + 1) * ff_chunk, :],
                                  preferred_element_type=F32)
            advance(c + 1)
        if with_mlp:
            o_ref[...] = _layer_norm(alpha * x1 + (1.0 + mod_ref[5]) * ff, g2_ref[...], b2_ref[...])

    @pl.when(i == 0)
    def _():
        st_ref[...] = jnp.zeros(st_ref.shape, F32)
        step(True, False)

    @pl.when(jnp.logical_and(i > 0, i < n_tiles))
    def _():
        step(True, True)

    @pl.when(i == n_tiles)
    def _():
        step(False, True)


def _mix_tail(x2, proj, small, y_att_t, mod4, w_out, anw, g1, b1, w1, w2, g2, b2, conv_w, conv_b,
              brow, alog_row, dsk_full, ssm_norm_w, seq, alpha, tm=512, ff_chunk=1024):
    n_tok, d = x2.shape
    mix = w_out.shape[0]
    d_ff = w1.shape[1]
    width = SSM_HEADS * SSM_HEAD_DIM
    bc_w = 2 * SSM_GROUPS * SSM_STATE
    x_col, bc_col = 1, 2 * width // bc_w
    tiles_per_seq = seq // tm
    n_tiles = n_tok // tm
    const = lambda i: (0, 0)
    resident = functools.partial(pl.BlockSpec, index_map=const, pipeline_mode=pl.Buffered(1))
    mlp_tile = lambda i: jnp.maximum(i - 1, 0)
    ssd_tile = lambda i: jnp.minimum(i, n_tiles - 1)
    halo = lambda i: jnp.maximum(ssd_tile(i) * (tm // BF16_SUBLANES) - 1, 0)
    return pl.pallas_call(
        functools.partial(_mix_tail_kernel, alpha=alpha, ff_chunk=ff_chunk,
                          tiles_per_seq=tiles_per_seq, n_tiles=n_tiles),
        grid=(n_tiles + 1,),
        in_specs=[pl.BlockSpec((tm, d), lambda i: (mlp_tile(i), 0)),
                  pl.BlockSpec((None, y_att_t.shape[1], tm),
                               lambda i: (mlp_tile(i) // tiles_per_seq, 0,
                                          mlp_tile(i) % tiles_per_seq)),
                  pl.BlockSpec((None, 6, 1, d), lambda i: (mlp_tile(i) // tiles_per_seq, 0, 0, 0)),
                  resident((mix, d)),
                  pl.BlockSpec((1, y_att_t.shape[1]), const),
                  pl.BlockSpec((1, d), const),
                  pl.BlockSpec((1, d), const),
                  resident((d, d_ff)),
                  resident((d_ff, d)),
                  pl.BlockSpec((1, d), const),
                  pl.BlockSpec((1, d), const),
                  pl.BlockSpec((tm, width), lambda i: (ssd_tile(i), 0)),
                  pl.BlockSpec((tm, width), lambda i: (ssd_tile(i), x_col)),
                  pl.BlockSpec((tm, bc_w), lambda i: (ssd_tile(i), bc_col)),
                  pl.BlockSpec((tm, LANES), lambda i: (ssd_tile(i), 0)),
                  pl.BlockSpec((BF16_SUBLANES, width), lambda i: (halo(i), x_col)),
                  pl.BlockSpec((BF16_SUBLANES, bc_w), lambda i: (halo(i), bc_col)),
                  pl.BlockSpec(conv_w.shape, const),
                  pl.BlockSpec(conv_b.shape, const),
                  pl.BlockSpec((1, LANES), const),
                  pl.BlockSpec((1, LANES), const),
                  pl.BlockSpec((1, width), const),
                  pl.BlockSpec((1, width), const)],
        out_specs=pl.BlockSpec((tm, d), lambda i: (mlp_tile(i), 0)),
        out_shape=jax.ShapeDtypeStruct((n_tok, d), F32),
        scratch_shapes=[pltpu.VMEM((2, tm, width), BF16),
                        pltpu.VMEM((SSM_GROUPS, SSM_STATE, HEADS_PER_GROUP * SSM_HEAD_DIM), F32),
                        pltpu.VMEM((BF16_SUBLANES + tm, width + bc_w), F32)],
        compiler_params=pltpu.CompilerParams(
            dimension_semantics=("arbitrary",), vmem_limit_bytes=VMEM_LIMIT_BYTES),
        name="mix_tail",
    )(x2, y_att_t, mod4, w_out, anw, g1, b1, w1, w2, g2, b2, proj, proj, proj, small, proj, proj,
      conv_w, conv_b, brow, alog_row, dsk_full, ssm_norm_w)


def _pad_lanes(v):
    return jnp.pad(v, ((0, 0), (0, LANES - v.shape[1])))


def kernel(x, c, w_ada, b_ada, w_in, conv_w, conv_b, dt_bias, a_log, d_skip, ssm_norm_w, f_bias,
           attn_norm_w, w_out, ln1_g, ln1_b, w_ff_in, w_ff_out, ln2_g, ln2_b):
    batch, seq, d = x.shape
    depth = w_ada.shape[0]
    alpha = (2.0 * depth) ** 0.25
    ssm_w = SSM_HEADS * SSM_HEAD_DIM
    att_w = ATT_HEADS * ATT_HEAD_DIM
    conv_dim = ssm_w + 2 * SSM_GROUPS * SSM_STATE
    o_xbc = ssm_w
    o_dt = o_xbc + conv_dim
    o_q = o_dt + SSM_HEADS
    o_k = o_q + att_w
    o_v = o_k + att_w
    o_f = o_v + att_w
    q_col = (ssm_w + conv_dim) // LANES
    k_col = q_col + att_w // LANES
    v_col = k_col + att_w // LANES

    x2 = x.reshape(batch * seq, d)
    c_pad = jnp.pad(c, ((0, SUBLANES - batch % SUBLANES if batch % SUBLANES else 0), (0, 0)))
    for l in range(depth):
        mod = _ada(c_pad, w_ada[l], b_ada[l][None, :])
        mod4 = mod[:batch].reshape(batch, 6, 1, d)
        wt = jnp.swapaxes(w_in[l], 0, 1).astype(BF16)
        wt_small = jnp.concatenate([wt[o_dt:o_q], wt[o_f:]], axis=0)
        wt_small = jnp.pad(wt_small, ((0, LANES - wt_small.shape[0]), (0, 0)))
        brow = _pad_lanes(jnp.concatenate([dt_bias[l], f_bias[l]])[None, :])
        proj, small, q_t, v_t, feat_q_t, feat_k = _inproj(
            x2, mod4, wt, wt_small, brow, batch, seq, v_col * LANES + att_w, o_dt, o_q - o_dt,
            ssm_w, q_col * LANES, k_col * LANES, v_col * LANES, ATT_HEAD_DIM ** -0.5 * LOG2E)
        y_att_t = _attn(proj, q_t, v_t, feat_q_t, feat_k, batch, seq, q_col)

        alog_row = _pad_lanes(a_log[l][None, :])
        dsk_full = jnp.repeat(d_skip[l], SSM_HEAD_DIM)[None, :]
        x2 = _mix_tail(x2, proj, small, y_att_t, mod4, w_out[l].astype(BF16),
                       attn_norm_w[l][None, :], ln1_g[l][None, :], ln1_b[l][None, :],
                       w_ff_in[l].astype(BF16), w_ff_out[l].astype(BF16), ln2_g[l][None, :],
                       ln2_b[l][None, :], conv_w[l], conv_b[l][None, :], brow, alog_row, dsk_full,
                       ssm_norm_w[l][None, :], seq, alpha)
    return x2.reshape(batch, seq, d)
```

```python
import functools

import jax
import jax.numpy as jnp
import numpy as np
from jax import lax
from jax.experimental import pallas as pl
from jax.experimental.pallas import tpu as pltpu

F32 = jnp.float32
BF16 = jnp.bfloat16

LANES = 128
SUBLANES = 8
BF16_SUBLANES = 16
VMEM_LIMIT_BYTES = 56 * 1024 * 1024

SSM_HEADS = 16
SSM_HEAD_DIM = 64
SSM_GROUPS = 2
SSM_STATE = 128
CONV_WIDTH = 4
CHUNK = 128
ATT_HEADS = 16
ATT_HEAD_DIM = 64
LN_EPS = 1e-5
RMS_EPS = 1e-5

HEADS_PER_GROUP = SSM_HEADS // SSM_GROUPS
HEADS_PER_BLOCK = LANES // ATT_HEAD_DIM
SMALL_DT = 0
SMALL_F = SSM_HEADS
LOG2E = 1.4426950408889634
MASKED = -1e30
BIAS_PIECES = 3
BIAS_LANES_PER_HEAD = 2 * BIAS_PIECES


def _silu(v):
    return v * (1.0 / (1.0 + jnp.exp(-v)))


def _softplus(v):
    return jnp.maximum(v, 0.0) + jnp.log(1.0 + jnp.exp(-jnp.abs(v)))


def _split3(c):
    hi = c.astype(BF16).astype(F32)
    r = c - hi
    mid = r.astype(BF16).astype(F32)
    return hi, mid, r - mid


def _pieces(c):
    return jnp.concatenate(_split3(c), axis=1).astype(BF16)


def _layer_norm(v, g, b):
    mu = jnp.mean(v, axis=-1, keepdims=True)
    d = v - mu
    var = jnp.mean(d * d, axis=-1, keepdims=True)
    return d * lax.rsqrt(var + LN_EPS) * g + b


def _ada_kernel(c_ref, w_ref, b_ref, o_ref):
    ca = _silu(c_ref[...]).astype(BF16)
    o_ref[...] = jnp.dot(ca, w_ref[...].astype(BF16), preferred_element_type=F32) + b_ref[...]


def _ada(c_pad, w, b, tn=2048):
    rows, d = c_pad.shape
    n = w.shape[1]
    return pl.pallas_call(
        _ada_kernel,
        grid=(n // tn,),
        in_specs=[pl.BlockSpec((rows, d), lambda j: (0, 0)),
                  pl.BlockSpec((d, tn), lambda j: (0, j)),
                  pl.BlockSpec((1, tn), lambda j: (0, j))],
        out_specs=pl.BlockSpec((rows, tn), lambda j: (0, j)),
        out_shape=jax.ShapeDtypeStruct((rows, n), F32),
        compiler_params=pltpu.CompilerParams(
            dimension_semantics=("arbitrary",), vmem_limit_bytes=VMEM_LIMIT_BYTES),
        name="ada",
    )(c_pad, w, b)


def _inproj_kernel(x_ref, mod_ref, wt_ref, wst_ref, brow_ref, place_ref, ones_ref, o_ref, os_ref,
                   oqt_ref, ovt_ref, fq_ref, fk_ref, car_ref, *, tn, tiles_per_seq, n_main, gap_at,
                   gap, z_hi, q_lo, q_hi, v_lo, q_scale):
    nt = (((1,), (1,)), ((), ()))
    tm = x_ref.shape[0]

    @pl.when(pl.program_id(0) == 0)
    def _():
        car_ref[...] = jnp.zeros(car_ref.shape, F32)

    h = (x_ref[...] * (1.0 + mod_ref[1]) + mod_ref[0]).astype(BF16)
    small = lax.dot_general(h, wst_ref[...], nt, preferred_element_type=F32)
    os_ref[...] = small

    lane = lax.broadcasted_iota(jnp.int32, (tm, LANES), 1)
    is_f = jnp.logical_and(lane >= SMALL_F, lane < SMALL_F + ATT_HEADS)
    log_f = jnp.where(is_f, -_softplus(-(small + brow_ref[...])), 0.0)
    tril = (lax.broadcasted_iota(jnp.int32, (CHUNK, CHUNK), 0)
            >= lax.broadcasted_iota(jnp.int32, (CHUNK, CHUNK), 1)).astype(BF16)
    seq_start = pl.program_id(0) % tiles_per_seq == 0
    carry = jnp.where(seq_start, 0.0, car_ref[...])
    cums = []
    for r0 in range(0, tm, CHUNK):
        cs3 = jnp.dot(tril, _pieces(log_f[r0:r0 + CHUNK, :]), preferred_element_type=F32)
        cums.append(cs3[:, 0:LANES] + cs3[:, LANES:2 * LANES] + cs3[:, 2 * LANES:] + carry)
        carry = cums[-1][CHUNK - 1:CHUNK, :]
    car_ref[...] = carry
    cum = jnp.concatenate(cums, axis=0)
    feats = jnp.dot(_pieces(cum * LOG2E), place_ref[...], preferred_element_type=F32) + ones_ref[...]
    fq_ref[...] = feats[:, 0:LANES].T.astype(BF16)
    fk_ref[...] = feats[:, LANES:].astype(BF16)

    for c0 in range(0, n_main, tn):
        r0 = c0 if c0 < gap_at else c0 + gap
        acc = lax.dot_general(h, wt_ref[r0:r0 + tn, :], nt, preferred_element_type=F32)
        if c0 >= v_lo:
            ovt_ref[c0 - v_lo:c0 - v_lo + tn, :] = acc.T.astype(BF16)
        elif c0 >= q_hi:
            o_ref[:, c0 - (q_hi - q_lo):c0 - (q_hi - q_lo) + tn] = acc.astype(BF16)
        elif c0 >= q_lo:
            oqt_ref[c0 - q_lo:c0 - q_lo + tn, :] = (acc * q_scale).T.astype(BF16)
        else:
            o_ref[:, c0:c0 + tn] = (_silu(acc) if c0 < z_hi else acc).astype(BF16)


def _inproj(x2, mod4, wt, wt_small, brow, batch, seq, n_main, gap_at, gap, z_hi, q_lo, q_hi, v_lo,
            q_scale, tm=1024, tn=512):
    n_tok, d = x2.shape
    assert all(c % tn == 0 for c in (z_hi, q_lo, q_hi, v_lo, n_main, gap_at))
    assert gap % BF16_SUBLANES == 0
    tiles_per_seq = seq // tm
    place, ones_row = _bias_routing()
    const = lambda i: (0, 0)
    tok = lambda i: (i, 0)
    resident = functools.partial(pl.BlockSpec, index_map=const, pipeline_mode=pl.Buffered(1))
    n_rowmajor = v_lo - (q_hi - q_lo)
    by_channel = lambda i: (i // tiles_per_seq, 0, i % tiles_per_seq)
    return pl.pallas_call(
        functools.partial(_inproj_kernel, tn=tn, tiles_per_seq=tiles_per_seq, n_main=n_main,
                          gap_at=gap_at, gap=gap, z_hi=z_hi, q_lo=q_lo, q_hi=q_hi, v_lo=v_lo,
                          q_scale=q_scale),
        grid=(n_tok // tm,),
        in_specs=[pl.BlockSpec((tm, d), tok),
                  pl.BlockSpec((None, 6, 1, d), lambda i: (i // tiles_per_seq, 0, 0, 0)),
                  resident(wt.shape),
                  resident((LANES, d)),
                  pl.BlockSpec((1, LANES), const),
                  pl.BlockSpec(place.shape, const),
                  pl.BlockSpec(ones_row.shape, const)],
        out_specs=[pl.BlockSpec((tm, n_rowmajor), tok),
                   pl.BlockSpec((tm, LANES), tok),
                   pl.BlockSpec((None, q_hi - q_lo, tm), by_channel),
                   pl.BlockSpec((None, n_main - v_lo, tm), by_channel),
                   pl.BlockSpec((None, LANES, tm), by_channel),
                   pl.BlockSpec((tm, LANES), tok)],
        out_shape=[jax.ShapeDtypeStruct((n_tok, n_rowmajor), BF16),
                   jax.ShapeDtypeStruct((n_tok, LANES), F32),
                   jax.ShapeDtypeStruct((batch, q_hi - q_lo, seq), BF16),
                   jax.ShapeDtypeStruct((batch, n_main - v_lo, seq), BF16),
                   jax.ShapeDtypeStruct((batch, LANES, seq), BF16),
                   jax.ShapeDtypeStruct((n_tok, LANES), BF16)],
        scratch_shapes=[pltpu.VMEM((1, LANES), F32)],
        compiler_params=pltpu.CompilerParams(
            dimension_semantics=("arbitrary",), vmem_limit_bytes=VMEM_LIMIT_BYTES),
        name="inproj",
    )(x2, mod4, wt, wt_small, brow, place, ones_row)


def _ssd_rows(gate, conv_silu, small, seq_start, brow_ref, alog_ref, dsk_ref, nw_ref, st_ref):
    width = SSM_HEADS * SSM_HEAD_DIM
    gw = HEADS_PER_GROUP * SSM_HEAD_DIM
    u = conv_silu()
    xs = u[:, 0:width]
    b_all = u[:, width:width + SSM_GROUPS * SSM_STATE]
    c_all = u[:, width + SSM_GROUPS * SSM_STATE:]

    lane = lax.broadcasted_iota(jnp.int32, (CHUNK, LANES), 1)
    row = lax.broadcasted_iota(jnp.int32, (CHUNK, LANES), 0)
    is_dt = lane < SMALL_F
    sp = _softplus(small + brow_ref[...])
    a_row = -jnp.exp(alog_ref[...])
    dt_tile = jnp.where(is_dt, sp, 0.0)
    val = jnp.where(is_dt, sp * a_row, 0.0)
    tril = row >= lane
    cs3 = jnp.dot(tril.astype(BF16), _pieces(val), preferred_element_type=F32)
    yield
    cs = cs3[:, 0:LANES] + cs3[:, LANES:2 * LANES] + cs3[:, 2 * LANES:]

    cs_t = cs.T
    dt_t = dt_tile.T
    lane_lo = lane < SSM_HEAD_DIM

    def per_head_lanes(tile):
        cols = [jnp.broadcast_to(tile[:, SMALL_DT + h:SMALL_DT + h + 1], (CHUNK, LANES))
                for h in range(SSM_HEADS)]
        blocks = [jnp.where(lane_lo, cols[h], cols[h + 1]) for h in range(0, SSM_HEADS, 2)]
        return cols, jnp.concatenate(blocks, axis=1)

    cs_cols, acs_full = per_head_lanes(cs)
    _, dt_full = per_head_lanes(dt_tile)
    atot_full = acs_full[CHUNK - 1:CHUNK, :]
    dec_out = jnp.exp(acs_full)
    dec_end = jnp.exp(atot_full - acs_full)
    chunk_dec = jnp.exp(atot_full)
    xs_b = xs.astype(BF16)
    xw = (xs * (dec_end * dt_full)).astype(BF16)

    bgs = [b_all[:, g * SSM_STATE:(g + 1) * SSM_STATE] for g in range(SSM_GROUPS)]
    cgs = [c_all[:, g * SSM_STATE:(g + 1) * SSM_STATE].astype(BF16) for g in range(SSM_GROUPS)]
    cbms = [lax.dot_general(cg, bg.astype(BF16), (((1,), (1,)), ((), ())),
                            preferred_element_type=F32) for cg, bg in zip(cgs, bgs)]
    yield
    ydiag_blocks = []
    yoff_blocks = []
    for g in range(SSM_GROUPS):
        bg, cg, cbm = bgs[g], cgs[g], cbms[g]
        for j in range(HEADS_PER_GROUP // HEADS_PER_BLOCK):
            h0 = g * HEADS_PER_GROUP + HEADS_PER_BLOCK * j
            xp = xs_b[:, h0 * SSM_HEAD_DIM:h0 * SSM_HEAD_DIM + LANES]
            scs = []
            for h in range(h0, h0 + HEADS_PER_BLOCK):
                seg = cs_cols[h] - cs_t[h:h + 1, :]
                dec = jnp.exp(jnp.where(tril, seg, MASKED))
                scs.append((cbm * dec * dt_t[h:h + 1, :]).astype(BF16))
            zero = jnp.zeros_like(xp)
            xp2 = jnp.concatenate([jnp.where(lane_lo, xp, zero), jnp.where(lane_lo, zero, xp)], axis=0)
            ydiag_blocks.append(jnp.dot(jnp.concatenate(scs, axis=1), xp2,
                                        preferred_element_type=F32))
        st = st_ref[g]
        if seq_start is not None:
            st = jnp.where(seq_start, 0.0, st)
        yoff_blocks.append(jnp.dot(cg, st.astype(BF16), preferred_element_type=F32)
                           * dec_out[:, g * gw:(g + 1) * gw])
        st_ref[g] = st * chunk_dec[:, g * gw:(g + 1) * gw] + jnp.dot(
            bg.T.astype(BF16), xw[:, g * gw:(g + 1) * gw], preferred_element_type=F32)
    y = (jnp.concatenate(ydiag_blocks, axis=1) + jnp.concatenate(yoff_blocks, axis=1)
         + dsk_ref[...] * xs)
    gated = y * gate.astype(F32)
    ms = jnp.mean(gated * gated, axis=-1, keepdims=True)
    yield (gated * lax.rsqrt(ms + RMS_EPS) * nw_ref[...]).astype(BF16)


def _bias_routing():
    place = np.zeros((BIAS_PIECES * LANES, 2 * LANES), np.float32)
    ones_row = np.zeros((1, 2 * LANES), np.float32)
    for h in range(ATT_HEADS):
        base = BIAS_LANES_PER_HEAD * h
        for t in range(BIAS_PIECES):
            place[t * LANES + SMALL_F + h, base + t] = 1.0
            place[t * LANES + SMALL_F + h, LANES + base + BIAS_PIECES + t] = -1.0
            ones_row[0, base + BIAS_PIECES + t] = 1.0
            ones_row[0, LANES + base + t] = 1.0
    return jnp.asarray(place, BF16), jnp.asarray(ones_row, F32)


ATT_TK = 256
ATT_SUB = 256
ATT_UNROLL = 8
ATT_VT_ROWS = ATT_HEAD_DIM + BF16_SUBLANES


def _attn_kernel(q_ref, k_ref, vt_ref, fq_ref, fk_ref, o_ref, st_ref, mx_ref, acc_ref, m_ref, *, tq):
    pair = pl.program_id(1)
    qi = pl.program_id(2)
    n_sub = tq // ATT_SUB
    assert n_sub % ATT_UNROLL == 0
    hd = ATT_HEAD_DIM
    ones_rows = jnp.ones((ATT_VT_ROWS - hd, ATT_TK), BF16)

    chan = lax.broadcasted_iota(jnp.int32, (LANES, tq), 0)
    q = q_ref[...]
    fq = fq_ref[...]
    q_aug = []
    for i in range(HEADS_PER_BLOCK):
        first = BIAS_LANES_PER_HEAD * (HEADS_PER_BLOCK * pair + i)
        own_q = (chan < hd) if i == 0 else (chan >= hd)
        own_f = jnp.logical_and(chan >= first, chan < first + BIAS_LANES_PER_HEAD)
        q_aug.append(jnp.concatenate([jnp.where(own_q, q, jnp.zeros_like(q)),
                                      jnp.where(own_f, fq, jnp.zeros_like(fq))], axis=0))
    causal = (lax.broadcasted_iota(jnp.int32, (ATT_TK, ATT_SUB), 0)
              <= lax.broadcasted_iota(jnp.int32, (ATT_TK, ATT_SUB), 1))
    chains = [(i, s) for i in range(HEADS_PER_BLOCK) for s in range(n_sub)]

    def head_scores(i, block, s_from=0):
        rows = pl.ds(pl.multiple_of(block * ATT_TK, ATT_TK), ATT_TK)
        kb = jnp.concatenate([k_ref[rows, :], fk_ref[rows, :]], axis=1)
        return jnp.dot(kb, q_aug[i][:, s_from * ATT_SUB:], preferred_element_type=F32)

    def update(c, block, st, st_max):
        m = m_ref[c]
        m_new = jnp.maximum(m, st_max)
        pt = jnp.exp2(st - m_new).astype(BF16)
        start = pl.multiple_of(block * ATT_TK, ATT_TK)
        i = chains[c][0]
        vt = jnp.concatenate([vt_ref[i * hd:(i + 1) * hd, pl.ds(start, ATT_TK)], ones_rows], axis=0)
        acc_ref[c] = (jnp.exp2(m - m_new) * acc_ref[c]
                      + jnp.dot(vt, pt, preferred_element_type=F32))
        m_ref[c] = m_new

    def stash(slot, block):
        for i in range(HEADS_PER_BLOCK):
            st = head_scores(i, block)
            for s in range(n_sub):
                part = st[:, s * ATT_SUB:(s + 1) * ATT_SUB]
                st_ref[slot, i * n_sub + s] = part
                mx_ref[slot, i * n_sub + s] = jnp.max(part, axis=0, keepdims=True)

    every = list(range(len(chains)))
    for c in every:
        m_ref[c] = jnp.full((1, ATT_SUB), MASKED, F32)
        acc_ref[c] = jnp.zeros((ATT_VT_ROWS, ATT_SUB), F32)

    first_diag = qi * n_sub
    diag = []
    for e in range(n_sub):
        for i in range(HEADS_PER_BLOCK):
            st = head_scores(i, first_diag + e, s_from=e)
            for s in range(e, n_sub):
                part = st[:, (s - e) * ATT_SUB:(s - e + 1) * ATT_SUB]
                if s == e:
                    part = jnp.where(causal, part, MASKED)
                diag.append((i * n_sub + s, first_diag + e, part))
    stash(0, 0)
    for c, block, part in diag:
        update(c, block, part, jnp.max(part, axis=0, keepdims=True))

    def half_step(block, slot, prefetch=True):
        if prefetch:
            stash(1 - slot, block + 1)
        for c in every:
            update(c, block, st_ref[slot, c], mx_ref[slot, c])

    def body(j, carry):
        for t in range(ATT_UNROLL):
            half_step(ATT_UNROLL * j + t, t % 2)
        return carry

    lax.fori_loop(0, qi * (n_sub // ATT_UNROLL) - 1, body, 0)

    @pl.when(qi > 0)
    def _():
        for t in range(ATT_UNROLL):
            half_step(first_diag - ATT_UNROLL + t, t % 2, prefetch=t < ATT_UNROLL - 1)

    heads = [[None] * n_sub for _ in range(HEADS_PER_BLOCK)]
    for c, (i, s) in enumerate(chains):
        acc = acc_ref[c]
        heads[i][s] = acc[0:hd, :] * (1.0 / acc[hd:hd + 1, :])
    o_t = jnp.concatenate([jnp.concatenate(h, axis=1) for h in heads], axis=0)
    o_ref[...] = o_t.astype(BF16)


def _attn(proj, q_t, v_t, feat_q_t, feat_k, batch, seq, k_col, tq=2048):
    nq = seq // tq
    n_pairs = ATT_HEADS // HEADS_PER_BLOCK
    n_chains = HEADS_PER_BLOCK * (tq // ATT_SUB)
    return pl.pallas_call(
        functools.partial(_attn_kernel, tq=tq),
        grid=(batch, n_pairs, nq),
        in_specs=[pl.BlockSpec((None, LANES, tq), lambda b, p, i: (b, p, i)),
                  pl.BlockSpec((seq, LANES), lambda b, p, i: (b, k_col + p)),
                  pl.BlockSpec((None, LANES, seq), lambda b, p, i: (b, p, 0)),
                  pl.BlockSpec((None, LANES, tq), lambda b, p, i: (b, 0, i)),
                  pl.BlockSpec((seq, LANES), lambda b, p, i: (b, 0))],
        out_specs=pl.BlockSpec((None, LANES, tq), lambda b, p, i: (b, p, i)),
        out_shape=jax.ShapeDtypeStruct((batch, ATT_HEADS * ATT_HEAD_DIM, seq), BF16),
        scratch_shapes=[pltpu.VMEM((2, n_chains, ATT_TK, ATT_SUB), F32),
                        pltpu.VMEM((2, n_chains, 1, ATT_SUB), F32),
                        pltpu.VMEM((n_chains, ATT_VT_ROWS, ATT_SUB), F32),
                        pltpu.VMEM((n_chains, 1, ATT_SUB), F32)],
        compiler_params=pltpu.CompilerParams(
            dimension_semantics=("arbitrary", "arbitrary", "arbitrary"),
            vmem_limit_bytes=VMEM_LIMIT_BYTES),
        name="attn",
    )(q_t, proj, v_t, feat_q_t, feat_k)


def _mix_tail_kernel(x_ref, ya_ref, mod_ref, wo_ref, anw_ref, g1_ref, b1_ref, w1_ref, w2_ref, g2_ref,
                     b2_ref, gate_ref, xs_ref, bc_ref, sm_ref, xh_ref, bch_ref, cw_ref, cb_ref,
                     brow_ref, alog_ref, dsk_ref, nw_ref, o_ref, ys_ref, st_ref, xf_ref, *,
                     alpha, ff_chunk, tiles_per_seq, n_tiles):
    i = pl.program_id(0)
    tm = x_ref.shape[0]
    slot = i % 2
    n_chunks = tm // CHUNK
    n_ff = w1_ref.shape[1] // ff_chunk
    assert n_chunks == n_ff
    chunk_rows = [slice(c * CHUNK, (c + 1) * CHUNK) for c in range(n_chunks)]
    width = xs_ref.shape[1]
    halo = xh_ref.shape[0]

    def step(with_ssd, with_mlp):
        ssd = []
        if with_ssd:
            seq_start = i % tiles_per_seq == 0
            before = jnp.concatenate([xh_ref[...], bch_ref[...]], axis=1).astype(F32)
            xf_ref[0:halo, :] = jnp.where(seq_start, 0.0, before)
            xf_ref[halo:, 0:width] = xs_ref[...].astype(F32)
            xf_ref[halo:, width:] = bc_ref[...].astype(F32)

            def conv_silu(c):
                conv = cb_ref[...]
                for k in range(CONV_WIDTH):
                    first = halo + c * CHUNK - k
                    conv = conv + (cw_ref[CONV_WIDTH - 1 - k:CONV_WIDTH - k, :]
                                   * xf_ref[first:first + CHUNK, :])
                return _silu(conv)

            ssd = [_ssd_rows(gate_ref[rows, :], functools.partial(conv_silu, c), sm_ref[rows, :],
                             seq_start if c == 0 else None, brow_ref, alog_ref, dsk_ref, nw_ref,
                             st_ref) for c, rows in enumerate(chunk_rows)]

        def advance(c, last=False):
            if 0 <= c < len(ssd):
                out = next(ssd[c])
                if last:
                    ys_ref[slot, chunk_rows[c], :] = out

        advance(0)
        if with_mlp:
            ssm_w = ys_ref.shape[2]
            ya = ya_ref[...].astype(F32).T
            ya = ya * lax.rsqrt(jnp.mean(ya * ya, axis=-1, keepdims=True) + RMS_EPS) * anw_ref[...]
            y = (jnp.dot(ys_ref[1 - slot], wo_ref[0:ssm_w, :], preferred_element_type=F32)
                 + jnp.dot(ya.astype(BF16), wo_ref[ssm_w:, :], preferred_element_type=F32))
            x1 = _layer_norm(alpha * x_ref[...] + (1.0 + mod_ref[2]) * y, g1_ref[...], b1_ref[...])
            h = (x1 * (1.0 + mod_ref[4]) + mod_ref[3]).astype(BF16)
            ff = jnp.zeros(x1.shape, F32)
        advance(0)
        for c in range(n_ff):
            if with_mlp:
                a = jnp.dot(h, w1_ref[:, c * ff_chunk:(c + 1) * ff_chunk],
                            preferred_element_type=F32)
                a = jnp.maximum(a, 0.0)
            advance(c, last=True)
            advance(c + 1)
            if with_mlp:
                ff = ff + jnp.dot((a * a).astype(BF16), w2_ref[c * ff_chunk:(c + 1) * ff_chunk, :],
                                  preferred_element_type=F32)
            advance(c + 1)
        if with_mlp:
            o_ref[...] = _layer_norm(alpha * x1 + (1.0 + mod_ref[5]) * ff, g2_ref[...], b2_ref[...])

    @pl.when(i == 0)
    def _():
        st_ref[...] = jnp.zeros(st_ref.shape, F32)
        step(True, False)

    @pl.when(jnp.logical_and(i > 0, i < n_tiles))
    def _():
        step(True, True)

    @pl.when(i == n_tiles)
    def _():
        step(False, True)


def _mix_tail(x2, proj, small, y_att_t, mod4, w_out, anw, g1, b1, w1, w2, g2, b2, conv_w, conv_b,
              brow, alog_row, dsk_full, ssm_norm_w, seq, alpha, tm=512, ff_chunk=1024):
    n_tok, d = x2.shape
    mix = w_out.shape[0]
    d_ff = w1.shape[1]
    width = SSM_HEADS * SSM_HEAD_DIM
    bc_w = 2 * SSM_GROUPS * SSM_STATE
    x_col, bc_col = 1, 2 * width // bc_w
    tiles_per_seq = seq // tm
    n_tiles = n_tok // tm
    const = lambda i: (0, 0)
    resident = functools.partial(pl.BlockSpec, index_map=const, pipeline_mode=pl.Buffered(1))
    mlp_tile = lambda i: jnp.maximum(i - 1, 0)
    ssd_tile = lambda i: jnp.minimum(i, n_tiles - 1)
    halo = lambda i: jnp.maximum(ssd_tile(i) * (tm // BF16_SUBLANES) - 1, 0)
    return pl.pallas_call(
        functools.partial(_mix_tail_kernel, alpha=alpha, ff_chunk=ff_chunk,
                          tiles_per_seq=tiles_per_seq, n_tiles=n_tiles),
        grid=(n_tiles + 1,),
        in_specs=[pl.BlockSpec((tm, d), lambda i: (mlp_tile(i), 0)),
                  pl.BlockSpec((None, y_att_t.shape[1], tm),
                               lambda i: (mlp_tile(i) // tiles_per_seq, 0,
                                          mlp_tile(i) % tiles_per_seq)),
                  pl.BlockSpec((None, 6, 1, d), lambda i: (mlp_tile(i) // tiles_per_seq, 0, 0, 0)),
                  resident((mix, d)),
                  pl.BlockSpec((1, y_att_t.shape[1]), const),
                  pl.BlockSpec((1, d), const),
                  pl.BlockSpec((1, d), const),
                  resident((d, d_ff)),
                  resident((d_ff, d)),
                  pl.BlockSpec((1, d), const),
                  pl.BlockSpec((1, d), const),
                  pl.BlockSpec((tm, width), lambda i: (ssd_tile(i), 0)),
                  pl.BlockSpec((tm, width), lambda i: (ssd_tile(i), x_col)),
                  pl.BlockSpec((tm, bc_w), lambda i: (ssd_tile(i), bc_col)),
                  pl.BlockSpec((tm, LANES), lambda i: (ssd_tile(i), 0)),
                  pl.BlockSpec((BF16_SUBLANES, width), lambda i: (halo(i), x_col)),
                  pl.BlockSpec((BF16_SUBLANES, bc_w), lambda i: (halo(i), bc_col)),
                  pl.BlockSpec(conv_w.shape, const),
                  pl.BlockSpec(conv_b.shape, const),
                  pl.BlockSpec((1, LANES), const),
                  pl.BlockSpec((1, LANES), const),
                  pl.BlockSpec((1, width), const),
                  pl.BlockSpec((1, width), const)],
        out_specs=pl.BlockSpec((tm, d), lambda i: (mlp_tile(i), 0)),
        out_shape=jax.ShapeDtypeStruct((n_tok, d), F32),
        scratch_shapes=[pltpu.VMEM((2, tm, width), BF16),
                        pltpu.VMEM((SSM_GROUPS, SSM_STATE, HEADS_PER_GROUP * SSM_HEAD_DIM), F32),
                        pltpu.VMEM((BF16_SUBLANES + tm, width + bc_w), F32)],
        compiler_params=pltpu.CompilerParams(
            dimension_semantics=("arbitrary",), vmem_limit_bytes=VMEM_LIMIT_BYTES),
        name="mix_tail",
    )(x2, y_att_t, mod4, w_out, anw, g1, b1, w1, w2, g2, b2, proj, proj, proj, small, proj, proj,
      conv_w, conv_b, brow, alog_row, dsk_full, ssm_norm_w)


def _pad_lanes(v):
    return jnp.pad(v, ((0, 0), (0, LANES - v.shape[1])))


def kernel(x, c, w_ada, b_ada, w_in, conv_w, conv_b, dt_bias, a_log, d_skip, ssm_norm_w, f_bias,
           attn_norm_w, w_out, ln1_g, ln1_b, w_ff_in, w_ff_out, ln2_g, ln2_b):
    batch, seq, d = x.shape
    depth = w_ada.shape[0]
    alpha = (2.0 * depth) ** 0.25
    ssm_w = SSM_HEADS * SSM_HEAD_DIM
    att_w = ATT_HEADS * ATT_HEAD_DIM
    conv_dim = ssm_w + 2 * SSM_GROUPS * SSM_STATE
    o_xbc = ssm_w
    o_dt = o_xbc + conv_dim
    o_q = o_dt + SSM_HEADS
    o_k = o_q + att_w
    o_v = o_k + att_w
    o_f = o_v + att_w
    q_col = (ssm_w + conv_dim) // LANES
    k_col = q_col + att_w // LANES
    v_col = k_col + att_w // LANES

    x2 = x.reshape(batch * seq, d)
    c_pad = jnp.pad(c, ((0, SUBLANES - batch % SUBLANES if batch % SUBLANES else 0), (0, 0)))
    for l in range(depth):
        mod = _ada(c_pad, w_ada[l], b_ada[l][None, :])
        mod4 = mod[:batch].reshape(batch, 6, 1, d)
        wt = jnp.swapaxes(w_in[l], 0, 1).astype(BF16)
        wt_small = jnp.concatenate([wt[o_dt:o_q], wt[o_f:]], axis=0)
        wt_small = jnp.pad(wt_small, ((0, LANES - wt_small.shape[0]), (0, 0)))
        brow = _pad_lanes(jnp.concatenate([dt_bias[l], f_bias[l]])[None, :])
        proj, small, q_t, v_t, feat_q_t, feat_k = _inproj(
            x2, mod4, wt, wt_small, brow, batch, seq, v_col * LANES + att_w, o_dt, o_q - o_dt,
            ssm_w, q_col * LANES, k_col * LANES, v_col * LANES, ATT_HEAD_DIM ** -0.5 * LOG2E)
        y_att_t = _attn(proj, q_t, v_t, feat_q_t, feat_k, batch, seq, q_col)

        alog_row = _pad_lanes(a_log[l][None, :])
        dsk_full = jnp.repeat(d_skip[l], SSM_HEAD_DIM)[None, :]
        x2 = _mix_tail(x2, proj, small, y_att_t, mod4, w_out[l].astype(BF16),
                       attn_norm_w[l][None, :], ln1_g[l][None, :], ln1_b[l][None, :],
                       w_ff_in[l].astype(BF16), w_ff_out[l].astype(BF16), ln2_g[l][None, :],
                       ln2_b[l][None, :], conv_w[l], conv_b[l][None, :], brow, alog_row, dsk_full,
                       ssm_norm_w[l][None, :], seq, alpha)
    return x2.reshape(batch, seq, d)
```

```python
import functools

import jax
import jax.numpy as jnp
import numpy as np
from jax import lax
from jax.experimental import pallas as pl
from jax.experimental.pallas import tpu as pltpu

F32 = jnp.float32
BF16 = jnp.bfloat16

LANES = 128
SUBLANES = 8
BF16_SUBLANES = 16
VMEM_LIMIT_BYTES = 56 * 1024 * 1024

SSM_HEADS = 16
SSM_HEAD_DIM = 64
SSM_GROUPS = 2
SSM_STATE = 128
CONV_WIDTH = 4
CHUNK = 128
ATT_HEADS = 16
ATT_HEAD_DIM = 64
LN_EPS = 1e-5
RMS_EPS = 1e-5

HEADS_PER_GROUP = SSM_HEADS // SSM_GROUPS
HEADS_PER_BLOCK = LANES // ATT_HEAD_DIM
SMALL_DT = 0
SMALL_F = SSM_HEADS
LOG2E = 1.4426950408889634
MASKED = -1e30
BIAS_PIECES = 3
BIAS_LANES_PER_HEAD = 2 * BIAS_PIECES


def _silu(v):
    return v * (1.0 / (1.0 + jnp.exp(-v)))


def _softplus(v):
    return jnp.maximum(v, 0.0) + jnp.log(1.0 + jnp.exp(-jnp.abs(v)))


def _split3(c):
    hi = c.astype(BF16).astype(F32)
    r = c - hi
    mid = r.astype(BF16).astype(F32)
    return hi, mid, r - mid


def _pieces(c):
    return jnp.concatenate(_split3(c), axis=1).astype(BF16)


def _layer_norm(v, g, b):
    mu = jnp.mean(v, axis=-1, keepdims=True)
    d = v - mu
    var = jnp.mean(d * d, axis=-1, keepdims=True)
    return d * lax.rsqrt(var + LN_EPS) * g + b


def _ada_kernel(c_ref, w_ref, b_ref, o_ref):
    ca = _silu(c_ref[...]).astype(BF16)
    o_ref[...] = jnp.dot(ca, w_ref[...].astype(BF16), preferred_element_type=F32) + b_ref[...]


def _ada(c_pad, w, b, tn=2048):
    rows, d = c_pad.shape
    n = w.shape[1]
    return pl.pallas_call(
        _ada_kernel,
        grid=(n // tn,),
        in_specs=[pl.BlockSpec((rows, d), lambda j: (0, 0)),
                  pl.BlockSpec((d, tn), lambda j: (0, j)),
                  pl.BlockSpec((1, tn), lambda j: (0, j))],
        out_specs=pl.BlockSpec((rows, tn), lambda j: (0, j)),
        out_shape=jax.ShapeDtypeStruct((rows, n), F32),
        compiler_params=pltpu.CompilerParams(
            dimension_semantics=("arbitrary",), vmem_limit_bytes=VMEM_LIMIT_BYTES),
        name="ada",
    )(c_pad, w, b)


def _inproj_kernel(x_ref, mod_ref, wt_ref, wst_ref, brow_ref, place_ref, ones_ref, o_ref, os_ref,
                   oqt_ref, ovt_ref, fq_ref, fk_ref, car_ref, *, tn, tiles_per_seq, n_main, gap_at,
                   gap, z_hi, q_lo, q_hi, v_lo, q_scale):
    nt = (((1,), (1,)), ((), ()))
    tm = x_ref.shape[0]

    @pl.when(pl.program_id(0) == 0)
    def _():
        car_ref[...] = jnp.zeros(car_ref.shape, F32)

    h = (x_ref[...] * (1.0 + mod_ref[1]) + mod_ref[0]).astype(BF16)
    small = lax.dot_general(h, wst_ref[...], nt, preferred_element_type=F32)
    os_ref[...] = small

    lane = lax.broadcasted_iota(jnp.int32, (tm, LANES), 1)
    is_f = jnp.logical_and(lane >= SMALL_F, lane < SMALL_F + ATT_HEADS)
    log_f = jnp.where(is_f, -_softplus(-(small + brow_ref[...])), 0.0)
    tril = (lax.broadcasted_iota(jnp.int32, (CHUNK, CHUNK), 0)
            >= lax.broadcasted_iota(jnp.int32, (CHUNK, CHUNK), 1)).astype(BF16)
    seq_start = pl.program_id(0) % tiles_per_seq == 0
    carry = jnp.where(seq_start, 0.0, car_ref[...])
    cums = []
    for r0 in range(0, tm, CHUNK):
        cs3 = jnp.dot(tril, _pieces(log_f[r0:r0 + CHUNK, :]), preferred_element_type=F32)
        cums.append(cs3[:, 0:LANES] + cs3[:, LANES:2 * LANES] + cs3[:, 2 * LANES:] + carry)
        carry = cums[-1][CHUNK - 1:CHUNK, :]
    car_ref[...] = carry
    cum = jnp.concatenate(cums, axis=0)
    feats = jnp.dot(_pieces(cum * LOG2E), place_ref[...], preferred_element_type=F32) + ones_ref[...]
    fq_ref[...] = feats[:, 0:LANES].T.astype(BF16)
    fk_ref[...] = feats[:, LANES:].astype(BF16)

    for c0 in range(0, n_main, tn):
        r0 = c0 if c0 < gap_at else c0 + gap
        acc = lax.dot_general(h, wt_ref[r0:r0 + tn, :], nt, preferred_element_type=F32)
        if c0 >= v_lo:
            ovt_ref[c0 - v_lo:c0 - v_lo + tn, :] = acc.T.astype(BF16)
        elif c0 >= q_hi:
            o_ref[:, c0 - (q_hi - q_lo):c0 - (q_hi - q_lo) + tn] = acc.astype(BF16)
        elif c0 >= q_lo:
            oqt_ref[c0 - q_lo:c0 - q_lo + tn, :] = (acc * q_scale).T.astype(BF16)
        else:
            o_ref[:, c0:c0 + tn] = (_silu(acc) if c0 < z_hi else acc).astype(BF16)


def _inproj(x2, mod4, wt, wt_small, brow, batch, seq, n_main, gap_at, gap, z_hi, q_lo, q_hi, v_lo,
            q_scale, tm=1024, tn=512):
    n_tok, d = x2.shape
    assert seq % tm == 0 and tm % CHUNK == 0
    assert all(c % tn == 0 for c in (z_hi, q_lo, q_hi, v_lo, n_main, gap_at))
    assert gap % BF16_SUBLANES == 0
    tiles_per_seq = seq // tm
    place, ones_row = _bias_routing()
    const = lambda i: (0, 0)
    tok = lambda i: (i, 0)
    resident = functools.partial(pl.BlockSpec, index_map=const, pipeline_mode=pl.Buffered(1))
    n_rowmajor = v_lo - (q_hi - q_lo)
    by_channel = lambda i: (i // tiles_per_seq, 0, i % tiles_per_seq)
    return pl.pallas_call(
        functools.partial(_inproj_kernel, tn=tn, tiles_per_seq=tiles_per_seq, n_main=n_main,
                          gap_at=gap_at, gap=gap, z_hi=z_hi, q_lo=q_lo, q_hi=q_hi, v_lo=v_lo,
                          q_scale=q_scale),
        grid=(n_tok // tm,),
        in_specs=[pl.BlockSpec((tm, d), tok),
                  pl.BlockSpec((None, 6, 1, d), lambda i: (i // tiles_per_seq, 0, 0, 0)),
                  resident(wt.shape),
                  resident((LANES, d)),
                  pl.BlockSpec((1, LANES), const),
                  pl.BlockSpec(place.shape, const),
                  pl.BlockSpec(ones_row.shape, const)],
        out_specs=[pl.BlockSpec((tm, n_rowmajor), tok),
                   pl.BlockSpec((tm, LANES), tok),
                   pl.BlockSpec((None, q_hi - q_lo, tm), by_channel),
                   pl.BlockSpec((None, n_main - v_lo, tm), by_channel),
                   pl.BlockSpec((None, LANES, tm), by_channel),
                   pl.BlockSpec((tm, LANES), tok)],
        out_shape=[jax.ShapeDtypeStruct((n_tok, n_rowmajor), BF16),
                   jax.ShapeDtypeStruct((n_tok, LANES), F32),
                   jax.ShapeDtypeStruct((batch, q_hi - q_lo, seq), BF16),
                   jax.ShapeDtypeStruct((batch, n_main - v_lo, seq), BF16),
                   jax.ShapeDtypeStruct((batch, LANES, seq), BF16),
                   jax.ShapeDtypeStruct((n_tok, LANES), BF16)],
        scratch_shapes=[pltpu.VMEM((1, LANES), F32)],
        compiler_params=pltpu.CompilerParams(
            dimension_semantics=("arbitrary",), vmem_limit_bytes=VMEM_LIMIT_BYTES),
        name="inproj",
    )(x2, mod4, wt, wt_small, brow, place, ones_row)


def _ssd_rows(gate, conv_silu, small, seq_start, brow_ref, alog_ref, dsk_ref, nw_ref, st_ref):
    width = SSM_HEADS * SSM_HEAD_DIM
    gw = HEADS_PER_GROUP * SSM_HEAD_DIM
    u = conv_silu()
    xs = u[:, 0:width]
    b_all = u[:, width:width + SSM_GROUPS * SSM_STATE]
    c_all = u[:, width + SSM_GROUPS * SSM_STATE:]

    lane = lax.broadcasted_iota(jnp.int32, (CHUNK, LANES), 1)
    row = lax.broadcasted_iota(jnp.int32, (CHUNK, LANES), 0)
    is_dt = lane < SMALL_F
    sp = _softplus(small + brow_ref[...])
    a_row = -jnp.exp(alog_ref[...])
    dt_tile = jnp.where(is_dt, sp, 0.0)
    val = jnp.where(is_dt, sp * a_row, 0.0)
    tril = row >= lane
    cs3 = jnp.dot(tril.astype(BF16), _pieces(val), preferred_element_type=F32)
    yield
    cs = cs3[:, 0:LANES] + cs3[:, LANES:2 * LANES] + cs3[:, 2 * LANES:]

    cs_t = cs.T
    dt_t = dt_tile.T
    lane_lo = lane < SSM_HEAD_DIM

    def per_head_lanes(tile):
        cols = [jnp.broadcast_to(tile[:, SMALL_DT + h:SMALL_DT + h + 1], (CHUNK, LANES))
                for h in range(SSM_HEADS)]
        blocks = [jnp.where(lane_lo, cols[h], cols[h + 1]) for h in range(0, SSM_HEADS, 2)]
        return cols, jnp.concatenate(blocks, axis=1)

    cs_cols, acs_full = per_head_lanes(cs)
    _, dt_full = per_head_lanes(dt_tile)
    atot_full = acs_full[CHUNK - 1:CHUNK, :]
    dec_out = jnp.exp(acs_full)
    dec_end = jnp.exp(atot_full - acs_full)
    chunk_dec = jnp.exp(atot_full)
    xs_b = xs.astype(BF16)
    xw = (xs * (dec_end * dt_full)).astype(BF16)

    bgs = [b_all[:, g * SSM_STATE:(g + 1) * SSM_STATE] for g in range(SSM_GROUPS)]
    cgs = [c_all[:, g * SSM_STATE:(g + 1) * SSM_STATE].astype(BF16) for g in range(SSM_GROUPS)]
    cbms = [lax.dot_general(cg, bg.astype(BF16), (((1,), (1,)), ((), ())),
                            preferred_element_type=F32) for cg, bg in zip(cgs, bgs)]
    yield
    ydiag_blocks = []
    yoff_blocks = []
    for g in range(SSM_GROUPS):
        bg, cg, cbm = bgs[g], cgs[g], cbms[g]
        for j in range(HEADS_PER_GROUP // HEADS_PER_BLOCK):
            h0 = g * HEADS_PER_GROUP + HEADS_PER_BLOCK * j
            xp = xs_b[:, h0 * SSM_HEAD_DIM:h0 * SSM_HEAD_DIM + LANES]
            scs = []
            for h in range(h0, h0 + HEADS_PER_BLOCK):
                seg = cs_cols[h] - cs_t[h:h + 1, :]
                dec = jnp.exp(jnp.where(tril, seg, MASKED))
                scs.append((cbm * dec * dt_t[h:h + 1, :]).astype(BF16))
            zero = jnp.zeros_like(xp)
            xp2 = jnp.concatenate([jnp.where(lane_lo, xp, zero), jnp.where(lane_lo, zero, xp)], axis=0)
            ydiag_blocks.append(jnp.dot(jnp.concatenate(scs, axis=1), xp2,
                                        preferred_element_type=F32))
        st = st_ref[g]
        if seq_start is not None:
            st = jnp.where(seq_start, 0.0, st)
        yoff_blocks.append(jnp.dot(cg, st.astype(BF16), preferred_element_type=F32)
                           * dec_out[:, g * gw:(g + 1) * gw])
        st_ref[g] = st * chunk_dec[:, g * gw:(g + 1) * gw] + jnp.dot(
            bg.T.astype(BF16), xw[:, g * gw:(g + 1) * gw], preferred_element_type=F32)
    y = (jnp.concatenate(ydiag_blocks, axis=1) + jnp.concatenate(yoff_blocks, axis=1)
         + dsk_ref[...] * xs)
    gated = y * gate.astype(F32)
    ms = jnp.mean(gated * gated, axis=-1, keepdims=True)
    yield (gated * lax.rsqrt(ms + RMS_EPS) * nw_ref[...]).astype(BF16)


def _bias_routing():
    place = np.zeros((BIAS_PIECES * LANES, 2 * LANES), np.float32)
    ones_row = np.zeros((1, 2 * LANES), np.float32)
    for h in range(ATT_HEADS):
        base = BIAS_LANES_PER_HEAD * h
        for t in range(BIAS_PIECES):
            place[t * LANES + SMALL_F + h, base + t] = 1.0
            place[t * LANES + SMALL_F + h, LANES + base + BIAS_PIECES + t] = -1.0
            ones_row[0, base + BIAS_PIECES + t] = 1.0
            ones_row[0, LANES + base + t] = 1.0
    return jnp.asarray(place, BF16), jnp.asarray(ones_row, F32)


ATT_TK = 256
ATT_SUB = 256
ATT_UNROLL = 8
ATT_VT_ROWS = ATT_HEAD_DIM + BF16_SUBLANES


def _attn_kernel(q_ref, k_ref, vt_ref, fq_ref, fk_ref, o_ref, st_ref, mx_ref, acc_ref, m_ref, *, tq):
    pair = pl.program_id(1)
    qi = pl.program_id(2)
    n_sub = tq // ATT_SUB
    assert n_sub % ATT_UNROLL == 0
    hd = ATT_HEAD_DIM
    ones_rows = jnp.ones((ATT_VT_ROWS - hd, ATT_TK), BF16)

    chan = lax.broadcasted_iota(jnp.int32, (LANES, tq), 0)
    q = q_ref[...]
    fq = fq_ref[...]
    q_aug = []
    for i in range(HEADS_PER_BLOCK):
        first = BIAS_LANES_PER_HEAD * (HEADS_PER_BLOCK * pair + i)
        own_q = (chan < hd) if i == 0 else (chan >= hd)
        own_f = jnp.logical_and(chan >= first, chan < first + BIAS_LANES_PER_HEAD)
        q_aug.append(jnp.concatenate([jnp.where(own_q, q, jnp.zeros_like(q)),
                                      jnp.where(own_f, fq, jnp.zeros_like(fq))], axis=0))
    causal = (lax.broadcasted_iota(jnp.int32, (ATT_TK, ATT_SUB), 0)
              <= lax.broadcasted_iota(jnp.int32, (ATT_TK, ATT_SUB), 1))
    chains = [(i, s) for i in range(HEADS_PER_BLOCK) for s in range(n_sub)]

    def head_scores(i, block, s_from=0):
        rows = pl.ds(pl.multiple_of(block * ATT_TK, ATT_TK), ATT_TK)
        kb = jnp.concatenate([k_ref[rows, :], fk_ref[rows, :]], axis=1)
        return jnp.dot(kb, q_aug[i][:, s_from * ATT_SUB:], preferred_element_type=F32)

    def update(c, block, st, st_max):
        m = m_ref[c]
        m_new = jnp.maximum(m, st_max)
        pt = jnp.exp2(st - m_new).astype(BF16)
        start = pl.multiple_of(block * ATT_TK, ATT_TK)
        i = chains[c][0]
        vt = jnp.concatenate([vt_ref[i * hd:(i + 1) * hd, pl.ds(start, ATT_TK)], ones_rows], axis=0)
        acc_ref[c] = (jnp.exp2(m - m_new) * acc_ref[c]
                      + jnp.dot(vt, pt, preferred_element_type=F32))
        m_ref[c] = m_new

    def stash(slot, block):
        for i in range(HEADS_PER_BLOCK):
            st = head_scores(i, block)
            for s in range(n_sub):
                part = st[:, s * ATT_SUB:(s + 1) * ATT_SUB]
                st_ref[slot, i * n_sub + s] = part
                mx_ref[slot, i * n_sub + s] = jnp.max(part, axis=0, keepdims=True)

    every = list(range(len(chains)))
    for c in every:
        m_ref[c] = jnp.full((1, ATT_SUB), MASKED, F32)
        acc_ref[c] = jnp.zeros((ATT_VT_ROWS, ATT_SUB), F32)

    first_diag = qi * n_sub
    diag = []
    for e in range(n_sub):
        for i in range(HEADS_PER_BLOCK):
            st = head_scores(i, first_diag + e, s_from=e)
            for s in range(e, n_sub):
                part = st[:, (s - e) * ATT_SUB:(s - e + 1) * ATT_SUB]
                if s == e:
                    part = jnp.where(causal, part, MASKED)
                diag.append((i * n_sub + s, first_diag + e, part))
    stash(0, 0)
    for c, block, part in diag:
        update(c, block, part, jnp.max(part, axis=0, keepdims=True))

    def half_step(block, slot, prefetch=True):
        if prefetch:
            stash(1 - slot, block + 1)
        for c in every:
            update(c, block, st_ref[slot, c], mx_ref[slot, c])

    def body(j, carry):
        for t in range(ATT_UNROLL):
            half_step(ATT_UNROLL * j + t, t % 2)
        return carry

    lax.fori_loop(0, qi * (n_sub // ATT_UNROLL) - 1, body, 0)

    @pl.when(qi > 0)
    def _():
        for t in range(ATT_UNROLL):
            half_step(first_diag - ATT_UNROLL + t, t % 2, prefetch=t < ATT_UNROLL - 1)

    heads = [[None] * n_sub for _ in range(HEADS_PER_BLOCK)]
    for c, (i, s) in enumerate(chains):
        acc = acc_ref[c]
        heads[i][s] = acc[0:hd, :] * (1.0 / acc[hd:hd + 1, :])
    o_t = jnp.concatenate([jnp.concatenate(h, axis=1) for h in heads], axis=0)
    o_ref[...] = o_t.astype(BF16)


def _attn(proj, q_t, v_t, feat_q_t, feat_k, batch, seq, k_col, tq=2048):
    assert seq % tq == 0 and tq % (ATT_SUB * ATT_UNROLL) == 0
    nq = seq // tq
    n_pairs = ATT_HEADS // HEADS_PER_BLOCK
    n_chains = HEADS_PER_BLOCK * (tq // ATT_SUB)
    return pl.pallas_call(
        functools.partial(_attn_kernel, tq=tq),
        grid=(batch, n_pairs, nq),
        in_specs=[pl.BlockSpec((None, LANES, tq), lambda b, p, i: (b, p, i)),
                  pl.BlockSpec((seq, LANES), lambda b, p, i: (b, k_col + p)),
                  pl.BlockSpec((None, LANES, seq), lambda b, p, i: (b, p, 0)),
                  pl.BlockSpec((None, LANES, tq), lambda b, p, i: (b, 0, i)),
                  pl.BlockSpec((seq, LANES), lambda b, p, i: (b, 0))],
        out_specs=pl.BlockSpec((None, LANES, tq), lambda b, p, i: (b, p, i)),
        out_shape=jax.ShapeDtypeStruct((batch, ATT_HEADS * ATT_HEAD_DIM, seq), BF16),
        scratch_shapes=[pltpu.VMEM((2, n_chains, ATT_TK, ATT_SUB), F32),
                        pltpu.VMEM((2, n_chains, 1, ATT_SUB), F32),
                        pltpu.VMEM((n_chains, ATT_VT_ROWS, ATT_SUB), F32),
                        pltpu.VMEM((n_chains, 1, ATT_SUB), F32)],
        compiler_params=pltpu.CompilerParams(
            dimension_semantics=("arbitrary", "arbitrary", "arbitrary"),
            vmem_limit_bytes=VMEM_LIMIT_BYTES),
        name="attn",
    )(q_t, proj, v_t, feat_q_t, feat_k)


def _mix_tail_kernel(x_ref, ya_ref, mod_ref, wo_ref, anw_ref, g1_ref, b1_ref, w1_ref, w2_ref, g2_ref,
                     b2_ref, gate_ref, xs_ref, bc_ref, sm_ref, xh_ref, bch_ref, cw_ref, cb_ref,
                     brow_ref, alog_ref, dsk_ref, nw_ref, o_ref, ys_ref, st_ref, xf_ref, *,
                     alpha, ff_chunk, tiles_per_seq, n_tiles):
    i = pl.program_id(0)
    tm = x_ref.shape[0]
    slot = i % 2
    n_chunks = tm // CHUNK
    n_ff = w1_ref.shape[1] // ff_chunk
    assert n_chunks == n_ff
    chunk_rows = [slice(c * CHUNK, (c + 1) * CHUNK) for c in range(n_chunks)]
    width = xs_ref.shape[1]
    halo = xh_ref.shape[0]

    def step(with_ssd, with_mlp):
        ssd = []
        if with_ssd:
            seq_start = i % tiles_per_seq == 0
            before = jnp.concatenate([xh_ref[...], bch_ref[...]], axis=1).astype(F32)
            xf_ref[0:halo, :] = jnp.where(seq_start, 0.0, before)
            xf_ref[halo:, 0:width] = xs_ref[...].astype(F32)
            xf_ref[halo:, width:] = bc_ref[...].astype(F32)

            def conv_silu(c):
                conv = cb_ref[...]
                for k in range(CONV_WIDTH):
                    first = halo + c * CHUNK - k
                    conv = conv + (cw_ref[CONV_WIDTH - 1 - k:CONV_WIDTH - k, :]
                                   * xf_ref[first:first + CHUNK, :])
                return _silu(conv)

            ssd = [_ssd_rows(gate_ref[rows, :], functools.partial(conv_silu, c), sm_ref[rows, :],
                             seq_start if c == 0 else None, brow_ref, alog_ref, dsk_ref, nw_ref,
                             st_ref) for c, rows in enumerate(chunk_rows)]

        def advance(c, last=False):
            if 0 <= c < len(ssd):
                out = next(ssd[c])
                if last:
                    ys_ref[slot, chunk_rows[c], :] = out

        advance(0)
        if with_mlp:
            ssm_w = ys_ref.shape[2]
            ya = ya_ref[...].astype(F32).T
            ya = ya * lax.rsqrt(jnp.mean(ya * ya, axis=-1, keepdims=True) + RMS_EPS) * anw_ref[...]
            y = (jnp.dot(ys_ref[1 - slot], wo_ref[0:ssm_w, :], preferred_element_type=F32)
                 + jnp.dot(ya.astype(BF16), wo_ref[ssm_w:, :], preferred_element_type=F32))
            x1 = _layer_norm(alpha * x_ref[...] + (1.0 + mod_ref[2]) * y, g1_ref[...], b1_ref[...])
            h = (x1 * (1.0 + mod_ref[4]) + mod_ref[3]).astype(BF16)
            ff = jnp.zeros(x1.shape, F32)
        advance(0)
        for c in range(n_ff):
            if with_mlp:
                a = jnp.dot(h, w1_ref[:, c * ff_chunk:(c + 1) * ff_chunk],
                            preferred_element_type=F32)
                a = jnp.maximum(a, 0.0)
            advance(c, last=True)
            advance(c + 1)
            if with_mlp:
                ff = ff + jnp.dot((a * a).astype(BF16), w2_ref[c * ff_chunk:(c + 1) * ff_chunk, :],
                                  preferred_element_type=F32)
            advance(c + 1)
        if with_mlp:
            o_ref[...] = _layer_norm(alpha * x1 + (1.0 + mod_ref[5]) * ff, g2_ref[...], b2_ref[...])

    @pl.when(i == 0)
    def _():
        st_ref[...] = jnp.zeros(st_ref.shape, F32)
        step(True, False)

    @pl.when(jnp.logical_and(i > 0, i < n_tiles))
    def _():
        step(True, True)

    @pl.when(i == n_tiles)
    def _():
        step(False, True)


def _mix_tail(x2, proj, small, y_att_t, mod4, w_out, anw, g1, b1, w1, w2, g2, b2, conv_w, conv_b,
              brow, alog_row, dsk_full, ssm_norm_w, seq, alpha, tm=512, ff_chunk=1024):
    n_tok, d = x2.shape
    mix = w_out.shape[0]
    d_ff = w1.shape[1]
    width = SSM_HEADS * SSM_HEAD_DIM
    bc_w = 2 * SSM_GROUPS * SSM_STATE
    x_col, bc_col = 1, 2 * width // bc_w
    assert seq % tm == 0 and tm % CHUNK == 0 and d_ff % ff_chunk == 0
    tiles_per_seq = seq // tm
    n_tiles = n_tok // tm
    const = lambda i: (0, 0)
    resident = functools.partial(pl.BlockSpec, index_map=const, pipeline_mode=pl.Buffered(1))
    mlp_tile = lambda i: jnp.maximum(i - 1, 0)
    ssd_tile = lambda i: jnp.minimum(i, n_tiles - 1)
    halo = lambda i: jnp.maximum(ssd_tile(i) * (tm // BF16_SUBLANES) - 1, 0)
    return pl.pallas_call(
        functools.partial(_mix_tail_kernel, alpha=alpha, ff_chunk=ff_chunk,
                          tiles_per_seq=tiles_per_seq, n_tiles=n_tiles),
        grid=(n_tiles + 1,),
        in_specs=[pl.BlockSpec((tm, d), lambda i: (mlp_tile(i), 0)),
                  pl.BlockSpec((None, y_att_t.shape[1], tm),
                               lambda i: (mlp_tile(i) // tiles_per_seq, 0,
                                          mlp_tile(i) % tiles_per_seq)),
                  pl.BlockSpec((None, 6, 1, d), lambda i: (mlp_tile(i) // tiles_per_seq, 0, 0, 0)),
                  resident((mix, d)),
                  pl.BlockSpec((1, y_att_t.shape[1]), const),
                  pl.BlockSpec((1, d), const),
                  pl.BlockSpec((1, d), const),
                  resident((d, d_ff)),
                  resident((d_ff, d)),
                  pl.BlockSpec((1, d), const),
                  pl.BlockSpec((1, d), const),
                  pl.BlockSpec((tm, width), lambda i: (ssd_tile(i), 0)),
                  pl.BlockSpec((tm, width), lambda i: (ssd_tile(i), x_col)),
                  pl.BlockSpec((tm, bc_w), lambda i: (ssd_tile(i), bc_col)),
                  pl.BlockSpec((tm, LANES), lambda i: (ssd_tile(i), 0)),
                  pl.BlockSpec((BF16_SUBLANES, width), lambda i: (halo(i), x_col)),
                  pl.BlockSpec((BF16_SUBLANES, bc_w), lambda i: (halo(i), bc_col)),
                  pl.BlockSpec(conv_w.shape, const),
                  pl.BlockSpec(conv_b.shape, const),
                  pl.BlockSpec((1, LANES), const),
                  pl.BlockSpec((1, LANES), const),
                  pl.BlockSpec((1, width), const),
                  pl.BlockSpec((1, width), const)],
        out_specs=pl.BlockSpec((tm, d), lambda i: (mlp_tile(i), 0)),
        out_shape=jax.ShapeDtypeStruct((n_tok, d), F32),
        scratch_shapes=[pltpu.VMEM((2, tm, width), BF16),
                        pltpu.VMEM((SSM_GROUPS, SSM_STATE, HEADS_PER_GROUP * SSM_HEAD_DIM), F32),
                        pltpu.VMEM((BF16_SUBLANES + tm, width + bc_w), F32)],
        compiler_params=pltpu.CompilerParams(
            dimension_semantics=("arbitrary",), vmem_limit_bytes=VMEM_LIMIT_BYTES),
        name="mix_tail",
    )(x2, y_att_t, mod4, w_out, anw, g1, b1, w1, w2, g2, b2, proj, proj, proj, small, proj, proj,
      conv_w, conv_b, brow, alog_row, dsk_full, ssm_norm_w)


def _pad_lanes(v):
    return jnp.pad(v, ((0, 0), (0, LANES - v.shape[1])))


def kernel(x, c, w_ada, b_ada, w_in, conv_w, conv_b, dt_bias, a_log, d_skip, ssm_norm_w, f_bias,
           attn_norm_w, w_out, ln1_g, ln1_b, w_ff_in, w_ff_out, ln2_g, ln2_b):
    batch, seq, d = x.shape
    depth = w_ada.shape[0]
    alpha = (2.0 * depth) ** 0.25
    ssm_w = SSM_HEADS * SSM_HEAD_DIM
    att_w = ATT_HEADS * ATT_HEAD_DIM
    conv_dim = ssm_w + 2 * SSM_GROUPS * SSM_STATE
    o_xbc = ssm_w
    o_dt = o_xbc + conv_dim
    o_q = o_dt + SSM_HEADS
    o_k = o_q + att_w
    o_v = o_k + att_w
    o_f = o_v + att_w
    q_col = (ssm_w + conv_dim) // LANES
    k_col = q_col + att_w // LANES
    v_col = k_col + att_w // LANES

    x2 = x.reshape(batch * seq, d)
    c_pad = jnp.pad(c, ((0, SUBLANES - batch % SUBLANES if batch % SUBLANES else 0), (0, 0)))
    for l in range(depth):
        mod = _ada(c_pad, w_ada[l], b_ada[l][None, :])
        mod4 = mod[:batch].reshape(batch, 6, 1, d)
        wt = jnp.swapaxes(w_in[l], 0, 1).astype(BF16)
        wt_small = jnp.concatenate([wt[o_dt:o_q], wt[o_f:]], axis=0)
        wt_small = jnp.pad(wt_small, ((0, LANES - wt_small.shape[0]), (0, 0)))
        brow = _pad_lanes(jnp.concatenate([dt_bias[l], f_bias[l]])[None, :])
        proj, small, q_t, v_t, feat_q_t, feat_k = _inproj(
            x2, mod4, wt, wt_small, brow, batch, seq, v_col * LANES + att_w, o_dt, o_q - o_dt,
            ssm_w, q_col * LANES, k_col * LANES, v_col * LANES, ATT_HEAD_DIM ** -0.5 * LOG2E)
        y_att_t = _attn(proj, q_t, v_t, feat_q_t, feat_k, batch, seq, q_col)

        alog_row = _pad_lanes(a_log[l][None, :])
        dsk_full = jnp.repeat(d_skip[l], SSM_HEAD_DIM)[None, :]
        x2 = _mix_tail(x2, proj, small, y_att_t, mod4, w_out[l].astype(BF16),
                       attn_norm_w[l][None, :], ln1_g[l][None, :], ln1_b[l][None, :],
                       w_ff_in[l].astype(BF16), w_ff_out[l].astype(BF16), ln2_g[l][None, :],
                       ln2_b[l][None, :], conv_w[l], conv_b[l][None, :], brow, alog_row, dsk_full,
                       ssm_norm_w[l][None, :], seq, alpha)
    return x2.reshape(batch, seq, d)
```

```python
import functools

import jax
import jax.numpy as jnp
import numpy as np
from jax import lax
from jax.experimental import pallas as pl
from jax.experimental.pallas import tpu as pltpu

F32 = jnp.float32
BF16 = jnp.bfloat16

LANES = 128
SUBLANES = 8
BF16_SUBLANES = 16
VMEM_LIMIT_BYTES = 56 * 1024 * 1024

SSM_HEADS = 16
SSM_HEAD_DIM = 64
SSM_GROUPS = 2
SSM_STATE = 128
CONV_WIDTH = 4
CHUNK = 128
ATT_HEADS = 16
ATT_HEAD_DIM = 64
LN_EPS = 1e-5
RMS_EPS = 1e-5

HEADS_PER_GROUP = SSM_HEADS // SSM_GROUPS
HEADS_PER_BLOCK = LANES // ATT_HEAD_DIM
SMALL_DT = 0
SMALL_F = SSM_HEADS
LOG2E = 1.4426950408889634
MASKED = -1e30
BIAS_PIECES = 3
BIAS_LANES_PER_HEAD = 2 * BIAS_PIECES


def _silu(v):
    return v * (1.0 / (1.0 + jnp.exp(-v)))


def _softplus(v):
    return jnp.maximum(v, 0.0) + jnp.log(1.0 + jnp.exp(-jnp.abs(v)))


def _split3(c):
    hi = c.astype(BF16).astype(F32)
    r = c - hi
    mid = r.astype(BF16).astype(F32)
    return hi, mid, r - mid


def _pieces(c):
    return jnp.concatenate(_split3(c), axis=1).astype(BF16)


def _layer_norm(v, g, b):
    mu = jnp.mean(v, axis=-1, keepdims=True)
    d = v - mu
    var = jnp.mean(d * d, axis=-1, keepdims=True)
    return d * lax.rsqrt(var + LN_EPS) * g + b


def _ada_kernel(c_ref, w_ref, b_ref, o_ref):
    ca = _silu(c_ref[...]).astype(BF16)
    o_ref[...] = jnp.dot(ca, w_ref[...].astype(BF16), preferred_element_type=F32) + b_ref[...]


def _ada(c_pad, w, b, tn=2048):
    rows, d = c_pad.shape
    n = w.shape[1]
    return pl.pallas_call(
        _ada_kernel,
        grid=(n // tn,),
        in_specs=[pl.BlockSpec((rows, d), lambda j: (0, 0)),
                  pl.BlockSpec((d, tn), lambda j: (0, j)),
                  pl.BlockSpec((1, tn), lambda j: (0, j))],
        out_specs=pl.BlockSpec((rows, tn), lambda j: (0, j)),
        out_shape=jax.ShapeDtypeStruct((rows, n), F32),
        compiler_params=pltpu.CompilerParams(
            dimension_semantics=("arbitrary",), vmem_limit_bytes=VMEM_LIMIT_BYTES),
        name="ada",
    )(c_pad, w, b)


def _inproj_kernel(x_ref, mod_ref, wt_ref, wst_ref, brow_ref, place_ref, ones_ref, o_ref, os_ref,
                   oqt_ref, ovt_ref, fq_ref, fk_ref, car_ref, *, tn, tiles_per_seq, n_main, gap_at,
                   gap, z_hi, q_lo, q_hi, v_lo, q_scale):
    nt = (((1,), (1,)), ((), ()))
    tm = x_ref.shape[0]

    @pl.when(pl.program_id(0) == 0)
    def _():
        car_ref[...] = jnp.zeros(car_ref.shape, F32)

    h = (x_ref[...] * (1.0 + mod_ref[1]) + mod_ref[0]).astype(BF16)
    small = lax.dot_general(h, wst_ref[...], nt, preferred_element_type=F32)
    os_ref[...] = small

    lane = lax.broadcasted_iota(jnp.int32, (tm, LANES), 1)
    is_f = jnp.logical_and(lane >= SMALL_F, lane < SMALL_F + ATT_HEADS)
    log_f = jnp.where(is_f, -_softplus(-(small + brow_ref[...])), 0.0)
    tril = (lax.broadcasted_iota(jnp.int32, (CHUNK, CHUNK), 0)
            >= lax.broadcasted_iota(jnp.int32, (CHUNK, CHUNK), 1)).astype(BF16)
    seq_start = pl.program_id(0) % tiles_per_seq == 0
    carry = jnp.where(seq_start, 0.0, car_ref[...])
    cums = []
    for r0 in range(0, tm, CHUNK):
        cs3 = jnp.dot(tril, _pieces(log_f[r0:r0 + CHUNK, :]), preferred_element_type=F32)
        cums.append(cs3[:, 0:LANES] + cs3[:, LANES:2 * LANES] + cs3[:, 2 * LANES:] + carry)
        carry = cums[-1][CHUNK - 1:CHUNK, :]
    car_ref[...] = carry
    cum = jnp.concatenate(cums, axis=0)
    feats = jnp.dot(_pieces(cum * LOG2E), place_ref[...], preferred_element_type=F32) + ones_ref[...]
    fq_ref[...] = feats[:, 0:LANES].T.astype(BF16)
    fk_ref[...] = feats[:, LANES:].astype(BF16)

    for c0 in range(0, n_main, tn):
        r0 = c0 if c0 < gap_at else c0 + gap
        acc = lax.dot_general(h, wt_ref[r0:r0 + tn, :], nt, preferred_element_type=F32)
        if c0 >= v_lo:
            ovt_ref[c0 - v_lo:c0 - v_lo + tn, :] = acc.T.astype(BF16)
        elif c0 >= q_hi:
            o_ref[:, c0 - (q_hi - q_lo):c0 - (q_hi - q_lo) + tn] = acc.astype(BF16)
        elif c0 >= q_lo:
            oqt_ref[c0 - q_lo:c0 - q_lo + tn, :] = (acc * q_scale).T.astype(BF16)
        else:
            o_ref[:, c0:c0 + tn] = (_silu(acc) if c0 < z_hi else acc).astype(BF16)


def _inproj(x2, mod4, wt, wt_small, brow, batch, seq, n_main, gap_at, gap, z_hi, q_lo, q_hi, v_lo,
            q_scale, tm=1024, tn=512):
    n_tok, d = x2.shape
    assert seq % tm == 0 and tm % CHUNK == 0
    assert all(c % tn == 0 for c in (z_hi, q_lo, q_hi, v_lo, n_main, gap_at))
    assert gap % BF16_SUBLANES == 0
    tiles_per_seq = seq // tm
    place, ones_row = _bias_routing()
    const = lambda i: (0, 0)
    tok = lambda i: (i, 0)
    resident = functools.partial(pl.BlockSpec, index_map=const, pipeline_mode=pl.Buffered(1))
    n_rowmajor = v_lo - (q_hi - q_lo)
    by_channel = lambda i: (i // tiles_per_seq, 0, i % tiles_per_seq)
    return pl.pallas_call(
        functools.partial(_inproj_kernel, tn=tn, tiles_per_seq=tiles_per_seq, n_main=n_main,
                          gap_at=gap_at, gap=gap, z_hi=z_hi, q_lo=q_lo, q_hi=q_hi, v_lo=v_lo,
                          q_scale=q_scale),
        grid=(n_tok // tm,),
        in_specs=[pl.BlockSpec((tm, d), tok),
                  pl.BlockSpec((None, 6, 1, d), lambda i: (i // tiles_per_seq, 0, 0, 0)),
                  resident(wt.shape),
                  resident((LANES, d)),
                  pl.BlockSpec((1, LANES), const),
                  pl.BlockSpec(place.shape, const),
                  pl.BlockSpec(ones_row.shape, const)],
        out_specs=[pl.BlockSpec((tm, n_rowmajor), tok),
                   pl.BlockSpec((tm, LANES), tok),
                   pl.BlockSpec((None, q_hi - q_lo, tm), by_channel),
                   pl.BlockSpec((None, n_main - v_lo, tm), by_channel),
                   pl.BlockSpec((None, LANES, tm), by_channel),
                   pl.BlockSpec((tm, LANES), tok)],
        out_shape=[jax.ShapeDtypeStruct((n_tok, n_rowmajor), BF16),
                   jax.ShapeDtypeStruct((n_tok, LANES), F32),
                   jax.ShapeDtypeStruct((batch, q_hi - q_lo, seq), BF16),
                   jax.ShapeDtypeStruct((batch, n_main - v_lo, seq), BF16),
                   jax.ShapeDtypeStruct((batch, LANES, seq), BF16),
                   jax.ShapeDtypeStruct((n_tok, LANES), BF16)],
        scratch_shapes=[pltpu.VMEM((1, LANES), F32)],
        compiler_params=pltpu.CompilerParams(
            dimension_semantics=("arbitrary",), vmem_limit_bytes=VMEM_LIMIT_BYTES),
        name="inproj",
    )(x2, mod4, wt, wt_small, brow, place, ones_row)


def _ssd_rows(gate, conv_silu, small, seq_start, brow_ref, alog_ref, dsk_ref, nw_ref, st_ref):
    width = SSM_HEADS * SSM_HEAD_DIM
    gw = HEADS_PER_GROUP * SSM_HEAD_DIM
    u = conv_silu()
    xs = u[:, 0:width]
    b_all = u[:, width:width + SSM_GROUPS * SSM_STATE]
    c_all = u[:, width + SSM_GROUPS * SSM_STATE:]

    lane = lax.broadcasted_iota(jnp.int32, (CHUNK, LANES), 1)
    row = lax.broadcasted_iota(jnp.int32, (CHUNK, LANES), 0)
    is_dt = lane < SMALL_F
    sp = _softplus(small + brow_ref[...])
    a_row = -jnp.exp(alog_ref[...])
    dt_tile = jnp.where(is_dt, sp, 0.0)
    val = jnp.where(is_dt, sp * a_row, 0.0)
    tril = row >= lane
    cs3 = jnp.dot(tril.astype(BF16), _pieces(val), preferred_element_type=F32)
    yield
    cs = cs3[:, 0:LANES] + cs3[:, LANES:2 * LANES] + cs3[:, 2 * LANES:]

    cs_t = cs.T
    dt_t = dt_tile.T
    lane_lo = lane < SSM_HEAD_DIM

    def per_head_lanes(tile):
        cols = [jnp.broadcast_to(tile[:, SMALL_DT + h:SMALL_DT + h + 1], (CHUNK, LANES))
                for h in range(SSM_HEADS)]
        blocks = [jnp.where(lane_lo, cols[h], cols[h + 1]) for h in range(0, SSM_HEADS, 2)]
        return cols, jnp.concatenate(blocks, axis=1)

    cs_cols, acs_full = per_head_lanes(cs)
    _, dt_full = per_head_lanes(dt_tile)
    atot_full = acs_full[CHUNK - 1:CHUNK, :]
    dec_out = jnp.exp(acs_full)
    dec_end = jnp.exp(atot_full - acs_full)
    chunk_dec = jnp.exp(atot_full)
    xs_b = xs.astype(BF16)
    xw = (xs * (dec_end * dt_full)).astype(BF16)

    bgs = [b_all[:, g * SSM_STATE:(g + 1) * SSM_STATE] for g in range(SSM_GROUPS)]
    cgs = [c_all[:, g * SSM_STATE:(g + 1) * SSM_STATE].astype(BF16) for g in range(SSM_GROUPS)]
    cbms = [lax.dot_general(cg, bg.astype(BF16), (((1,), (1,)), ((), ())),
                            preferred_element_type=F32) for cg, bg in zip(cgs, bgs)]
    yield
    ydiag_blocks = []
    yoff_blocks = []
    for g in range(SSM_GROUPS):
        bg, cg, cbm = bgs[g], cgs[g], cbms[g]
        for j in range(HEADS_PER_GROUP // HEADS_PER_BLOCK):
            h0 = g * HEADS_PER_GROUP + HEADS_PER_BLOCK * j
            xp = xs_b[:, h0 * SSM_HEAD_DIM:h0 * SSM_HEAD_DIM + LANES]
            scs = []
            for h in range(h0, h0 + HEADS_PER_BLOCK):
                seg = cs_cols[h] - cs_t[h:h + 1, :]
                dec = jnp.exp(jnp.where(tril, seg, MASKED))
                scs.append((cbm * dec * dt_t[h:h + 1, :]).astype(BF16))
            zero = jnp.zeros_like(xp)
            xp2 = jnp.concatenate([jnp.where(lane_lo, xp, zero), jnp.where(lane_lo, zero, xp)], axis=0)
            ydiag_blocks.append(jnp.dot(jnp.concatenate(scs, axis=1), xp2,
                                        preferred_element_type=F32))
        st = st_ref[g]
        if seq_start is not None:
            st = jnp.where(seq_start, 0.0, st)
        yoff_blocks.append(jnp.dot(cg, st.astype(BF16), preferred_element_type=F32)
                           * dec_out[:, g * gw:(g + 1) * gw])
        st_ref[g] = st * chunk_dec[:, g * gw:(g + 1) * gw] + jnp.dot(
            bg.T.astype(BF16), xw[:, g * gw:(g + 1) * gw], preferred_element_type=F32)
    y = (jnp.concatenate(ydiag_blocks, axis=1) + jnp.concatenate(yoff_blocks, axis=1)
         + dsk_ref[...] * xs)
    gated = y * gate.astype(F32)
    ms = jnp.mean(gated * gated, axis=-1, keepdims=True)
    yield (gated * lax.rsqrt(ms + RMS_EPS) * nw_ref[...]).astype(BF16)


def _bias_routing():
    place = np.zeros((BIAS_PIECES * LANES, 2 * LANES), np.float32)
    ones_row = np.zeros((1, 2 * LANES), np.float32)
    for h in range(ATT_HEADS):
        base = BIAS_LANES_PER_HEAD * h
        for t in range(BIAS_PIECES):
            place[t * LANES + SMALL_F + h, base + t] = 1.0
            place[t * LANES + SMALL_F + h, LANES + base + BIAS_PIECES + t] = -1.0
            ones_row[0, base + BIAS_PIECES + t] = 1.0
            ones_row[0, LANES + base + t] = 1.0
    return jnp.asarray(place, BF16), jnp.asarray(ones_row, F32)


ATT_TK = 256
ATT_SUB = 256
ATT_UNROLL = 8
ATT_VT_ROWS = ATT_HEAD_DIM + BF16_SUBLANES


def _attn_kernel(q_ref, k_ref, vt_ref, fq_ref, fk_ref, o_ref, st_ref, mx_ref, acc_ref, m_ref, *, tq):
    pair = pl.program_id(1)
    n_sub = tq // ATT_SUB
    assert n_sub % ATT_UNROLL == 0
    hd = ATT_HEAD_DIM
    ones_rows = jnp.ones((ATT_VT_ROWS - hd, ATT_TK), BF16)
    chan = lax.broadcasted_iota(jnp.int32, (LANES, tq), 0)
    causal = (lax.broadcasted_iota(jnp.int32, (ATT_TK, ATT_SUB), 0)
              <= lax.broadcasted_iota(jnp.int32, (ATT_TK, ATT_SUB), 1))
    chains = [(i, s) for i in range(HEADS_PER_BLOCK) for s in range(n_sub)]
    every = list(range(len(chains)))

    def key_rows(block):
        start = block * ATT_TK
        if not isinstance(block, int):
            start = pl.multiple_of(start, ATT_TK)
        return pl.ds(start, ATT_TK)

    for qi in range(q_ref.shape[1] // tq):
        queries = slice(qi * tq, (qi + 1) * tq)
        q = q_ref[:, queries]
        fq = fq_ref[:, queries]
        q_aug = []
        for i in range(HEADS_PER_BLOCK):
            first = BIAS_LANES_PER_HEAD * (HEADS_PER_BLOCK * pair + i)
            own_q = (chan < hd) if i == 0 else (chan >= hd)
            own_f = jnp.logical_and(chan >= first, chan < first + BIAS_LANES_PER_HEAD)
            q_aug.append(jnp.concatenate([jnp.where(own_q, q, jnp.zeros_like(q)),
                                          jnp.where(own_f, fq, jnp.zeros_like(fq))], axis=0))

        def head_scores(i, block, s_from=0, q_aug=q_aug):
            rows = key_rows(block)
            kb = jnp.concatenate([k_ref[rows, :], fk_ref[rows, :]], axis=1)
            return jnp.dot(kb, q_aug[i][:, s_from * ATT_SUB:], preferred_element_type=F32)

        def update(c, block, st, st_max):
            m = m_ref[c]
            m_new = jnp.maximum(m, st_max)
            pt = jnp.exp2(st - m_new).astype(BF16)
            i = chains[c][0]
            vt = jnp.concatenate([vt_ref[i * hd:(i + 1) * hd, key_rows(block)], ones_rows], axis=0)
            acc_ref[c] = (jnp.exp2(m - m_new) * acc_ref[c]
                          + jnp.dot(vt, pt, preferred_element_type=F32))
            m_ref[c] = m_new

        def stash(slot, block, head_scores=head_scores):
            for i in range(HEADS_PER_BLOCK):
                st = head_scores(i, block)
                for s in range(n_sub):
                    part = st[:, s * ATT_SUB:(s + 1) * ATT_SUB]
                    st_ref[slot, i * n_sub + s] = part
                    mx_ref[slot, i * n_sub + s] = jnp.max(part, axis=0, keepdims=True)

        for c in every:
            m_ref[c] = jnp.full((1, ATT_SUB), MASKED, F32)
            acc_ref[c] = jnp.zeros((ATT_VT_ROWS, ATT_SUB), F32)

        first_diag = qi * n_sub
        diag = []
        for e in range(n_sub):
            for i in range(HEADS_PER_BLOCK):
                st = head_scores(i, first_diag + e, s_from=e)
                for s in range(e, n_sub):
                    part = st[:, (s - e) * ATT_SUB:(s - e + 1) * ATT_SUB]
                    if s == e:
                        part = jnp.where(causal, part, MASKED)
                    diag.append((i * n_sub + s, first_diag + e, part))
        if first_diag > 0:
            stash(0, 0)
        for c, block, part in diag:
            update(c, block, part, jnp.max(part, axis=0, keepdims=True))

        def half_step(block, slot, prefetch=True, stash=stash, update=update):
            if prefetch:
                stash(1 - slot, block + 1)
            for c in every:
                update(c, block, st_ref[slot, c], mx_ref[slot, c])

        def body(j, carry, half_step=half_step):
            for t in range(ATT_UNROLL):
                half_step(ATT_UNROLL * j + t, t % 2)
            return carry

        if first_diag > 0:
            groups = first_diag // ATT_UNROLL
            if groups > 1:
                lax.fori_loop(0, groups - 1, body, 0)
            for t in range(ATT_UNROLL):
                half_step(first_diag - ATT_UNROLL + t, t % 2, prefetch=t < ATT_UNROLL - 1)

        heads = [[None] * n_sub for _ in range(HEADS_PER_BLOCK)]
        for c, (i, s) in enumerate(chains):
            acc = acc_ref[c]
            heads[i][s] = acc[0:hd, :] * (1.0 / acc[hd:hd + 1, :])
        o_t = jnp.concatenate([jnp.concatenate(h, axis=1) for h in heads], axis=0)
        o_ref[:, queries] = o_t.astype(BF16)


def _attn(proj, q_t, v_t, feat_q_t, feat_k, batch, seq, k_col, tq=2048):
    assert seq % tq == 0 and tq % (ATT_SUB * ATT_UNROLL) == 0
    n_pairs = ATT_HEADS // HEADS_PER_BLOCK
    n_chains = HEADS_PER_BLOCK * (tq // ATT_SUB)
    return pl.pallas_call(
        functools.partial(_attn_kernel, tq=tq),
        grid=(batch, n_pairs),
        in_specs=[pl.BlockSpec((None, LANES, seq), lambda b, p: (b, p, 0)),
                  pl.BlockSpec((seq, LANES), lambda b, p: (b, k_col + p)),
                  pl.BlockSpec((None, LANES, seq), lambda b, p: (b, p, 0)),
                  pl.BlockSpec((None, LANES, seq), lambda b, p: (b, 0, 0)),
                  pl.BlockSpec((seq, LANES), lambda b, p: (b, 0))],
        out_specs=pl.BlockSpec((None, LANES, seq), lambda b, p: (b, p, 0)),
        out_shape=jax.ShapeDtypeStruct((batch, ATT_HEADS * ATT_HEAD_DIM, seq), BF16),
        scratch_shapes=[pltpu.VMEM((2, n_chains, ATT_TK, ATT_SUB), F32),
                        pltpu.VMEM((2, n_chains, 1, ATT_SUB), F32),
                        pltpu.VMEM((n_chains, ATT_VT_ROWS, ATT_SUB), F32),
                        pltpu.VMEM((n_chains, 1, ATT_SUB), F32)],
        compiler_params=pltpu.CompilerParams(
            dimension_semantics=("arbitrary", "arbitrary"), vmem_limit_bytes=VMEM_LIMIT_BYTES),
        name="attn",
    )(q_t, proj, v_t, feat_q_t, feat_k)


def _mix_tail_kernel(x_ref, ya_ref, mod_ref, wo_ref, anw_ref, g1_ref, b1_ref, w1_ref, w2_ref, g2_ref,
                     b2_ref, gate_ref, xs_ref, bc_ref, sm_ref, xh_ref, bch_ref, cw_ref, cb_ref,
                     brow_ref, alog_ref, dsk_ref, nw_ref, o_ref, ys_ref, st_ref, xf_ref, *,
                     alpha, ff_chunk, tiles_per_seq, n_tiles):
    i = pl.program_id(0)
    tm = x_ref.shape[0]
    slot = i % 2
    n_chunks = tm // CHUNK
    n_ff = w1_ref.shape[1] // ff_chunk
    assert n_chunks == n_ff
    chunk_rows = [slice(c * CHUNK, (c + 1) * CHUNK) for c in range(n_chunks)]
    width = xs_ref.shape[1]
    halo = xh_ref.shape[0]

    def step(with_ssd, with_mlp):
        ssd = []
        if with_ssd:
            seq_start = i % tiles_per_seq == 0
            before = jnp.concatenate([xh_ref[...], bch_ref[...]], axis=1).astype(F32)
            xf_ref[0:halo, :] = jnp.where(seq_start, 0.0, before)
            xf_ref[halo:, 0:width] = xs_ref[...].astype(F32)
            xf_ref[halo:, width:] = bc_ref[...].astype(F32)

            def conv_silu(c):
                conv = cb_ref[...]
                for k in range(CONV_WIDTH):
                    first = halo + c * CHUNK - k
                    conv = conv + (cw_ref[CONV_WIDTH - 1 - k:CONV_WIDTH - k, :]
                                   * xf_ref[first:first + CHUNK, :])
                return _silu(conv)

            ssd = [_ssd_rows(gate_ref[rows, :], functools.partial(conv_silu, c), sm_ref[rows, :],
                             seq_start if c == 0 else None, brow_ref, alog_ref, dsk_ref, nw_ref,
                             st_ref) for c, rows in enumerate(chunk_rows)]

        def advance(c, last=False):
            if 0 <= c < len(ssd):
                out = next(ssd[c])
                if last:
                    ys_ref[slot, chunk_rows[c], :] = out

        advance(0)
        if with_mlp:
            ssm_w = ys_ref.shape[2]
            ya = ya_ref[...].astype(F32).T
            ya = ya * lax.rsqrt(jnp.mean(ya * ya, axis=-1, keepdims=True) + RMS_EPS) * anw_ref[...]
            y = (jnp.dot(ys_ref[1 - slot], wo_ref[0:ssm_w, :], preferred_element_type=F32)
                 + jnp.dot(ya.astype(BF16), wo_ref[ssm_w:, :], preferred_element_type=F32))
            x1 = _layer_norm(alpha * x_ref[...] + (1.0 + mod_ref[2]) * y, g1_ref[...], b1_ref[...])
            h = (x1 * (1.0 + mod_ref[4]) + mod_ref[3]).astype(BF16)
            ff = jnp.zeros(x1.shape, F32)
        advance(0)
        for c in range(n_ff):
            if with_mlp:
                a = jnp.dot(h, w1_ref[:, c * ff_chunk:(c + 1) * ff_chunk],
                            preferred_element_type=F32)
                a = jnp.maximum(a, 0.0)
            advance(c, last=True)
            advance(c + 1)
            if with_mlp:
                ff = ff + jnp.dot((a * a).astype(BF16), w2_ref[c * ff_chunk:(c + 1) * ff_chunk, :],
                                  preferred_element_type=F32)
            advance(c + 1)
        if with_mlp:
            o_ref[...] = _layer_norm(alpha * x1 + (1.0 + mod_ref[5]) * ff, g2_ref[...], b2_ref[...])

    @pl.when(i == 0)
    def _():
        st_ref[...] = jnp.zeros(st_ref.shape, F32)
        step(True, False)

    @pl.when(jnp.logical_and(i > 0, i < n_tiles))
    def _():
        step(True, True)

    @pl.when(i == n_tiles)
    def _():
        step(False, True)


def _mix_tail(x2, proj, small, y_att_t, mod4, w_out, anw, g1, b1, w1, w2, g2, b2, conv_w, conv_b,
              brow, alog_row, dsk_full, ssm_norm_w, seq, alpha, tm=512, ff_chunk=1024):
    n_tok, d = x2.shape
    mix = w_out.shape[0]
    d_ff = w1.shape[1]
    width = SSM_HEADS * SSM_HEAD_DIM
    bc_w = 2 * SSM_GROUPS * SSM_STATE
    x_col, bc_col = 1, 2 * width // bc_w
    assert seq % tm == 0 and tm % CHUNK == 0 and d_ff % ff_chunk == 0
    tiles_per_seq = seq // tm
    n_tiles = n_tok // tm
    const = lambda i: (0, 0)
    resident = functools.partial(pl.BlockSpec, index_map=const, pipeline_mode=pl.Buffered(1))
    mlp_tile = lambda i: jnp.maximum(i - 1, 0)
    ssd_tile = lambda i: jnp.minimum(i, n_tiles - 1)
    halo = lambda i: jnp.maximum(ssd_tile(i) * (tm // BF16_SUBLANES) - 1, 0)
    return pl.pallas_call(
        functools.partial(_mix_tail_kernel, alpha=alpha, ff_chunk=ff_chunk,
                          tiles_per_seq=tiles_per_seq, n_tiles=n_tiles),
        grid=(n_tiles + 1,),
        in_specs=[pl.BlockSpec((tm, d), lambda i: (mlp_tile(i), 0)),
                  pl.BlockSpec((None, y_att_t.shape[1], tm),
                               lambda i: (mlp_tile(i) // tiles_per_seq, 0,
                                          mlp_tile(i) % tiles_per_seq)),
                  pl.BlockSpec((None, 6, 1, d), lambda i: (mlp_tile(i) // tiles_per_seq, 0, 0, 0)),
                  resident((mix, d)),
                  pl.BlockSpec((1, y_att_t.shape[1]), const),
                  pl.BlockSpec((1, d), const),
                  pl.BlockSpec((1, d), const),
                  resident((d, d_ff)),
                  resident((d_ff, d)),
                  pl.BlockSpec((1, d), const),
                  pl.BlockSpec((1, d), const),
                  pl.BlockSpec((tm, width), lambda i: (ssd_tile(i), 0)),
                  pl.BlockSpec((tm, width), lambda i: (ssd_tile(i), x_col)),
                  pl.BlockSpec((tm, bc_w), lambda i: (ssd_tile(i), bc_col)),
                  pl.BlockSpec((tm, LANES), lambda i: (ssd_tile(i), 0)),
                  pl.BlockSpec((BF16_SUBLANES, width), lambda i: (halo(i), x_col)),
                  pl.BlockSpec((BF16_SUBLANES, bc_w), lambda i: (halo(i), bc_col)),
                  pl.BlockSpec(conv_w.shape, const),
                  pl.BlockSpec(conv_b.shape, const),
                  pl.BlockSpec((1, LANES), const),
                  pl.BlockSpec((1, LANES), const),
                  pl.BlockSpec((1, width), const),
                  pl.BlockSpec((1, width), const)],
        out_specs=pl.BlockSpec((tm, d), lambda i: (mlp_tile(i), 0)),
        out_shape=jax.ShapeDtypeStruct((n_tok, d), F32),
        scratch_shapes=[pltpu.VMEM((2, tm, width), BF16),
                        pltpu.VMEM((SSM_GROUPS, SSM_STATE, HEADS_PER_GROUP * SSM_HEAD_DIM), F32),
                        pltpu.VMEM((BF16_SUBLANES + tm, width + bc_w), F32)],
        compiler_params=pltpu.CompilerParams(
            dimension_semantics=("arbitrary",), vmem_limit_bytes=VMEM_LIMIT_BYTES),
        name="mix_tail",
    )(x2, y_att_t, mod4, w_out, anw, g1, b1, w1, w2, g2, b2, proj, proj, proj, small, proj, proj,
      conv_w, conv_b, brow, alog_row, dsk_full, ssm_norm_w)


def _pad_lanes(v):
    return jnp.pad(v, ((0, 0), (0, LANES - v.shape[1])))


def kernel(x, c, w_ada, b_ada, w_in, conv_w, conv_b, dt_bias, a_log, d_skip, ssm_norm_w, f_bias,
           attn_norm_w, w_out, ln1_g, ln1_b, w_ff_in, w_ff_out, ln2_g, ln2_b):
    batch, seq, d = x.shape
    depth = w_ada.shape[0]
    alpha = (2.0 * depth) ** 0.25
    ssm_w = SSM_HEADS * SSM_HEAD_DIM
    att_w = ATT_HEADS * ATT_HEAD_DIM
    conv_dim = ssm_w + 2 * SSM_GROUPS * SSM_STATE
    o_xbc = ssm_w
    o_dt = o_xbc + conv_dim
    o_q = o_dt + SSM_HEADS
    o_k = o_q + att_w
    o_v = o_k + att_w
    o_f = o_v + att_w
    q_col = (ssm_w + conv_dim) // LANES
    k_col = q_col + att_w // LANES
    v_col = k_col + att_w // LANES

    x2 = x.reshape(batch * seq, d)
    c_pad = jnp.pad(c, ((0, SUBLANES - batch % SUBLANES if batch % SUBLANES else 0), (0, 0)))
    for l in range(depth):
        mod = _ada(c_pad, w_ada[l], b_ada[l][None, :])
        mod4 = mod[:batch].reshape(batch, 6, 1, d)
        wt = jnp.swapaxes(w_in[l], 0, 1).astype(BF16)
        wt_small = jnp.concatenate([wt[o_dt:o_q], wt[o_f:]], axis=0)
        wt_small = jnp.pad(wt_small, ((0, LANES - wt_small.shape[0]), (0, 0)))
        brow = _pad_lanes(jnp.concatenate([dt_bias[l], f_bias[l]])[None, :])
        proj, small, q_t, v_t, feat_q_t, feat_k = _inproj(
            x2, mod4, wt, wt_small, brow, batch, seq, v_col * LANES + att_w, o_dt, o_q - o_dt,
            ssm_w, q_col * LANES, k_col * LANES, v_col * LANES, ATT_HEAD_DIM ** -0.5 * LOG2E)
        y_att_t = _attn(proj, q_t, v_t, feat_q_t, feat_k, batch, seq, q_col)

        alog_row = _pad_lanes(a_log[l][None, :])
        dsk_full = jnp.repeat(d_skip[l], SSM_HEAD_DIM)[None, :]
        x2 = _mix_tail(x2, proj, small, y_att_t, mod4, w_out[l].astype(BF16),
                       attn_norm_w[l][None, :], ln1_g[l][None, :], ln1_b[l][None, :],
                       w_ff_in[l].astype(BF16), w_ff_out[l].astype(BF16), ln2_g[l][None, :],
                       ln2_b[l][None, :], conv_w[l], conv_b[l][None, :], brow, alog_row, dsk_full,
                       ssm_norm_w[l][None, :], seq, alpha)
    return x2.reshape(batch, seq, d)
```
